```python
import math
import jax, jax.numpy as jnp
from jax import lax
import numpy as np

D_MODEL = 1024
BATCH = 4
SEQ = 8192
DEPTH = 4

CHUNK = 64
Q_BLOCK = 128
N_A_LAYERS = DEPTH // 2
N_B_LAYERS = DEPTH - N_A_LAYERS
A_HEADS = 8
A_NOPE = 128
A_ROPE = 64
A_VDIM = 128
A_QK = A_NOPE + A_ROPE
A_Q_LORA = 384
A_KV_LORA = 256
A_WIDTH = A_HEADS * A_VDIM
A_IN = A_Q_LORA + A_KV_LORA + A_ROPE + A_WIDTH
ROPE_THETA = 10000.0
B_Q_HEADS = 16
B_KV_HEADS = 2
B_GROUP = B_Q_HEADS // B_KV_HEADS
B_HEAD_DIM = 64
B_WIDTH = B_Q_HEADS * B_HEAD_DIM
B_KV_WIDTH = B_KV_HEADS * B_HEAD_DIM
WINDOW = 128
WIN_CHUNKS = WINDOW // CHUNK
NUM_BUCKETS = 32
MAX_DISTANCE = 128
DEEPNORM_ALPHA = (2 * DEPTH) ** 0.25
DEEPNORM_BETA = (8 * DEPTH) ** -0.25
NORM_EPS = 1e-5
MASK_VALUE = -1e30

kernel_name = "yoco_mla_swa_sink_deepnorm"


def _rms_norm(x, g):
    xf = x.astype(jnp.float32)
    y = xf * lax.rsqrt(jnp.mean(xf * xf, axis=-1, keepdims=True) + NORM_EPS)
    return (y * g.astype(jnp.float32)).astype(x.dtype)


def _layer_norm(x, g, b):
    xf = x.astype(jnp.float32)
    mu = jnp.mean(xf, axis=-1, keepdims=True)
    var = jnp.mean(jnp.square(xf - mu), axis=-1, keepdims=True)
    y = (xf - mu) * lax.rsqrt(var + NORM_EPS) * g.astype(jnp.float32) + b.astype(jnp.float32)
    return y.astype(x.dtype)


def _rope(t, seq_len):
    half = A_ROPE // 2
    inv = ROPE_THETA ** (-jnp.arange(half, dtype=jnp.float32) / half)
    ang = jnp.arange(seq_len, dtype=jnp.float32)[:, None] * inv[None, :]
    cos, sin = jnp.cos(ang)[:, None, :], jnp.sin(ang)[:, None, :]
    tf = t.astype(jnp.float32)
    t1, t2 = tf[..., :half], tf[..., half:]
    return jnp.concatenate([t1 * cos - t2 * sin, t2 * cos + t1 * sin], axis=-1).astype(t.dtype)


def _mla_attention(q_nope, q_rope, k_nope, k_rope, v):
    B, S, H, _ = q_nope.shape
    nb = S // Q_BLOCK
    scale = A_QK ** -0.5
    key_chunk = jnp.arange(S) // CHUNK

    def blockify(t):
        return jnp.moveaxis(t.reshape((B, nb, Q_BLOCK) + t.shape[2:]), 1, 0)

    def one_block(args):
        qn, qr, blk = args
        s = (jnp.einsum('bqhd,bkhd->bhqk', qn, k_nope, preferred_element_type=jnp.float32)
             + jnp.einsum('bqhd,bkd->bhqk', qr, k_rope, preferred_element_type=jnp.float32)) * scale
        q_chunk = (blk * Q_BLOCK + jnp.arange(Q_BLOCK)) // CHUNK
        visible = key_chunk[None, :] <= q_chunk[:, None]
        p = jax.nn.softmax(jnp.where(visible, s, MASK_VALUE), axis=-1)
        return jnp.einsum('bhqk,bkhd->bqhd', p.astype(v.dtype), v)

    o = lax.map(one_block, (blockify(q_nope), blockify(q_rope), jnp.arange(nb)))
    return jnp.moveaxis(o, 0, 1).reshape(B, S, H, v.shape[-1])


def _mla_mixer(x, w_in, q_norm, kv_norm, w_uq, w_ukv, w_o):
    B, S, _ = x.shape
    h = x @ w_in
    c_q, c_kv, k_rope, gate = jnp.split(
        h, [A_Q_LORA, A_Q_LORA + A_KV_LORA, A_Q_LORA + A_KV_LORA + A_ROPE], axis=-1)
    q = (_rms_norm(c_q, q_norm) @ w_uq).reshape(B, S, A_HEADS, A_QK)
    kv = (_rms_norm(c_kv, kv_norm) @ w_ukv).reshape(B, S, A_HEADS, A_NOPE + A_VDIM)
    q_nope, q_rope = q[..., :A_NOPE], _rope(q[..., A_NOPE:], S)
    k_rope = _rope(k_rope[:, :, None, :], S)[:, :, 0, :]
    k_nope, v = kv[..., :A_NOPE], kv[..., A_NOPE:]
    o = _mla_attention(q_nope, q_rope, k_nope, k_rope, v).reshape(B, S, A_WIDTH)
    return (o * jax.nn.silu(gate)) @ w_o


def _t5_bucket(rel):
    half = NUM_BUCKETS // 2
    ret = jnp.where(rel > 0, half, 0)
    n = jnp.abs(rel)
    max_exact = half // 2
    large = max_exact + (jnp.log(jnp.maximum(n, 1).astype(jnp.float32) / max_exact)
                         / math.log(MAX_DISTANCE / max_exact) * (half - max_exact)).astype(jnp.int32)
    large = jnp.minimum(large, half - 1)
    return ret + jnp.where(n < max_exact, n, large)


def _band_structure(seq_len):
    nb = seq_len // Q_BLOCK
    r = jnp.arange(Q_BLOCK)
    j = jnp.arange(2 * Q_BLOCK)
    rel = j[None, :] - Q_BLOCK - r[:, None]
    d = (r // CHUNK)[:, None] - ((j - Q_BLOCK) // CHUNK)[None, :]
    rel_ok = (d >= 0) & (d <= WIN_CHUNKS)
    kpos = jnp.arange(nb)[:, None] * Q_BLOCK - Q_BLOCK + j[None, :]
    mask = rel_ok[None, :, :] & (kpos >= 0)[:, None, :]
    return rel, mask


def _band(t):
    B, S = t.shape[:2]
    nb = S // Q_BLOCK
    tp = jnp.pad(t, ((0, 0), (Q_BLOCK, 0), (0, 0), (0, 0))).reshape((B, nb + 1, Q_BLOCK) + t.shape[2:])
    return jnp.concatenate([tp[:, :-1], tp[:, 1:]], axis=2)


def _swa_sink_mixer(x, k_band, v_band, band_bias, band_mask, w_in, sinks, w_o):
    B, S, _ = x.shape
    nb = S // Q_BLOCK
    h = x @ w_in
    q, gate = jnp.split(h, [B_WIDTH], axis=-1)
    q = q.reshape(B, nb, Q_BLOCK, B_KV_HEADS, B_GROUP, B_HEAD_DIM)
    s = jnp.einsum('bnqhgd,bnkhd->bnhgqk', q, k_band, preferred_element_type=jnp.float32)
    s = s * (B_HEAD_DIM ** -0.5) + band_bias
    s = jnp.where(band_mask[None, :, None, None, :, :], s, MASK_VALUE)
    sink = sinks.astype(jnp.float32).reshape(1, 1, B_KV_HEADS, B_GROUP, 1, 1)
    m = jnp.maximum(jnp.max(s, axis=-1, keepdims=True), sink)
    e = jnp.exp(s - m)
    p = e / (jnp.sum(e, axis=-1, keepdims=True) + jnp.exp(sink - m))
    o = jnp.einsum('bnhgqk,bnkhd->bnqhgd', p.astype(v_band.dtype), v_band).reshape(B, S, B_WIDTH)
    return (o * jax.nn.silu(gate)) @ w_o


def setup_inputs(seed: int = 0) -> dict:
    key = jax.random.key(seed)
    ks = jax.random.split(key, 14)
    f32 = jnp.float32
    nrm = lambda k, shape, scale: jax.random.normal(k, shape, f32) * scale
    return {
        "x": nrm(ks[0], (BATCH, SEQ, D_MODEL), 1.0),
        "w_in_a": nrm(ks[1], (N_A_LAYERS, D_MODEL, A_IN), D_MODEL ** -0.5),
        "q_norm_a": 1.0 + nrm(ks[2], (N_A_LAYERS, A_Q_LORA), 0.02),
        "kv_norm_a": 1.0 + nrm(ks[3], (N_A_LAYERS, A_KV_LORA), 0.02),
        "w_uq_a": nrm(ks[4], (N_A_LAYERS, A_Q_LORA, A_HEADS * A_QK), A_Q_LORA ** -0.5),
        "w_ukv_a": nrm(ks[5], (N_A_LAYERS, A_KV_LORA, A_HEADS * (A_NOPE + A_VDIM)), A_KV_LORA ** -0.5),
        "w_o_a": nrm(ks[6], (N_A_LAYERS, A_WIDTH, D_MODEL), A_WIDTH ** -0.5 * DEEPNORM_BETA),
        "w_kv_shared": nrm(ks[7], (D_MODEL, 2 * B_KV_WIDTH), D_MODEL ** -0.5),
        "w_in_b": nrm(ks[8], (N_B_LAYERS, D_MODEL, 2 * B_WIDTH), D_MODEL ** -0.5),
        "sinks_b": nrm(ks[9], (N_B_LAYERS, B_Q_HEADS), 0.5),
        "w_o_b": nrm(ks[10], (N_B_LAYERS, B_WIDTH, D_MODEL), B_WIDTH ** -0.5 * DEEPNORM_BETA),
        "rel_bias_table": nrm(ks[11], (NUM_BUCKETS, B_Q_HEADS), 0.2),
        "ln_gain": 1.0 + nrm(ks[12], (DEPTH, D_MODEL), 0.02),
        "ln_bias": nrm(ks[13], (DEPTH, D_MODEL), 0.02),
    }


def reference(x, w_in_a, q_norm_a, kv_norm_a, w_uq_a, w_ukv_a, w_o_a, w_kv_shared,
              w_in_b, sinks_b, w_o_b, rel_bias_table, ln_gain, ln_bias):
    S = x.shape[1]
    rel, band_mask = _band_structure(S)
    band_bias = jnp.transpose(rel_bias_table[_t5_bucket(rel)], (2, 0, 1)).astype(jnp.float32)
    band_bias = band_bias.reshape(B_KV_HEADS, B_GROUP, Q_BLOCK, 2 * Q_BLOCK)
    k_band = v_band = None
    for layer in range(DEPTH):
        if layer < N_A_LAYERS:
            i = layer
            y = _mla_mixer(x, w_in_a[i], q_norm_a[i], kv_norm_a[i], w_uq_a[i], w_ukv_a[i], w_o_a[i])
        else:
            j = layer - N_A_LAYERS
            y = _swa_sink_mixer(x, k_band, v_band, band_bias, band_mask, w_in_b[j], sinks_b[j], w_o_b[j])
        x = _layer_norm(DEEPNORM_ALPHA * x + y, ln_gain[layer], ln_bias[layer])
        if layer == N_A_LAYERS - 1:
            B = x.shape[0]
            k, v = jnp.split(x @ w_kv_shared, 2, axis=-1)
            k_band = _band(k.reshape(B, S, B_KV_HEADS, B_HEAD_DIM))
            v_band = _band(v.reshape(B, S, B_KV_HEADS, B_HEAD_DIM))
    return x
```

```python
import functools
import math

import jax
import jax.numpy as jnp
from jax import lax
from jax.experimental import pallas as pl
from jax.experimental.pallas import tpu as pltpu

F32 = jnp.float32
BF16 = jnp.bfloat16

D_MODEL = 1024
DEPTH = 4
CHUNK = 64
Q_BLOCK = 128
N_A_LAYERS = DEPTH // 2
A_HEADS = 8
A_NOPE = 128
A_ROPE = 64
A_VDIM = 128
A_QK = A_NOPE + A_ROPE
A_Q_LORA = 384
A_KV_LORA = 256
A_WIDTH = A_HEADS * A_VDIM
A_QPAD = 256
ROPE_THETA = 10000.0
ROPE_HALF = A_ROPE // 2
B_Q_HEADS = 16
B_KV_HEADS = 2
B_GROUP = B_Q_HEADS // B_KV_HEADS
B_HEAD_DIM = 64
B_WIDTH = B_Q_HEADS * B_HEAD_DIM
B_KV_WIDTH = B_KV_HEADS * B_HEAD_DIM
WINDOW = 128
WIN_CHUNKS = WINDOW // CHUNK
NUM_BUCKETS = 32
MAX_DISTANCE = 128
DEEPNORM_ALPHA = (2 * DEPTH) ** 0.25
NORM_EPS = 1e-5
MASK_VALUE = -1e30

LANES = 128
VMEM_LIMIT_BYTES = 56 * 1024 * 1024

_NT = (((1,), (1,)), ((), ()))


def _cparams(n_axes):
    return pltpu.CompilerParams(
        dimension_semantics=("arbitrary",) * n_axes, vmem_limit_bytes=VMEM_LIMIT_BYTES)


def _sigmoid(x):
    return 1.0 / (1.0 + jnp.exp(-x))


def _rms_scale(c, g):
    return c * lax.rsqrt(jnp.mean(c * c, axis=-1, keepdims=True) + NORM_EPS) * g


def _deepnorm_ln(x, y, g, b):
    z = DEEPNORM_ALPHA * x + y
    mu = jnp.mean(z, axis=-1, keepdims=True)
    zc = z - mu
    var = jnp.mean(zc * zc, axis=-1, keepdims=True)
    return zc * lax.rsqrt(var + NORM_EPS) * g + b


def _mla_proj_kernel(x_ref, w_in_ref, gq_ref, gkv_ref, w_uqT_ref, w_kn_ref, w_vT_ref,
                     cosT_ref, sinT_ref, cosk_ref, sinka_ref, sinkb_ref,
                     qT_ref, kn_ref, kr_ref, vT_ref, sg_ref):
    xb = x_ref[...].astype(BF16)
    h = jnp.dot(xb, w_in_ref[...], preferred_element_type=F32)
    o_kv = A_Q_LORA
    o_kr = A_Q_LORA + A_KV_LORA
    o_g = o_kr + LANES
    cq = _rms_scale(h[:, :o_kv], gq_ref[...]).astype(BF16)
    ckv = _rms_scale(h[:, o_kv:o_kr], gkv_ref[...]).astype(BF16)

    scale = A_QK ** -0.5
    qT = lax.dot_general(w_uqT_ref[...], cq, _NT, preferred_element_type=F32) * scale
    cos = cosT_ref[...]
    sin = sinT_ref[...]
    for hh in range(A_HEADS):
        r0 = hh * A_QPAD
        r1 = r0 + A_NOPE
        r2 = r1 + ROPE_HALF
        r3 = r2 + ROPE_HALF
        t1 = qT[r1:r2]
        t2 = qT[r2:r3]
        qT_ref[r0:r1, :] = qT[r0:r1].astype(BF16)
        qT_ref[r1:r2, :] = (t1 * cos - t2 * sin).astype(BF16)
        qT_ref[r2:r3, :] = (t2 * cos + t1 * sin).astype(BF16)
        qT_ref[r3:r0 + A_QPAD, :] = jnp.zeros((A_QPAD - A_QK, qT.shape[1]), BF16)

    krs = h[:, o_kr:o_g]
    kr = (krs * cosk_ref[...]
          + pltpu.roll(krs, LANES - ROPE_HALF, axis=1) * sinka_ref[...]
          + pltpu.roll(krs, ROPE_HALF, axis=1) * sinkb_ref[...])
    kr_ref[...] = kr.astype(BF16)

    kn_ref[...] = jnp.dot(ckv, w_kn_ref[...], preferred_element_type=F32).astype(BF16)
    vT_ref[...] = lax.dot_general(w_vT_ref[...], ckv, _NT, preferred_element_type=F32).astype(BF16)

    gate = h[:, o_g:]
    sg_ref[...] = (gate * _sigmoid(gate)).astype(BF16)


def _mla_proj(x, w_in, gq, gkv, w_uqT, w_kn, w_vT, rope, tm):
    B, S, _ = x.shape
    nt = S // tm
    cosT, sinT, cosk, sinka, sinkb = rope
    const = lambda *shape: pl.BlockSpec(shape, lambda b, i: (0,) * len(shape))
    return pl.pallas_call(
        _mla_proj_kernel,
        grid=(B, nt),
        in_specs=[
            pl.BlockSpec((None, tm, D_MODEL), lambda b, i: (b, i, 0)),
            const(*w_in.shape), const(*gq.shape), const(*gkv.shape),
            const(*w_uqT.shape), const(*w_kn.shape), const(*w_vT.shape),
            pl.BlockSpec((ROPE_HALF, tm), lambda b, i: (0, i)),
            pl.BlockSpec((ROPE_HALF, tm), lambda b, i: (0, i)),
            pl.BlockSpec((tm, LANES), lambda b, i: (i, 0)),
            pl.BlockSpec((tm, LANES), lambda b, i: (i, 0)),
            pl.BlockSpec((tm, LANES), lambda b, i: (i, 0)),
        ],
        out_specs=[
            pl.BlockSpec((None, A_HEADS * A_QPAD, tm), lambda b, i: (b, 0, i)),
            pl.BlockSpec((None, tm, A_HEADS * A_NOPE), lambda b, i: (b, i, 0)),
            pl.BlockSpec((None, tm, LANES), lambda b, i: (b, i, 0)),
            pl.BlockSpec((None, None, A_WIDTH, tm), lambda b, i: (b, i, 0, 0)),
            pl.BlockSpec((None, tm, A_WIDTH), lambda b, i: (b, i, 0)),
        ],
        out_shape=[
            jax.ShapeDtypeStruct((B, A_HEADS * A_QPAD, S), BF16),
            jax.ShapeDtypeStruct((B, S, A_HEADS * A_NOPE), BF16),
            jax.ShapeDtypeStruct((B, S, LANES), BF16),
            jax.ShapeDtypeStruct((B, nt, A_WIDTH, tm), BF16),
            jax.ShapeDtypeStruct((B, S, A_WIDTH), BF16),
        ],
        compiler_params=_cparams(2),
        name="mla_proj",
    )(x, w_in, gq, gkv, w_uqT, w_kn, w_vT, cosT, sinT, cosk, sinka, sinkb)


def _mla_attn_kernel(qT_ref, kn_ref, kr_ref, vT_ref, o_ref, m_ref, l_ref, acc_ref, *, tq):
    qi = pl.program_id(2)
    qT = qT_ref[...]

    m_ref[...] = jnp.full(m_ref.shape, MASK_VALUE, F32)
    l_ref[...] = jnp.zeros(l_ref.shape, F32)
    acc_ref[...] = jnp.zeros(acc_ref.shape, F32)

    def tile(j, diagonal):
        k0 = pl.multiple_of(j * tq, tq)
        kcat = jnp.concatenate([kn_ref[pl.ds(k0, tq), :], kr_ref[pl.ds(k0, tq), :]], axis=1)
        sT = jnp.dot(kcat, qT, preferred_element_type=F32)
        if diagonal:
            kc = lax.broadcasted_iota(jnp.int32, sT.shape, 0) // CHUNK
            qc = lax.broadcasted_iota(jnp.int32, sT.shape, 1) // CHUNK
            sT = jnp.where(kc <= qc, sT, MASK_VALUE)
        m_old = m_ref[...]
        m_new = jnp.maximum(m_old, jnp.max(sT, axis=0, keepdims=True))
        alpha = jnp.exp(m_old - m_new)
        pT = jnp.exp(sT - m_new)
        l_ref[...] = alpha * l_ref[...] + jnp.sum(pT, axis=0, keepdims=True)
        acc_ref[...] = alpha * acc_ref[...] + jnp.dot(
            vT_ref[j], pT.astype(BF16), preferred_element_type=F32)
        m_ref[...] = m_new

    def body(j, carry):
        tile(j, False)
        return carry

    lax.fori_loop(0, qi, body, 0)
    tile(qi, True)
    o_ref[...] = (acc_ref[...] / l_ref[...]).T.astype(o_ref.dtype)


def _mla_attn(qT, kn, kr, vT, tq):
    B, _, S = qT.shape
    nt = S // tq
    return pl.pallas_call(
        functools.partial(_mla_attn_kernel, tq=tq),
        grid=(B, A_HEADS, nt),
        in_specs=[
            pl.BlockSpec((None, A_QPAD, tq), lambda b, h, i: (b, h, i)),
            pl.BlockSpec((None, S, A_NOPE), lambda b, h, i: (b, 0, h)),
            pl.BlockSpec((None, S, LANES), lambda b, h, i: (b, 0, 0)),
            pl.BlockSpec((None, nt, A_VDIM, tq), lambda b, h, i: (b, 0, h, 0)),
        ],
        out_specs=pl.BlockSpec((None, tq, A_VDIM), lambda b, h, i: (b, i, h)),
        out_shape=jax.ShapeDtypeStruct((B, S, A_WIDTH), BF16),
        scratch_shapes=[
            pltpu.VMEM((1, tq), F32),
            pltpu.VMEM((1, tq), F32),
            pltpu.VMEM((A_VDIM, tq), F32),
        ],
        compiler_params=_cparams(3),
        name="mla_attn",
    )(qT, kn, kr, vT)


def _out_ln_kernel(o_ref, sg_ref, x_ref, wo_ref, g_ref, b_ref, xo_ref):
    og = o_ref[...] * sg_ref[...]
    y = jnp.dot(og, wo_ref[...], preferred_element_type=F32)
    xo_ref[...] = _deepnorm_ln(x_ref[...], y, g_ref[...], b_ref[...])


def _out_ln_kv_kernel(o_ref, sg_ref, x_ref, wo_ref, g_ref, b_ref, wk_ref, wvT_ref,
                      xo_ref, ks_ref, vsT_ref):
    og = o_ref[...] * sg_ref[...]
    y = jnp.dot(og, wo_ref[...], preferred_element_type=F32)
    xn = _deepnorm_ln(x_ref[...], y, g_ref[...], b_ref[...])
    xo_ref[...] = xn
    xb = xn.astype(BF16)
    ks_ref[...] = jnp.dot(xb, wk_ref[...], preferred_element_type=F32).astype(BF16)
    vT = lax.dot_general(wvT_ref[...], xb, _NT, preferred_element_type=F32).astype(BF16)
    for n in range(vsT_ref.shape[0]):
        vsT_ref[n] = vT[:, n * Q_BLOCK:(n + 1) * Q_BLOCK]


def _out_ln(o, sg, x, wo, g, b, tm, kv_weights=None):
    B, S, _ = x.shape
    row = pl.BlockSpec((None, tm, D_MODEL), lambda b_, i: (b_, i, 0))
    const = lambda *shape: pl.BlockSpec(shape, lambda b_, i: (0,) * len(shape))
    in_specs = [row, row, row, const(*wo.shape), const(*g.shape), const(*b.shape)]
    if kv_weights is None:
        return pl.pallas_call(
            _out_ln_kernel, grid=(B, S // tm), in_specs=in_specs, out_specs=row,
            out_shape=jax.ShapeDtypeStruct(x.shape, F32),
            compiler_params=_cparams(2), name="out_ln",
        )(o, sg, x, wo, g, b)
    wk, wvT = kv_weights
    nb = tm // Q_BLOCK
    return pl.pallas_call(
        _out_ln_kv_kernel, grid=(B, S // tm),
        in_specs=in_specs + [const(*wk.shape), const(*wvT.shape)],
        out_specs=[
            row,
            pl.BlockSpec((None, tm, B_KV_WIDTH), lambda b_, i: (b_, i, 0)),
            pl.BlockSpec((None, nb, B_KV_WIDTH, Q_BLOCK), lambda b_, i: (b_, i, 0, 0)),
        ],
        out_shape=[
            jax.ShapeDtypeStruct(x.shape, F32),
            jax.ShapeDtypeStruct((B, S, B_KV_WIDTH), BF16),
            jax.ShapeDtypeStruct((B, S // Q_BLOCK, B_KV_WIDTH, Q_BLOCK), BF16),
        ],
        compiler_params=_cparams(2), name="out_ln_kv",
    )(o, sg, x, wo, g, b, wk, wvT)


def _band_bias_kernel(tab_ref, bkt_ref, ok_ref, out_ref):
    bkt = bkt_ref[...]
    ok = ok_ref[...] != 0
    has_prev = lax.broadcasted_iota(jnp.int32, bkt.shape, 0) >= Q_BLOCK
    for h in range(B_Q_HEADS):
        acc = jnp.zeros(bkt.shape, F32)
        for bucket in range(NUM_BUCKETS):
            acc = jnp.where(bkt == bucket, tab_ref[bucket, h], acc)
        generic = jnp.where(ok, acc, MASK_VALUE)
        g, hi = divmod(h, B_GROUP)
        out_ref[0, g, :, hi * Q_BLOCK:(hi + 1) * Q_BLOCK] = jnp.where(has_prev, generic, MASK_VALUE)
        out_ref[1, g, :, hi * Q_BLOCK:(hi + 1) * Q_BLOCK] = generic


def _t5_bucket_map(rel):
    half = NUM_BUCKETS // 2
    ret = jnp.where(rel > 0, half, 0)
    n = jnp.abs(rel)
    max_exact = half // 2
    large = max_exact + (jnp.log(jnp.maximum(n, 1).astype(F32) / max_exact)
                         / math.log(MAX_DISTANCE / max_exact) * (half - max_exact)).astype(jnp.int32)
    large = jnp.minimum(large, half - 1)
    return ret + jnp.where(n < max_exact, n, large)


def _band_bias(rel_bias_table):
    j = jnp.arange(2 * Q_BLOCK)[:, None]
    r = jnp.arange(Q_BLOCK)[None, :]
    rel = j - Q_BLOCK - r
    d = (r // CHUNK) - ((j - Q_BLOCK) // CHUNK)
    ok = ((d >= 0) & (d <= WIN_CHUNKS)).astype(jnp.int32)
    bkt = _t5_bucket_map(rel).astype(jnp.int32)
    return pl.pallas_call(
        _band_bias_kernel,
        in_specs=[
            pl.BlockSpec(memory_space=pltpu.SMEM),
            pl.BlockSpec(memory_space=pltpu.VMEM),
            pl.BlockSpec(memory_space=pltpu.VMEM),
        ],
        out_specs=pl.BlockSpec(memory_space=pltpu.VMEM),
        out_shape=jax.ShapeDtypeStruct((2, B_KV_HEADS, 2 * Q_BLOCK, B_GROUP * Q_BLOCK), F32),
        name="band_bias",
    )(rel_bias_table, bkt, ok)


def _swa_layer_kernel(x_ref, wqT_ref, wg_ref, wo_ref, ks_ref, vsT_ref, bias_ref, sink_ref,
                      g_ref, b_ref, xo_ref, og_ref, *, nblk):
    i = pl.program_id(1)
    x = x_ref[...]
    xb = x.astype(BF16)
    qT = (lax.dot_general(wqT_ref[...], xb, _NT, preferred_element_type=F32)
          * (B_HEAD_DIM ** -0.5)).astype(BF16)
    gate = jnp.dot(xb, wg_ref[...], preferred_element_type=F32)
    sg = gate * _sigmoid(gate)
    zeros = jnp.zeros((B_HEAD_DIM, B_GROUP * Q_BLOCK), BF16)

    for n in range(nblk):
        blk = i * nblk + n
        cur = pl.multiple_of(blk * Q_BLOCK, Q_BLOCK)
        prev_blk = jnp.maximum(blk - 1, 0)
        prev = pl.multiple_of(prev_blk * Q_BLOCK, Q_BLOCK)
        kband = jnp.concatenate([ks_ref[pl.ds(prev, Q_BLOCK), :], ks_ref[pl.ds(cur, Q_BLOCK), :]], axis=0)
        vband = jnp.concatenate([vsT_ref[prev_blk], vsT_ref[blk]], axis=1)
        table = jnp.minimum(blk, 1)
        lanes = slice(n * Q_BLOCK, (n + 1) * Q_BLOCK)
        o_parts = []
        for g in range(B_KV_HEADS):
            qg = jnp.concatenate(
                [qT[(g * B_GROUP + hi) * B_HEAD_DIM:(g * B_GROUP + hi + 1) * B_HEAD_DIM, lanes]
                 for hi in range(B_GROUP)], axis=1)
            rhs = jnp.concatenate([qg, zeros] if g == 0 else [zeros, qg], axis=0)
            sT = jnp.dot(kband, rhs, preferred_element_type=F32) + bias_ref[table, g]
            sink = sink_ref[g]
            m = jnp.maximum(jnp.max(sT, axis=0, keepdims=True), sink)
            e = jnp.exp(sT - m)
            den = jnp.sum(e, axis=0, keepdims=True) + jnp.exp(sink - m)
            v_rows = slice(g * B_HEAD_DIM, (g + 1) * B_HEAD_DIM)
            oT = jnp.dot(vband[v_rows, :], e.astype(BF16), preferred_element_type=F32)
            o_parts.append(oT / den)
        for hi in range(B_GROUP):
            cols = slice(hi * Q_BLOCK, (hi + 1) * Q_BLOCK)
            o_hi = jnp.concatenate([o_parts[0][:, cols], o_parts[1][:, cols]], axis=0).T
            og_ref[lanes, cols] = (o_hi * sg[lanes, cols]).astype(BF16)

    y = jnp.dot(og_ref[...], wo_ref[...], preferred_element_type=F32)
    xo_ref[...] = _deepnorm_ln(x, y, g_ref[...], b_ref[...])


def _swa_layer(x, wqT, wg, wo, ks, vsT, bias, sink, g, b, tm):
    B, S, _ = x.shape
    nblk = tm // Q_BLOCK
    row = pl.BlockSpec((None, tm, D_MODEL), lambda b_, i: (b_, i, 0))
    const = lambda *shape: pl.BlockSpec(shape, lambda b_, i: (0,) * len(shape))
    return pl.pallas_call(
        functools.partial(_swa_layer_kernel, nblk=nblk),
        grid=(B, S // tm),
        in_specs=[
            row, const(*wqT.shape), const(*wg.shape), const(*wo.shape),
            pl.BlockSpec((None, S, B_KV_WIDTH), lambda b_, i: (b_, 0, 0)),
            pl.BlockSpec((None, S // Q_BLOCK, B_KV_WIDTH, Q_BLOCK), lambda b_, i: (b_, 0, 0, 0)),
            const(*bias.shape), const(*sink.shape), const(*g.shape), const(*b.shape),
        ],
        out_specs=row,
        out_shape=jax.ShapeDtypeStruct(x.shape, F32),
        scratch_shapes=[pltpu.VMEM((tm, B_WIDTH), BF16)],
        compiler_params=_cparams(2),
        name="swa_layer",
    )(x, wqT, wg, wo, ks, vsT, bias, sink, g, b)


def _rope_tables(S):
    inv = ROPE_THETA ** (-jnp.arange(ROPE_HALF, dtype=F32) / ROPE_HALF)
    ang = jnp.arange(S, dtype=F32)[:, None] * inv[None, :]
    cos, sin = jnp.cos(ang), jnp.sin(ang)
    zeros = jnp.zeros((S, LANES - A_ROPE), F32)
    z32 = jnp.zeros((S, ROPE_HALF), F32)
    cosk = jnp.concatenate([cos, cos, zeros], axis=1)
    sinka = jnp.concatenate([-sin, z32, zeros], axis=1)
    sinkb = jnp.concatenate([z32, sin, zeros], axis=1)
    return cos.T, sin.T, cosk, sinka, sinkb


def _mla_weights(w_in, w_uq, w_ukv):
    o_kr = A_Q_LORA + A_KV_LORA
    w_in_pad = jnp.concatenate(
        [w_in[:, :o_kr + A_ROPE], jnp.zeros((D_MODEL, LANES - A_ROPE), w_in.dtype),
         w_in[:, o_kr + A_ROPE:]], axis=1).astype(BF16)
    w_uq_h = w_uq.reshape(A_Q_LORA, A_HEADS, A_QK)
    w_uq_pad = jnp.pad(w_uq_h, ((0, 0), (0, 0), (0, A_QPAD - A_QK))).reshape(A_Q_LORA, A_HEADS * A_QPAD)
    w_ukv_h = w_ukv.reshape(A_KV_LORA, A_HEADS, A_NOPE + A_VDIM)
    w_kn = w_ukv_h[:, :, :A_NOPE].reshape(A_KV_LORA, A_HEADS * A_NOPE)
    w_v = w_ukv_h[:, :, A_NOPE:].reshape(A_KV_LORA, A_WIDTH)
    return w_in_pad, w_uq_pad.T.astype(BF16), w_kn.astype(BF16), w_v.T.astype(BF16)


def _swa_feature_perm():
    hi, g, d = jnp.meshgrid(jnp.arange(B_GROUP), jnp.arange(B_KV_HEADS), jnp.arange(B_HEAD_DIM),
                            indexing="ij")
    return ((g * B_GROUP + hi) * B_HEAD_DIM + d).reshape(-1)


def kernel(x, w_in_a, q_norm_a, kv_norm_a, w_uq_a, w_ukv_a, w_o_a, w_kv_shared, w_in_b, sinks_b,
           w_o_b, rel_bias_table, ln_gain, ln_bias):
    B, S, _ = x.shape
    tm = min(512, S)
    rope = _rope_tables(S)

    ks = vsT = None
    for i in range(N_A_LAYERS):
        w_in, w_uqT, w_kn, w_vT = _mla_weights(w_in_a[i], w_uq_a[i], w_ukv_a[i])
        qT, kn, kr, vT, sg = _mla_proj(
            x, w_in, q_norm_a[i][None, :], kv_norm_a[i][None, :], w_uqT, w_kn, w_vT, rope, tm)
        o = _mla_attn(qT, kn, kr, vT, tm)
        g, b = ln_gain[i][None, :], ln_bias[i][None, :]
        wo = w_o_a[i].astype(BF16)
        if i < N_A_LAYERS - 1:
            x = _out_ln(o, sg, x, wo, g, b, tm)
        else:
            x, ks, vsT = _out_ln(o, sg, x, wo, g, b, tm,
                                 kv_weights=(w_kv_shared[:, :B_KV_WIDTH].astype(BF16),
                                             w_kv_shared[:, B_KV_WIDTH:].T.astype(BF16)))

    bias = _band_bias(rel_bias_table)
    perm = _swa_feature_perm()
    for j in range(DEPTH - N_A_LAYERS):
        layer = N_A_LAYERS + j
        wqT = w_in_b[j][:, :B_WIDTH].T.astype(BF16)
        wg = w_in_b[j][:, B_WIDTH:][:, perm].astype(BF16)
        wo = w_o_b[j][perm, :].astype(BF16)
        sink = jnp.repeat(sinks_b[j].astype(F32).reshape(B_KV_HEADS, 1, B_GROUP), Q_BLOCK, axis=2)
        x = _swa_layer(x, wqT, wg, wo, ks, vsT, bias, sink,
                       ln_gain[layer][None, :], ln_bias[layer][None, :], min(256, S))
    return x
```

```python
import functools
import math

import jax
import jax.numpy as jnp
from jax import lax
from jax.experimental import pallas as pl
from jax.experimental.pallas import tpu as pltpu

F32 = jnp.float32
BF16 = jnp.bfloat16

D_MODEL = 1024
DEPTH = 4
CHUNK = 64
Q_BLOCK = 128
N_A_LAYERS = DEPTH // 2
A_HEADS = 8
A_NOPE = 128
A_ROPE = 64
A_VDIM = 128
A_QK = A_NOPE + A_ROPE
A_Q_LORA = 384
A_KV_LORA = 256
A_WIDTH = A_HEADS * A_VDIM
A_QPAD = 256
ROPE_THETA = 10000.0
ROPE_HALF = A_ROPE // 2
B_Q_HEADS = 16
B_KV_HEADS = 2
B_GROUP = B_Q_HEADS // B_KV_HEADS
B_HEAD_DIM = 64
B_WIDTH = B_Q_HEADS * B_HEAD_DIM
B_KV_WIDTH = B_KV_HEADS * B_HEAD_DIM
WINDOW = 128
WIN_CHUNKS = WINDOW // CHUNK
NUM_BUCKETS = 32
MAX_DISTANCE = 128
DEEPNORM_ALPHA = (2 * DEPTH) ** 0.25
NORM_EPS = 1e-5
MASK_VALUE = -1e30

LANES = 128
VMEM_LIMIT_BYTES = 56 * 1024 * 1024

_NT = (((1,), (1,)), ((), ()))


def _cparams(n_axes):
    return pltpu.CompilerParams(
        dimension_semantics=("arbitrary",) * n_axes, vmem_limit_bytes=VMEM_LIMIT_BYTES)


def _sigmoid(x):
    return 1.0 / (1.0 + jnp.exp(-x))


def _rms_scale(c, g):
    return c * lax.rsqrt(jnp.mean(c * c, axis=-1, keepdims=True) + NORM_EPS) * g


def _deepnorm_ln(x, y, g, b):
    z = DEEPNORM_ALPHA * x + y
    mu = jnp.mean(z, axis=-1, keepdims=True)
    zc = z - mu
    var = jnp.mean(zc * zc, axis=-1, keepdims=True)
    return zc * lax.rsqrt(var + NORM_EPS) * g + b


def _mla_proj_kernel(x_ref, w_in_ref, gq_ref, gkv_ref, w_uqT_ref, w_kn_ref, w_vT_ref,
                     cosT_ref, sinT_ref, cosk_ref, sinka_ref, sinkb_ref,
                     qT_ref, kn_ref, kr_ref, vT_ref, sg_ref):
    xb = x_ref[...].astype(BF16)
    h = jnp.dot(xb, w_in_ref[...], preferred_element_type=F32)
    o_kv = A_Q_LORA
    o_kr = A_Q_LORA + A_KV_LORA
    o_g = o_kr + LANES
    cq = _rms_scale(h[:, :o_kv], gq_ref[...]).astype(BF16)
    ckv = _rms_scale(h[:, o_kv:o_kr], gkv_ref[...]).astype(BF16)

    scale = A_QK ** -0.5 * math.log2(math.e)
    qT = lax.dot_general(w_uqT_ref[...], cq, _NT, preferred_element_type=F32) * scale
    cos = cosT_ref[...]
    sin = sinT_ref[...]
    for hh in range(A_HEADS):
        r0 = hh * A_QPAD
        r1 = r0 + A_NOPE
        r2 = r1 + ROPE_HALF
        r3 = r2 + ROPE_HALF
        t1 = qT[r1:r2]
        t2 = qT[r2:r3]
        qT_ref[r0:r1, :] = qT[r0:r1].astype(BF16)
        qT_ref[r1:r2, :] = (t1 * cos - t2 * sin).astype(BF16)
        qT_ref[r2:r3, :] = (t2 * cos + t1 * sin).astype(BF16)
        qT_ref[r3:r0 + A_QPAD, :] = jnp.zeros((A_QPAD - A_QK, qT.shape[1]), BF16)

    krs = h[:, o_kr:o_g]
    kr = (krs * cosk_ref[...]
          + pltpu.roll(krs, LANES - ROPE_HALF, axis=1) * sinka_ref[...]
          + pltpu.roll(krs, ROPE_HALF, axis=1) * sinkb_ref[...])
    kr_ref[...] = kr.astype(BF16)

    kn_ref[...] = jnp.dot(ckv, w_kn_ref[...], preferred_element_type=F32).astype(BF16)
    vT = lax.dot_general(w_vT_ref[...], ckv, _NT, preferred_element_type=F32).astype(BF16)
    tk = vT_ref.shape[-1]
    for n in range(vT_ref.shape[0]):
        vT_ref[n] = vT[:, n * tk:(n + 1) * tk]

    gate = h[:, o_g:]
    sg_ref[...] = (gate * _sigmoid(gate)).astype(BF16)


def _mla_proj(x, w_in, gq, gkv, w_uqT, w_kn, w_vT, rope, tm, tk):
    B, S, _ = x.shape
    nt = S // tm
    cosT, sinT, cosk, sinka, sinkb = rope
    const = lambda *shape: pl.BlockSpec(shape, lambda b, i: (0,) * len(shape))
    return pl.pallas_call(
        _mla_proj_kernel,
        grid=(B, nt),
        in_specs=[
            pl.BlockSpec((None, tm, D_MODEL), lambda b, i: (b, i, 0)),
            const(*w_in.shape), const(*gq.shape), const(*gkv.shape),
            const(*w_uqT.shape), const(*w_kn.shape), const(*w_vT.shape),
            pl.BlockSpec((ROPE_HALF, tm), lambda b, i: (0, i)),
            pl.BlockSpec((ROPE_HALF, tm), lambda b, i: (0, i)),
            pl.BlockSpec((tm, LANES), lambda b, i: (i, 0)),
            pl.BlockSpec((tm, LANES), lambda b, i: (i, 0)),
            pl.BlockSpec((tm, LANES), lambda b, i: (i, 0)),
        ],
        out_specs=[
            pl.BlockSpec((None, A_HEADS * A_QPAD, tm), lambda b, i: (b, 0, i)),
            pl.BlockSpec((None, tm, A_HEADS * A_NOPE), lambda b, i: (b, i, 0)),
            pl.BlockSpec((None, tm, LANES), lambda b, i: (b, i, 0)),
            pl.BlockSpec((None, tm // tk, A_WIDTH, tk), lambda b, i: (b, i, 0, 0)),
            pl.BlockSpec((None, tm, A_WIDTH), lambda b, i: (b, i, 0)),
        ],
        out_shape=[
            jax.ShapeDtypeStruct((B, A_HEADS * A_QPAD, S), BF16),
            jax.ShapeDtypeStruct((B, S, A_HEADS * A_NOPE), BF16),
            jax.ShapeDtypeStruct((B, S, LANES), BF16),
            jax.ShapeDtypeStruct((B, S // tk, A_WIDTH, tk), BF16),
            jax.ShapeDtypeStruct((B, S, A_WIDTH), BF16),
        ],
        compiler_params=_cparams(2),
        name="mla_proj",
    )(x, w_in, gq, gkv, w_uqT, w_kn, w_vT, cosT, sinT, cosk, sinka, sinkb)


def _mla_attn_kernel(qT_ref, kn_ref, kr_ref, vT_ref, dmask_ref, o_ref,
                     s_a, s_b, p_a, p_b, al_a, al_b, m_ref, l_ref, acc_ref, *, tq, tk):
    qi = pl.program_id(2)
    qT = qT_ref[...]
    full = slice(0, tq)
    late = slice(tk, tq)

    def scores(j, s_ref, cols):
        k0 = pl.multiple_of(j * tk, tk)
        kcat = jnp.concatenate([kn_ref[pl.ds(k0, tk), :], kr_ref[pl.ds(k0, tk), :]], axis=1)
        s_ref[:, cols] = jnp.dot(kcat, qT[:, cols], preferred_element_type=F32)

    def softmax(s_ref, p_ref, al_ref, cols, masked):
        s = s_ref[:, cols]
        if masked:
            diag = s[:, :tk] + dmask_ref[...]
            s = diag if s.shape[1] == tk else jnp.concatenate([diag, s[:, tk:]], axis=1)
        m_old = m_ref[:, cols]
        m_new = jnp.maximum(m_old, jnp.max(s, axis=0, keepdims=True))
        alpha = jnp.exp2(m_old - m_new)
        p = jnp.exp2(s - m_new)
        l_ref[:, cols] = alpha * l_ref[:, cols] + jnp.sum(p, axis=0, keepdims=True)
        m_ref[:, cols] = m_new
        al_ref[:, cols] = alpha
        p_ref[:, cols] = p.astype(BF16)

    def weighted_values(j, p_ref, al_ref, cols):
        acc_ref[:, cols] = al_ref[:, cols] * acc_ref[:, cols] + jnp.dot(
            vT_ref[j], p_ref[:, cols], preferred_element_type=F32)

    m_ref[...] = jnp.full(m_ref.shape, MASK_VALUE, F32)
    l_ref[...] = jnp.zeros(l_ref.shape, F32)
    acc_ref[...] = jnp.zeros(acc_ref.shape, F32)
    p_b[...] = jnp.zeros(p_b.shape, BF16)
    al_b[...] = jnp.ones(al_b.shape, F32)
    scores(0, s_a, full)

    def pair(t, carry):
        j = 2 * t
        weighted_values(jnp.maximum(j - 1, 0), p_b, al_b, full)
        scores(j + 1, s_b, full)
        softmax(s_a, p_a, al_a, full, False)
        weighted_values(j, p_a, al_a, full)
        scores(j + 2, s_a, full)
        softmax(s_b, p_b, al_b, full, False)
        return carry

    lax.fori_loop(0, qi, pair, 0)

    j = 2 * qi
    weighted_values(jnp.maximum(j - 1, 0), p_b, al_b, full)
    scores(j + 1, s_b, late)
    softmax(s_a, p_a, al_a, full, True)
    weighted_values(j, p_a, al_a, full)
    softmax(s_b, p_b, al_b, late, True)
    weighted_values(j + 1, p_b, al_b, late)
    o_ref[...] = (acc_ref[...] / l_ref[...]).T.astype(o_ref.dtype)


def _mla_attn(qT, kn, kr, vT, tq):
    B, _, S = qT.shape
    tk = vT.shape[-1]
    assert tq == 2 * tk and tk % CHUNK == 0
    nq, nk = S // tq, S // tk
    kc = jnp.arange(tk)[:, None] // CHUNK
    qc = jnp.arange(tk)[None, :] // CHUNK
    dmask = jnp.where(kc <= qc, 0.0, MASK_VALUE).astype(F32)
    return pl.pallas_call(
        functools.partial(_mla_attn_kernel, tq=tq, tk=tk),
        grid=(B, A_HEADS, nq),
        in_specs=[
            pl.BlockSpec((None, A_QPAD, tq), lambda b, h, i: (b, h, i)),
            pl.BlockSpec((None, S, A_NOPE), lambda b, h, i: (b, 0, h)),
            pl.BlockSpec((None, S, LANES), lambda b, h, i: (b, 0, 0)),
            pl.BlockSpec((None, nk, A_VDIM, tk), lambda b, h, i: (b, 0, h, 0)),
            pl.BlockSpec((tk, tk), lambda b, h, i: (0, 0)),
        ],
        out_specs=pl.BlockSpec((None, tq, A_VDIM), lambda b, h, i: (b, i, h)),
        out_shape=jax.ShapeDtypeStruct((B, S, A_WIDTH), BF16),
        scratch_shapes=[
            pltpu.VMEM((tk, tq), F32), pltpu.VMEM((tk, tq), F32),
            pltpu.VMEM((tk, tq), BF16), pltpu.VMEM((tk, tq), BF16),
            pltpu.VMEM((1, tq), F32), pltpu.VMEM((1, tq), F32),
            pltpu.VMEM((1, tq), F32), pltpu.VMEM((1, tq), F32),
            pltpu.VMEM((A_VDIM, tq), F32),
        ],
        compiler_params=_cparams(3),
        name="mla_attn",
    )(qT, kn, kr, vT, dmask)


def _out_ln_kernel(o_ref, sg_ref, x_ref, wo_ref, g_ref, b_ref, xo_ref):
    og = o_ref[...] * sg_ref[...]
    y = jnp.dot(og, wo_ref[...], preferred_element_type=F32)
    xo_ref[...] = _deepnorm_ln(x_ref[...], y, g_ref[...], b_ref[...])


def _out_ln_kv_kernel(o_ref, sg_ref, x_ref, wo_ref, g_ref, b_ref, wk_ref, wvT_ref,
                      xo_ref, ks_ref, vsT_ref):
    og = o_ref[...] * sg_ref[...]
    y = jnp.dot(og, wo_ref[...], preferred_element_type=F32)
    xn = _deepnorm_ln(x_ref[...], y, g_ref[...], b_ref[...])
    xo_ref[...] = xn
    xb = xn.astype(BF16)
    ks_ref[...] = jnp.dot(xb, wk_ref[...], preferred_element_type=F32).astype(BF16)
    vT = lax.dot_general(wvT_ref[...], xb, _NT, preferred_element_type=F32).astype(BF16)
    for n in range(vsT_ref.shape[0]):
        vsT_ref[n] = vT[:, n * Q_BLOCK:(n + 1) * Q_BLOCK]


def _out_ln(o, sg, x, wo, g, b, tm, kv_weights=None):
    B, S, _ = x.shape
    row = pl.BlockSpec((None, tm, D_MODEL), lambda b_, i: (b_, i, 0))
    const = lambda *shape: pl.BlockSpec(shape, lambda b_, i: (0,) * len(shape))
    in_specs = [row, row, row, const(*wo.shape), const(*g.shape), const(*b.shape)]
    if kv_weights is None:
        return pl.pallas_call(
            _out_ln_kernel, grid=(B, S // tm), in_specs=in_specs, out_specs=row,
            out_shape=jax.ShapeDtypeStruct(x.shape, F32),
            compiler_params=_cparams(2), name="out_ln",
        )(o, sg, x, wo, g, b)
    wk, wvT = kv_weights
    nb = tm // Q_BLOCK
    return pl.pallas_call(
        _out_ln_kv_kernel, grid=(B, S // tm),
        in_specs=in_specs + [const(*wk.shape), const(*wvT.shape)],
        out_specs=[
            row,
            pl.BlockSpec((None, tm, B_KV_WIDTH), lambda b_, i: (b_, i, 0)),
            pl.BlockSpec((None, nb, B_KV_WIDTH, Q_BLOCK), lambda b_, i: (b_, i, 0, 0)),
        ],
        out_shape=[
            jax.ShapeDtypeStruct(x.shape, F32),
            jax.ShapeDtypeStruct((B, S, B_KV_WIDTH), BF16),
            jax.ShapeDtypeStruct((B, S // Q_BLOCK, B_KV_WIDTH, Q_BLOCK), BF16),
        ],
        compiler_params=_cparams(2), name="out_ln_kv",
    )(o, sg, x, wo, g, b, wk, wvT)


def _band_bias_kernel(tab_ref, bkt_ref, ok_ref, out_ref):
    bkt = bkt_ref[...]
    ok = ok_ref[...] != 0
    has_prev = lax.broadcasted_iota(jnp.int32, bkt.shape, 0) >= Q_BLOCK
    for h in range(B_Q_HEADS):
        acc = jnp.zeros(bkt.shape, F32)
        for bucket in range(NUM_BUCKETS):
            acc = jnp.where(bkt == bucket, tab_ref[bucket, h], acc)
        generic = jnp.where(ok, acc, MASK_VALUE)
        g, hi = divmod(h, B_GROUP)
        out_ref[0, g, :, hi * Q_BLOCK:(hi + 1) * Q_BLOCK] = jnp.where(has_prev, generic, MASK_VALUE)
        out_ref[1, g, :, hi * Q_BLOCK:(hi + 1) * Q_BLOCK] = generic


def _t5_bucket_map(rel):
    half = NUM_BUCKETS // 2
    ret = jnp.where(rel > 0, half, 0)
    n = jnp.abs(rel)
    max_exact = half // 2
    large = max_exact + (jnp.log(jnp.maximum(n, 1).astype(F32) / max_exact)
                         / math.log(MAX_DISTANCE / max_exact) * (half - max_exact)).astype(jnp.int32)
    large = jnp.minimum(large, half - 1)
    return ret + jnp.where(n < max_exact, n, large)


def _band_bias(rel_bias_table):
    j = jnp.arange(2 * Q_BLOCK)[:, None]
    r = jnp.arange(Q_BLOCK)[None, :]
    rel = j - Q_BLOCK - r
    d = (r // CHUNK) - ((j - Q_BLOCK) // CHUNK)
    ok = ((d >= 0) & (d <= WIN_CHUNKS)).astype(jnp.int32)
    bkt = _t5_bucket_map(rel).astype(jnp.int32)
    return pl.pallas_call(
        _band_bias_kernel,
        in_specs=[
            pl.BlockSpec(memory_space=pltpu.SMEM),
            pl.BlockSpec(memory_space=pltpu.VMEM),
            pl.BlockSpec(memory_space=pltpu.VMEM),
        ],
        out_specs=pl.BlockSpec(memory_space=pltpu.VMEM),
        out_shape=jax.ShapeDtypeStruct((2, B_KV_HEADS, 2 * Q_BLOCK, B_GROUP * Q_BLOCK), F32),
        name="band_bias",
    )(rel_bias_table, bkt, ok)


def _swa_layer_kernel(x_ref, wqT_ref, wg_ref, wo_ref, ks_ref, vsT_ref, bias_ref, sink_ref,
                      g_ref, b_ref, xo_ref, og_ref, *, nblk):
    i = pl.program_id(1)
    x = x_ref[...]
    xb = x.astype(BF16)
    qT = (lax.dot_general(wqT_ref[...], xb, _NT, preferred_element_type=F32)
          * (B_HEAD_DIM ** -0.5)).astype(BF16)
    gate = jnp.dot(xb, wg_ref[...], preferred_element_type=F32)
    sg = gate * _sigmoid(gate)
    zeros = jnp.zeros((B_HEAD_DIM, B_GROUP * Q_BLOCK), BF16)

    for n in range(nblk):
        blk = i * nblk + n
        cur = pl.multiple_of(blk * Q_BLOCK, Q_BLOCK)
        prev_blk = jnp.maximum(blk - 1, 0)
        prev = pl.multiple_of(prev_blk * Q_BLOCK, Q_BLOCK)
        kband = jnp.concatenate([ks_ref[pl.ds(prev, Q_BLOCK), :], ks_ref[pl.ds(cur, Q_BLOCK), :]], axis=0)
        vband = jnp.concatenate([vsT_ref[prev_blk], vsT_ref[blk]], axis=1)
        table = jnp.minimum(blk, 1)
        lanes = slice(n * Q_BLOCK, (n + 1) * Q_BLOCK)
        o_parts = []
        for g in range(B_KV_HEADS):
            qg = jnp.concatenate(
                [qT[(g * B_GROUP + hi) * B_HEAD_DIM:(g * B_GROUP + hi + 1) * B_HEAD_DIM, lanes]
                 for hi in range(B_GROUP)], axis=1)
            rhs = jnp.concatenate([qg, zeros] if g == 0 else [zeros, qg], axis=0)
            sT = jnp.dot(kband, rhs, preferred_element_type=F32) + bias_ref[table, g]
            sink = sink_ref[g]
            m = jnp.maximum(jnp.max(sT, axis=0, keepdims=True), sink)
            e = jnp.exp(sT - m)
            den = jnp.sum(e, axis=0, keepdims=True) + jnp.exp(sink - m)
            v_rows = slice(g * B_HEAD_DIM, (g + 1) * B_HEAD_DIM)
            oT = jnp.dot(vband[v_rows, :], e.astype(BF16), preferred_element_type=F32)
            o_parts.append(oT / den)
        for hi in range(B_GROUP):
            cols = slice(hi * Q_BLOCK, (hi + 1) * Q_BLOCK)
            o_hi = jnp.concatenate([o_parts[0][:, cols], o_parts[1][:, cols]], axis=0).T
            og_ref[lanes, cols] = (o_hi * sg[lanes, cols]).astype(BF16)

    y = jnp.dot(og_ref[...], wo_ref[...], preferred_element_type=F32)
    xo_ref[...] = _deepnorm_ln(x, y, g_ref[...], b_ref[...])


def _swa_layer(x, wqT, wg, wo, ks, vsT, bias, sink, g, b, tm):
    B, S, _ = x.shape
    nblk = tm // Q_BLOCK
    row = pl.BlockSpec((None, tm, D_MODEL), lambda b_, i: (b_, i, 0))
    const = lambda *shape: pl.BlockSpec(shape, lambda b_, i: (0,) * len(shape))
    return pl.pallas_call(
        functools.partial(_swa_layer_kernel, nblk=nblk),
        grid=(B, S // tm),
        in_specs=[
            row, const(*wqT.shape), const(*wg.shape), const(*wo.shape),
            pl.BlockSpec((None, S, B_KV_WIDTH), lambda b_, i: (b_, 0, 0)),
            pl.BlockSpec((None, S // Q_BLOCK, B_KV_WIDTH, Q_BLOCK), lambda b_, i: (b_, 0, 0, 0)),
            const(*bias.shape), const(*sink.shape), const(*g.shape), const(*b.shape),
        ],
        out_specs=row,
        out_shape=jax.ShapeDtypeStruct(x.shape, F32),
        scratch_shapes=[pltpu.VMEM((tm, B_WIDTH), BF16)],
        compiler_params=_cparams(2),
        name="swa_layer",
    )(x, wqT, wg, wo, ks, vsT, bias, sink, g, b)


def _rope_tables(S):
    inv = ROPE_THETA ** (-jnp.arange(ROPE_HALF, dtype=F32) / ROPE_HALF)
    ang = jnp.arange(S, dtype=F32)[:, None] * inv[None, :]
    cos, sin = jnp.cos(ang), jnp.sin(ang)
    zeros = jnp.zeros((S, LANES - A_ROPE), F32)
    z32 = jnp.zeros((S, ROPE_HALF), F32)
    cosk = jnp.concatenate([cos, cos, zeros], axis=1)
    sinka = jnp.concatenate([-sin, z32, zeros], axis=1)
    sinkb = jnp.concatenate([z32, sin, zeros], axis=1)
    return cos.T, sin.T, cosk, sinka, sinkb


def _mla_weights(w_in, w_uq, w_ukv):
    o_kr = A_Q_LORA + A_KV_LORA
    w_in_pad = jnp.concatenate(
        [w_in[:, :o_kr + A_ROPE], jnp.zeros((D_MODEL, LANES - A_ROPE), w_in.dtype),
         w_in[:, o_kr + A_ROPE:]], axis=1).astype(BF16)
    w_uq_h = w_uq.reshape(A_Q_LORA, A_HEADS, A_QK)
    w_uq_pad = jnp.pad(w_uq_h, ((0, 0), (0, 0), (0, A_QPAD - A_QK))).reshape(A_Q_LORA, A_HEADS * A_QPAD)
    w_ukv_h = w_ukv.reshape(A_KV_LORA, A_HEADS, A_NOPE + A_VDIM)
    w_kn = w_ukv_h[:, :, :A_NOPE].reshape(A_KV_LORA, A_HEADS * A_NOPE)
    w_v = w_ukv_h[:, :, A_NOPE:].reshape(A_KV_LORA, A_WIDTH)
    return w_in_pad, w_uq_pad.T.astype(BF16), w_kn.astype(BF16), w_v.T.astype(BF16)


def _swa_feature_perm():
    hi, g, d = jnp.meshgrid(jnp.arange(B_GROUP), jnp.arange(B_KV_HEADS), jnp.arange(B_HEAD_DIM),
                            indexing="ij")
    return ((g * B_GROUP + hi) * B_HEAD_DIM + d).reshape(-1)


def kernel(x, w_in_a, q_norm_a, kv_norm_a, w_uq_a, w_ukv_a, w_o_a, w_kv_shared, w_in_b, sinks_b,
           w_o_b, rel_bias_table, ln_gain, ln_bias):
    B, S, _ = x.shape
    tm = min(512, S)
    tq = min(512, S)
    rope = _rope_tables(S)

    ks = vsT = None
    for i in range(N_A_LAYERS):
        w_in, w_uqT, w_kn, w_vT = _mla_weights(w_in_a[i], w_uq_a[i], w_ukv_a[i])
        qT, kn, kr, vT, sg = _mla_proj(
            x, w_in, q_norm_a[i][None, :], kv_norm_a[i][None, :], w_uqT, w_kn, w_vT, rope, tm, tq // 2)
        o = _mla_attn(qT, kn, kr, vT, tq)
        g, b = ln_gain[i][None, :], ln_bias[i][None, :]
        wo = w_o_a[i].astype(BF16)
        if i < N_A_LAYERS - 1:
            x = _out_ln(o, sg, x, wo, g, b, tm)
        else:
            x, ks, vsT = _out_ln(o, sg, x, wo, g, b, tm,
                                 kv_weights=(w_kv_shared[:, :B_KV_WIDTH].astype(BF16),
                                             w_kv_shared[:, B_KV_WIDTH:].T.astype(BF16)))

    bias = _band_bias(rel_bias_table)
    perm = _swa_feature_perm()
    for j in range(DEPTH - N_A_LAYERS):
        layer = N_A_LAYERS + j
        wqT = w_in_b[j][:, :B_WIDTH].T.astype(BF16)
        wg = w_in_b[j][:, B_WIDTH:][:, perm].astype(BF16)
        wo = w_o_b[j][perm, :].astype(BF16)
        sink = jnp.repeat(sinks_b[j].astype(F32).reshape(B_KV_HEADS, 1, B_GROUP), Q_BLOCK, axis=2)
        x = _swa_layer(x, wqT, wg, wo, ks, vsT, bias, sink,
                       ln_gain[layer][None, :], ln_bias[layer][None, :], min(256, S))
    return x
```

```python
import functools
import math

import jax
import jax.numpy as jnp
from jax import lax
from jax.experimental import pallas as pl
from jax.experimental.pallas import tpu as pltpu

F32 = jnp.float32
BF16 = jnp.bfloat16

D_MODEL = 1024
DEPTH = 4
CHUNK = 64
Q_BLOCK = 128
N_A_LAYERS = DEPTH // 2
A_HEADS = 8
A_NOPE = 128
A_ROPE = 64
A_VDIM = 128
A_QK = A_NOPE + A_ROPE
A_Q_LORA = 384
A_KV_LORA = 256
A_WIDTH = A_HEADS * A_VDIM
A_QPAD = 256
A_VROWS = A_VDIM + 16
ROPE_THETA = 10000.0
ROPE_HALF = A_ROPE // 2
B_Q_HEADS = 16
B_KV_HEADS = 2
B_GROUP = B_Q_HEADS // B_KV_HEADS
B_HEAD_DIM = 64
B_WIDTH = B_Q_HEADS * B_HEAD_DIM
B_KV_WIDTH = B_KV_HEADS * B_HEAD_DIM
WINDOW = 128
WIN_CHUNKS = WINDOW // CHUNK
NUM_BUCKETS = 32
MAX_DISTANCE = 128
DEEPNORM_ALPHA = (2 * DEPTH) ** 0.25
NORM_EPS = 1e-5
MASK_VALUE = -1e30

LANES = 128
VMEM_LIMIT_BYTES = 56 * 1024 * 1024

_NT = (((1,), (1,)), ((), ()))
UNROLL = 4


def _cparams(n_axes, flags=None):
    return pltpu.CompilerParams(
        dimension_semantics=("arbitrary",) * n_axes, vmem_limit_bytes=VMEM_LIMIT_BYTES, flags=flags)


def _sigmoid(x):
    return 1.0 / (1.0 + jnp.exp(-x))


def _rms_scale(c, g):
    return c * lax.rsqrt(jnp.mean(c * c, axis=-1, keepdims=True) + NORM_EPS) * g


def _deepnorm_ln(x, y, g, b):
    z = DEEPNORM_ALPHA * x + y
    mu = jnp.mean(z, axis=-1, keepdims=True)
    zc = z - mu
    var = jnp.mean(zc * zc, axis=-1, keepdims=True)
    return zc * lax.rsqrt(var + NORM_EPS) * g + b


def _mla_proj_kernel(x_ref, w_in_ref, gq_ref, gkv_ref, w_uqT_ref, w_kn_ref, w_vT_ref,
                     cosT_ref, sinT_ref, cosk_ref, sinka_ref, sinkb_ref, chunk1h_ref,
                     qT_ref, kn_ref, kr_ref, vT_ref, sg_ref):
    xb = x_ref[...].astype(BF16)
    h = jnp.dot(xb, w_in_ref[...], preferred_element_type=F32)
    o_kv = A_Q_LORA
    o_kr = A_Q_LORA + A_KV_LORA
    o_g = o_kr + LANES
    cq = _rms_scale(h[:, :o_kv], gq_ref[...]).astype(BF16)
    ckv = _rms_scale(h[:, o_kv:o_kr], gkv_ref[...]).astype(BF16)

    scale = A_QK ** -0.5 * math.log2(math.e)
    qT = lax.dot_general(w_uqT_ref[...], cq, _NT, preferred_element_type=F32) * scale
    cos = cosT_ref[...]
    sin = sinT_ref[...]
    n_r1 = A_NOPE
    n_r2 = n_r1 + ROPE_HALF
    n_r3 = n_r2 + ROPE_HALF
    for hh in range(A_HEADS):
        r0 = hh * A_QPAD
        t1 = qT[r0 + n_r1:r0 + n_r2]
        t2 = qT[r0 + n_r2:r0 + n_r3]
        qT_ref[hh, :n_r1, :] = qT[r0:r0 + n_r1].astype(BF16)
        qT_ref[hh, n_r1:n_r2, :] = (t1 * cos - t2 * sin).astype(BF16)
        qT_ref[hh, n_r2:n_r3, :] = (t2 * cos + t1 * sin).astype(BF16)
        qT_ref[hh, n_r3:, :] = jnp.zeros((A_QPAD - A_QK, qT.shape[1]), BF16)

    krs = h[:, o_kr:o_g]
    kr = (krs * cosk_ref[...]
          + pltpu.roll(krs, LANES - ROPE_HALF, axis=1) * sinka_ref[...]
          + pltpu.roll(krs, ROPE_HALF, axis=1) * sinkb_ref[...]
          + chunk1h_ref[...])
    kr_ref[...] = kr.astype(BF16)

    kn = jnp.dot(ckv, w_kn_ref[...], preferred_element_type=F32).astype(BF16)
    vT = lax.dot_general(w_vT_ref[...], ckv, _NT, preferred_element_type=F32).astype(BF16)
    tk = vT_ref.shape[-1]
    extra = (lax.broadcasted_iota(jnp.int32, (A_VROWS - A_VDIM, tk), 0) == 0).astype(BF16)
    for hh in range(A_HEADS):
        kn_ref[hh] = kn[:, hh * A_NOPE:(hh + 1) * A_NOPE]
        for n in range(vT_ref.shape[1]):
            vT_ref[hh, n, :A_VDIM, :] = vT[hh * A_VDIM:(hh + 1) * A_VDIM, n * tk:(n + 1) * tk]
            vT_ref[hh, n, A_VDIM:, :] = extra

    gate = h[:, o_g:]
    sg_ref[...] = (gate * _sigmoid(gate)).astype(BF16)


def _mla_proj(x, w_in, gq, gkv, w_uqT, w_kn, w_vT, rope, tm, tq, tk):
    B, S, _ = x.shape
    nt = S // tm
    cosT, sinT, cosk, sinka, sinkb = rope
    key = jnp.arange(S)[:, None]
    lane = jnp.arange(LANES)[None, :]
    chunk1h = (lane - A_ROPE == (key % tk) // CHUNK).astype(F32)
    const = lambda *shape: pl.BlockSpec(shape, lambda b, i: (0,) * len(shape))
    tok_lanes = pl.BlockSpec((tm, LANES), lambda b, i: (i, 0))
    return pl.pallas_call(
        _mla_proj_kernel,
        grid=(B, nt),
        in_specs=[
            pl.BlockSpec((None, tm, D_MODEL), lambda b, i: (b, i, 0)),
            const(*w_in.shape), const(*gq.shape), const(*gkv.shape),
            const(*w_uqT.shape), const(*w_kn.shape), const(*w_vT.shape),
            pl.BlockSpec((ROPE_HALF, tm), lambda b, i: (0, i)),
            pl.BlockSpec((ROPE_HALF, tm), lambda b, i: (0, i)),
            tok_lanes, tok_lanes, tok_lanes, tok_lanes,
        ],
        out_specs=[
            pl.BlockSpec((None, A_HEADS, None, A_QPAD, tm),
                         lambda b, i: (b, 0, i // (tq // tm), 0, i % (tq // tm))),
            pl.BlockSpec((None, A_HEADS, tm, A_NOPE), lambda b, i: (b, 0, i, 0)),
            pl.BlockSpec((None, tm, LANES), lambda b, i: (b, i, 0)),
            pl.BlockSpec((None, A_HEADS, tm // tk, A_VROWS, tk), lambda b, i: (b, 0, i, 0, 0)),
            pl.BlockSpec((None, tm, A_WIDTH), lambda b, i: (b, i, 0)),
        ],
        out_shape=[
            jax.ShapeDtypeStruct((B, A_HEADS, S // tq, A_QPAD, tq), BF16),
            jax.ShapeDtypeStruct((B, A_HEADS, S, A_NOPE), BF16),
            jax.ShapeDtypeStruct((B, S, LANES), BF16),
            jax.ShapeDtypeStruct((B, A_HEADS, S // tk, A_VROWS, tk), BF16),
            jax.ShapeDtypeStruct((B, S, A_WIDTH), BF16),
        ],
        compiler_params=_cparams(2),
        name="mla_proj",
    )(x, w_in, gq, gkv, w_uqT, w_kn, w_vT, cosT, sinT, cosk, sinka, sinkb, chunk1h)


def _mla_attn_kernel(qT_ref, kn_ref, kr_ref, vT_ref, mrow_ref, o_ref,
                     s_ref, smax_ref, p_ref, al_ref, m_ref, acc_ref, *, nq, tq):
    def scores(qi, t, variant, buf):
        k0 = pl.multiple_of(t * tq, tq)
        kcat = jnp.concatenate([kn_ref[pl.ds(k0, tq), :], kr_ref[pl.ds(k0, tq), :]], axis=1)
        q_full = jnp.concatenate([qT_ref[qi, :A_QK, :], mrow_ref[variant]], axis=0)
        s = jnp.dot(kcat, q_full, preferred_element_type=F32)
        s_ref[...] = s
        smax_ref[buf] = jnp.max(s, axis=0, keepdims=True)

    def softmax(qi, buf):
        m = m_ref[qi]
        m_new = jnp.maximum(m, smax_ref[buf])
        al_ref[buf] = jnp.exp2(m - m_new)
        p_ref[...] = jnp.exp2(s_ref[...] - m_new).astype(BF16)
        m_ref[qi] = m_new

    def weighted_values(qi, t, buf):
        acc_ref[qi] = acc_ref[qi] * al_ref[buf] + jnp.dot(
            vT_ref[t], p_ref[...], preferred_element_type=F32)

    m_ref[...] = jnp.full(m_ref.shape, MASK_VALUE, F32)
    acc_ref[...] = jnp.zeros(acc_ref.shape, F32)
    p_ref[...] = jnp.zeros(p_ref.shape, BF16)
    al_ref[1] = jnp.ones(al_ref.shape[1:], F32)
    scores(0, 0, 1, 0)

    def step(carry, buf):
        qi, t, qi_prev, t_prev = carry
        last = t == qi
        qi_next = jnp.where(last, qi + 1, qi)
        t_next = jnp.where(last, 0, t + 1)
        weighted_values(qi_prev, t_prev, 1 - buf)
        softmax(qi, buf)
        scores(jnp.minimum(qi_next, nq - 1), t_next, (t_next == qi_next).astype(jnp.int32), 1 - buf)
        return qi_next, t_next, qi, t

    zero = jnp.int32(0)
    carry = (zero, zero, zero, zero)
    n_steps = nq * (nq + 1) // 2
    n_lead = n_steps % UNROLL
    for u in range(n_lead):
        carry = step(carry, u % 2)

    def steps(it, carry):
        for u in range(UNROLL):
            carry = step(carry, (n_lead + u) % 2)
        return carry

    _, _, qi_last, t_last = lax.fori_loop(0, n_steps // UNROLL, steps, carry)
    weighted_values(qi_last, t_last, (n_steps - 1) % 2)

    def finish(qi, carry):
        r0 = pl.multiple_of(qi * tq, tq)
        acc = acc_ref[qi]
        o_ref[pl.ds(r0, tq), :] = (acc[:A_VDIM] / acc[A_VDIM:A_VDIM + 1]).T.astype(o_ref.dtype)
        return carry

    lax.fori_loop(0, nq, finish, 0)


def _mla_attn(qT, kn, kr, vT):
    B, H, nq, _, tq = qT.shape
    S = kn.shape[2]
    assert vT.shape[2:] == (nq, A_VROWS, tq) and tq // CHUNK <= A_QPAD - A_QK
    a = jnp.arange(A_QPAD - A_QK)[:, None]
    qchunk = (jnp.arange(tq) // CHUNK)[None, :]
    diagonal = jnp.where((qchunk < a) & (a < tq // CHUNK), MASK_VALUE, 0.0)
    mrow = jnp.stack([jnp.zeros_like(diagonal), diagonal]).astype(BF16)
    return pl.pallas_call(
        functools.partial(_mla_attn_kernel, nq=nq, tq=tq),
        grid=(B, H),
        in_specs=[
            pl.BlockSpec((None, None, nq, A_QPAD, tq), lambda b, h: (b, h, 0, 0, 0)),
            pl.BlockSpec((None, None, S, A_NOPE), lambda b, h: (b, h, 0, 0)),
            pl.BlockSpec((None, S, LANES), lambda b, h: (b, 0, 0)),
            pl.BlockSpec((None, None, nq, A_VROWS, tq), lambda b, h: (b, h, 0, 0, 0)),
            pl.BlockSpec(mrow.shape, lambda b, h: (0, 0, 0)),
        ],
        out_specs=pl.BlockSpec((None, None, S, A_VDIM), lambda b, h: (b, h, 0, 0)),
        out_shape=jax.ShapeDtypeStruct((B, H, S, A_VDIM), BF16),
        scratch_shapes=[
            pltpu.VMEM((tq, tq), F32),
            pltpu.VMEM((2, 1, tq), F32),
            pltpu.VMEM((tq, tq), BF16),
            pltpu.VMEM((2, 1, tq), F32),
            pltpu.VMEM((nq, 1, tq), F32),
            pltpu.VMEM((nq, A_VROWS, tq), F32),
        ],
        compiler_params=_cparams(2),
        name="mla_attn",
    )(qT, kn, kr, vT, mrow)


def _gated(o_ref, sg_ref):
    o = jnp.concatenate([o_ref[hh] for hh in range(o_ref.shape[0])], axis=1)
    return o * sg_ref[...]


def _out_ln_kernel(o_ref, sg_ref, x_ref, wo_ref, g_ref, b_ref, xo_ref):
    y = jnp.dot(_gated(o_ref, sg_ref), wo_ref[...], preferred_element_type=F32)
    xo_ref[...] = _deepnorm_ln(x_ref[...], y, g_ref[...], b_ref[...])


def _out_ln_kv_kernel(o_ref, sg_ref, x_ref, wo_ref, g_ref, b_ref, wk_ref, wvT_ref,
                      xo_ref, ks_ref, vsT_ref):
    y = jnp.dot(_gated(o_ref, sg_ref), wo_ref[...], preferred_element_type=F32)
    xn = _deepnorm_ln(x_ref[...], y, g_ref[...], b_ref[...])
    xo_ref[...] = xn
    xb = xn.astype(BF16)
    ks_ref[...] = jnp.dot(xb, wk_ref[...], preferred_element_type=F32).astype(BF16)
    vT = lax.dot_general(wvT_ref[...], xb, _NT, preferred_element_type=F32).astype(BF16)
    for n in range(vsT_ref.shape[0]):
        vsT_ref[n] = vT[:, n * Q_BLOCK:(n + 1) * Q_BLOCK]


def _out_ln(o, sg, x, wo, g, b, tm, kv_weights=None):
    B, S, _ = x.shape
    row = pl.BlockSpec((None, tm, D_MODEL), lambda b_, i: (b_, i, 0))
    const = lambda *shape: pl.BlockSpec(shape, lambda b_, i: (0,) * len(shape))
    heads = pl.BlockSpec((None, A_HEADS, tm, A_VDIM), lambda b_, i: (b_, 0, i, 0))
    in_specs = [heads, row, row, const(*wo.shape), const(*g.shape), const(*b.shape)]
    if kv_weights is None:
        return pl.pallas_call(
            _out_ln_kernel, grid=(B, S // tm), in_specs=in_specs, out_specs=row,
            out_shape=jax.ShapeDtypeStruct(x.shape, F32),
            compiler_params=_cparams(2), name="out_ln",
        )(o, sg, x, wo, g, b)
    wk, wvT = kv_weights
    nb = tm // Q_BLOCK
    return pl.pallas_call(
        _out_ln_kv_kernel, grid=(B, S // tm),
        in_specs=in_specs + [const(*wk.shape), const(*wvT.shape)],
        out_specs=[
            row,
            pl.BlockSpec((None, tm, B_KV_WIDTH), lambda b_, i: (b_, i, 0)),
            pl.BlockSpec((None, nb, B_KV_WIDTH, Q_BLOCK), lambda b_, i: (b_, i, 0, 0)),
        ],
        out_shape=[
            jax.ShapeDtypeStruct(x.shape, F32),
            jax.ShapeDtypeStruct((B, S, B_KV_WIDTH), BF16),
            jax.ShapeDtypeStruct((B, S // Q_BLOCK, B_KV_WIDTH, Q_BLOCK), BF16),
        ],
        compiler_params=_cparams(2), name="out_ln_kv",
    )(o, sg, x, wo, g, b, wk, wvT)


def _band_bias_kernel(tab_ref, bkt_ref, ok_ref, out_ref):
    bkt = bkt_ref[...]
    ok = ok_ref[...] != 0
    has_prev = lax.broadcasted_iota(jnp.int32, bkt.shape, 0) >= Q_BLOCK
    for h in range(B_Q_HEADS):
        acc = jnp.zeros(bkt.shape, F32)
        for bucket in range(NUM_BUCKETS):
            acc = jnp.where(bkt == bucket, tab_ref[bucket, h], acc)
        generic = jnp.where(ok, acc, MASK_VALUE)
        g, hi = divmod(h, B_GROUP)
        out_ref[0, g, :, hi * Q_BLOCK:(hi + 1) * Q_BLOCK] = jnp.where(has_prev, generic, MASK_VALUE)
        out_ref[1, g, :, hi * Q_BLOCK:(hi + 1) * Q_BLOCK] = generic


def _t5_bucket_map(rel):
    half = NUM_BUCKETS // 2
    ret = jnp.where(rel > 0, half, 0)
    n = jnp.abs(rel)
    max_exact = half // 2
    large = max_exact + (jnp.log(jnp.maximum(n, 1).astype(F32) / max_exact)
                         / math.log(MAX_DISTANCE / max_exact) * (half - max_exact)).astype(jnp.int32)
    large = jnp.minimum(large, half - 1)
    return ret + jnp.where(n < max_exact, n, large)


def _band_bias(rel_bias_table):
    j = jnp.arange(2 * Q_BLOCK)[:, None]
    r = jnp.arange(Q_BLOCK)[None, :]
    rel = j - Q_BLOCK - r
    d = (r // CHUNK) - ((j - Q_BLOCK) // CHUNK)
    ok = ((d >= 0) & (d <= WIN_CHUNKS)).astype(jnp.int32)
    bkt = _t5_bucket_map(rel).astype(jnp.int32)
    return pl.pallas_call(
        _band_bias_kernel,
        in_specs=[
            pl.BlockSpec(memory_space=pltpu.SMEM),
            pl.BlockSpec(memory_space=pltpu.VMEM),
            pl.BlockSpec(memory_space=pltpu.VMEM),
        ],
        out_specs=pl.BlockSpec(memory_space=pltpu.VMEM),
        out_shape=jax.ShapeDtypeStruct((2, B_KV_HEADS, 2 * Q_BLOCK, B_GROUP * Q_BLOCK), F32),
        name="band_bias",
    )(rel_bias_table, bkt, ok)


def _swa_layer_kernel(x_ref, wqT_ref, wg_ref, wo_ref, ks_ref, vsT_ref, bias_ref, sink_ref,
                      g_ref, b_ref, xo_ref, og_ref, *, nblk):
    i = pl.program_id(1)
    x = x_ref[...]
    xb = x.astype(BF16)
    qT = (lax.dot_general(wqT_ref[...], xb, _NT, preferred_element_type=F32)
          * (B_HEAD_DIM ** -0.5)).astype(BF16)
    gate = jnp.dot(xb, wg_ref[...], preferred_element_type=F32)
    sg = gate * _sigmoid(gate)
    zeros = jnp.zeros((B_HEAD_DIM, B_GROUP * Q_BLOCK), BF16)

    for n in range(nblk):
        blk = i * nblk + n
        cur = pl.multiple_of(blk * Q_BLOCK, Q_BLOCK)
        prev_blk = jnp.maximum(blk - 1, 0)
        prev = pl.multiple_of(prev_blk * Q_BLOCK, Q_BLOCK)
        kband = jnp.concatenate([ks_ref[pl.ds(prev, Q_BLOCK), :], ks_ref[pl.ds(cur, Q_BLOCK), :]], axis=0)
        vband = jnp.concatenate([vsT_ref[prev_blk], vsT_ref[blk]], axis=1)
        table = jnp.minimum(blk, 1)
        lanes = slice(n * Q_BLOCK, (n + 1) * Q_BLOCK)
        o_parts = []
        for g in range(B_KV_HEADS):
            qg = jnp.concatenate(
                [qT[(g * B_GROUP + hi) * B_HEAD_DIM:(g * B_GROUP + hi + 1) * B_HEAD_DIM, lanes]
                 for hi in range(B_GROUP)], axis=1)
            rhs = jnp.concatenate([qg, zeros] if g == 0 else [zeros, qg], axis=0)
            sT = jnp.dot(kband, rhs, preferred_element_type=F32) + bias_ref[table, g]
            sink = sink_ref[g]
            m = jnp.maximum(jnp.max(sT, axis=0, keepdims=True), sink)
            e = jnp.exp(sT - m)
            den = jnp.sum(e, axis=0, keepdims=True) + jnp.exp(sink - m)
            v_rows = slice(g * B_HEAD_DIM, (g + 1) * B_HEAD_DIM)
            oT = jnp.dot(vband[v_rows, :], e.astype(BF16), preferred_element_type=F32)
            o_parts.append(oT / den)
        for hi in range(B_GROUP):
            cols = slice(hi * Q_BLOCK, (hi + 1) * Q_BLOCK)
            o_hi = jnp.concatenate([o_parts[0][:, cols], o_parts[1][:, cols]], axis=0).T
            og_ref[lanes, cols] = (o_hi * sg[lanes, cols]).astype(BF16)

    y = jnp.dot(og_ref[...], wo_ref[...], preferred_element_type=F32)
    xo_ref[...] = _deepnorm_ln(x, y, g_ref[...], b_ref[...])


def _swa_layer(x, wqT, wg, wo, ks, vsT, bias, sink, g, b, tm):
    B, S, _ = x.shape
    nblk = tm // Q_BLOCK
    row = pl.BlockSpec((None, tm, D_MODEL), lambda b_, i: (b_, i, 0))
    const = lambda *shape: pl.BlockSpec(shape, lambda b_, i: (0,) * len(shape))
    return pl.pallas_call(
        functools.partial(_swa_layer_kernel, nblk=nblk),
        grid=(B, S // tm),
        in_specs=[
            row, const(*wqT.shape), const(*wg.shape), const(*wo.shape),
            pl.BlockSpec((None, S, B_KV_WIDTH), lambda b_, i: (b_, 0, 0)),
            pl.BlockSpec((None, S // Q_BLOCK, B_KV_WIDTH, Q_BLOCK), lambda b_, i: (b_, 0, 0, 0)),
            const(*bias.shape), const(*sink.shape), const(*g.shape), const(*b.shape),
        ],
        out_specs=row,
        out_shape=jax.ShapeDtypeStruct(x.shape, F32),
        scratch_shapes=[pltpu.VMEM((tm, B_WIDTH), BF16)],
        compiler_params=_cparams(2),
        name="swa_layer",
    )(x, wqT, wg, wo, ks, vsT, bias, sink, g, b)


def _rope_tables(S):
    inv = ROPE_THETA ** (-jnp.arange(ROPE_HALF, dtype=F32) / ROPE_HALF)
    ang = jnp.arange(S, dtype=F32)[:, None] * inv[None, :]
    cos, sin = jnp.cos(ang), jnp.sin(ang)
    zeros = jnp.zeros((S, LANES - A_ROPE), F32)
    z32 = jnp.zeros((S, ROPE_HALF), F32)
    cosk = jnp.concatenate([cos, cos, zeros], axis=1)
    sinka = jnp.concatenate([-sin, z32, zeros], axis=1)
    sinkb = jnp.concatenate([z32, sin, zeros], axis=1)
    return cos.T, sin.T, cosk, sinka, sinkb


def _mla_weights(w_in, w_uq, w_ukv):
    o_kr = A_Q_LORA + A_KV_LORA
    w_in_pad = jnp.concatenate(
        [w_in[:, :o_kr + A_ROPE], jnp.zeros((D_MODEL, LANES - A_ROPE), w_in.dtype),
         w_in[:, o_kr + A_ROPE:]], axis=1).astype(BF16)
    w_uq_h = w_uq.reshape(A_Q_LORA, A_HEADS, A_QK)
    w_uq_pad = jnp.pad(w_uq_h, ((0, 0), (0, 0), (0, A_QPAD - A_QK))).reshape(A_Q_LORA, A_HEADS * A_QPAD)
    w_ukv_h = w_ukv.reshape(A_KV_LORA, A_HEADS, A_NOPE + A_VDIM)
    w_kn = w_ukv_h[:, :, :A_NOPE].reshape(A_KV_LORA, A_HEADS * A_NOPE)
    w_v = w_ukv_h[:, :, A_NOPE:].reshape(A_KV_LORA, A_WIDTH)
    return w_in_pad, w_uq_pad.T.astype(BF16), w_kn.astype(BF16), w_v.T.astype(BF16)


def _swa_feature_perm():
    hi, g, d = jnp.meshgrid(jnp.arange(B_GROUP), jnp.arange(B_KV_HEADS), jnp.arange(B_HEAD_DIM),
                            indexing="ij")
    return ((g * B_GROUP + hi) * B_HEAD_DIM + d).reshape(-1)


def kernel(x, w_in_a, q_norm_a, kv_norm_a, w_uq_a, w_ukv_a, w_o_a, w_kv_shared, w_in_b, sinks_b,
           w_o_b, rel_bias_table, ln_gain, ln_bias):
    B, S, _ = x.shape
    tm = min(512, S)
    tq = min(512, S)
    rope = _rope_tables(S)

    ks = vsT = None
    for i in range(N_A_LAYERS):
        w_in, w_uqT, w_kn, w_vT = _mla_weights(w_in_a[i], w_uq_a[i], w_ukv_a[i])
        qT, kn, kr, vT, sg = _mla_proj(
            x, w_in, q_norm_a[i][None, :], kv_norm_a[i][None, :], w_uqT, w_kn, w_vT, rope, tm, tq, tq)
        o = _mla_attn(qT, kn, kr, vT)
        g, b = ln_gain[i][None, :], ln_bias[i][None, :]
        wo = w_o_a[i].astype(BF16)
        if i < N_A_LAYERS - 1:
            x = _out_ln(o, sg, x, wo, g, b, tm)
        else:
            x, ks, vsT = _out_ln(o, sg, x, wo, g, b, tm,
                                 kv_weights=(w_kv_shared[:, :B_KV_WIDTH].astype(BF16),
                                             w_kv_shared[:, B_KV_WIDTH:].T.astype(BF16)))

    bias = _band_bias(rel_bias_table)
    perm = _swa_feature_perm()
    for j in range(DEPTH - N_A_LAYERS):
        layer = N_A_LAYERS + j
        wqT = w_in_b[j][:, :B_WIDTH].T.astype(BF16)
        wg = w_in_b[j][:, B_WIDTH:][:, perm].astype(BF16)
        wo = w_o_b[j][perm, :].astype(BF16)
        sink = jnp.repeat(sinks_b[j].astype(F32).reshape(B_KV_HEADS, 1, B_GROUP), Q_BLOCK, axis=2)
        x = _swa_layer(x, wqT, wg, wo, ks, vsT, bias, sink,
                       ln_gain[layer][None, :], ln_bias[layer][None, :], min(256, S))
    return x
```

```python
import functools
import math

import jax
import jax.numpy as jnp
from jax import lax
from jax.experimental import pallas as pl
from jax.experimental.pallas import tpu as pltpu

F32 = jnp.float32
BF16 = jnp.bfloat16

D_MODEL = 1024
DEPTH = 4
CHUNK = 64
Q_BLOCK = 128
N_A_LAYERS = DEPTH // 2
A_HEADS = 8
A_NOPE = 128
A_ROPE = 64
A_VDIM = 128
A_QK = A_NOPE + A_ROPE
A_Q_LORA = 384
A_KV_LORA = 256
A_WIDTH = A_HEADS * A_VDIM
A_QPAD = 256
A_VROWS = A_VDIM + 16
ROPE_THETA = 10000.0
ROPE_HALF = A_ROPE // 2
B_Q_HEADS = 16
B_KV_HEADS = 2
B_GROUP = B_Q_HEADS // B_KV_HEADS
B_HEAD_DIM = 64
B_WIDTH = B_Q_HEADS * B_HEAD_DIM
B_KV_WIDTH = B_KV_HEADS * B_HEAD_DIM
WINDOW = 128
WIN_CHUNKS = WINDOW // CHUNK
NUM_BUCKETS = 32
MAX_DISTANCE = 128
DEEPNORM_ALPHA = (2 * DEPTH) ** 0.25
NORM_EPS = 1e-5
MASK_VALUE = -1e30

LANES = 128
VMEM_LIMIT_BYTES = 56 * 1024 * 1024

_NT = (((1,), (1,)), ((), ()))
UNROLL = 8
BOUND_MARGIN = 1.0 + 2.0 ** -6
DENOM_FLOOR = 2.0 ** -80


def _cparams(n_axes, flags=None):
    return pltpu.CompilerParams(
        dimension_semantics=("arbitrary",) * n_axes, vmem_limit_bytes=VMEM_LIMIT_BYTES, flags=flags)


def _sigmoid(x):
    return 1.0 / (1.0 + jnp.exp(-x))


def _rms_scale(c, g):
    return c * lax.rsqrt(jnp.mean(c * c, axis=-1, keepdims=True) + NORM_EPS) * g


def _deepnorm_ln(x, y, g, b):
    z = DEEPNORM_ALPHA * x + y
    mu = jnp.mean(z, axis=-1, keepdims=True)
    zc = z - mu
    var = jnp.mean(zc * zc, axis=-1, keepdims=True)
    return zc * lax.rsqrt(var + NORM_EPS) * g + b


def _mla_proj_kernel(x_ref, w_in_ref, gq_ref, gkv_ref, w_uqT_ref, w_kn_ref, w_vT_ref,
                     cosT_ref, sinT_ref, cosk_ref, sinka_ref, sinkb_ref, ktail_ref, headsel_ref,
                     qT_ref, kn_ref, kr_ref, vT_ref, sg_ref, kmax_ref):
    i = pl.program_id(1)
    xb = x_ref[...].astype(BF16)
    h = jnp.dot(xb, w_in_ref[...], preferred_element_type=F32)
    o_kv = A_Q_LORA
    o_kr = A_Q_LORA + A_KV_LORA
    o_g = o_kr + LANES
    cq = _rms_scale(h[:, :o_kv], gq_ref[...]).astype(BF16)
    ckv = _rms_scale(h[:, o_kv:o_kr], gkv_ref[...]).astype(BF16)

    scale = A_QK ** -0.5 * math.log2(math.e)
    qT = lax.dot_general(w_uqT_ref[...], cq, _NT, preferred_element_type=F32) * scale

    krs = h[:, o_kr:o_g]
    kr = (krs * cosk_ref[...]
          + pltpu.roll(krs, LANES - ROPE_HALF, axis=1) * sinka_ref[...]
          + pltpu.roll(krs, ROPE_HALF, axis=1) * sinkb_ref[...])
    kr_ref[...] = (kr + ktail_ref[...]).astype(BF16)
    kn = jnp.dot(ckv, w_kn_ref[...], preferred_element_type=F32)

    kn2 = jnp.dot((kn * kn).astype(BF16), headsel_ref[...], preferred_element_type=F32)
    k2 = kn2 + jnp.sum(kr * kr, axis=1, keepdims=True)
    @pl.when(i == 0)
    def _start_of_sequence():
        kmax_ref[...] = jnp.zeros(kmax_ref.shape, F32)

    kmax2 = jnp.maximum(kmax_ref[...], jnp.max(k2, axis=0, keepdims=True))
    kmax_ref[...] = kmax2

    cos = cosT_ref[...]
    sin = sinT_ref[...]
    n_r1 = A_NOPE
    n_r2 = n_r1 + ROPE_HALF
    n_r3 = n_r2 + ROPE_HALF
    tail_row = lax.broadcasted_iota(jnp.int32, (A_QPAD - A_QK, qT.shape[1]), 0)
    bound_row = vT_ref.shape[-1] // CHUNK
    for hh in range(A_HEADS):
        r0 = hh * A_QPAD
        t1 = qT[r0 + n_r1:r0 + n_r2]
        t2 = qT[r0 + n_r2:r0 + n_r3]
        qT_ref[hh, :n_r1, :] = qT[r0:r0 + n_r1].astype(BF16)
        qT_ref[hh, n_r1:n_r2, :] = (t1 * cos - t2 * sin).astype(BF16)
        qT_ref[hh, n_r2:n_r3, :] = (t2 * cos + t1 * sin).astype(BF16)
        qh = qT[r0:r0 + A_QK]
        q2 = jnp.sum(qh * qh, axis=0, keepdims=True)
        bound = jnp.sqrt(q2 * kmax2[:, hh:hh + 1]) * BOUND_MARGIN
        qT_ref[hh, n_r3:, :] = jnp.where(tail_row == bound_row, -bound, 0.0).astype(BF16)

    kn = kn.astype(BF16)
    vT = lax.dot_general(w_vT_ref[...], ckv, _NT, preferred_element_type=F32).astype(BF16)
    tk = vT_ref.shape[-1]
    extra = (lax.broadcasted_iota(jnp.int32, (A_VROWS - A_VDIM, tk), 0) == 0).astype(BF16)
    for hh in range(A_HEADS):
        kn_ref[hh] = kn[:, hh * A_NOPE:(hh + 1) * A_NOPE]
        for n in range(vT_ref.shape[1]):
            vT_ref[hh, n, :A_VDIM, :] = vT[hh * A_VDIM:(hh + 1) * A_VDIM, n * tk:(n + 1) * tk]
            vT_ref[hh, n, A_VDIM:, :] = extra

    gate = h[:, o_g:]
    sg_ref[...] = (gate * _sigmoid(gate)).astype(BF16)


def _mla_proj(x, w_in, gq, gkv, w_uqT, w_kn, w_vT, rope, tm, tq, tk):
    B, S, _ = x.shape
    nt = S // tm
    cosT, sinT, cosk, sinka, sinkb = rope
    key = jnp.arange(S)[:, None]
    lane = jnp.arange(LANES)[None, :] - A_ROPE
    ktail = ((lane == (key % tk) // CHUNK) | (lane == tk // CHUNK)).astype(F32)
    headsel = (jnp.arange(A_HEADS * A_NOPE)[:, None] // A_NOPE == jnp.arange(LANES)[None, :]).astype(BF16)
    const = lambda *shape: pl.BlockSpec(shape, lambda b, i: (0,) * len(shape))
    tok_lanes = pl.BlockSpec((tm, LANES), lambda b, i: (i, 0))
    return pl.pallas_call(
        _mla_proj_kernel,
        grid=(B, nt),
        in_specs=[
            pl.BlockSpec((None, tm, D_MODEL), lambda b, i: (b, i, 0)),
            const(*w_in.shape), const(*gq.shape), const(*gkv.shape),
            const(*w_uqT.shape), const(*w_kn.shape), const(*w_vT.shape),
            pl.BlockSpec((ROPE_HALF, tm), lambda b, i: (0, i)),
            pl.BlockSpec((ROPE_HALF, tm), lambda b, i: (0, i)),
            tok_lanes, tok_lanes, tok_lanes, tok_lanes, const(*headsel.shape),
        ],
        out_specs=[
            pl.BlockSpec((None, A_HEADS, None, A_QPAD, tm),
                         lambda b, i: (b, 0, i // (tq // tm), 0, i % (tq // tm))),
            pl.BlockSpec((None, A_HEADS, tm, A_NOPE), lambda b, i: (b, 0, i, 0)),
            pl.BlockSpec((None, tm, LANES), lambda b, i: (b, i, 0)),
            pl.BlockSpec((None, A_HEADS, tm // tk, A_VROWS, tk), lambda b, i: (b, 0, i, 0, 0)),
            pl.BlockSpec((None, tm, A_WIDTH), lambda b, i: (b, i, 0)),
        ],
        out_shape=[
            jax.ShapeDtypeStruct((B, A_HEADS, S // tq, A_QPAD, tq), BF16),
            jax.ShapeDtypeStruct((B, A_HEADS, S, A_NOPE), BF16),
            jax.ShapeDtypeStruct((B, S, LANES), BF16),
            jax.ShapeDtypeStruct((B, A_HEADS, S // tk, A_VROWS, tk), BF16),
            jax.ShapeDtypeStruct((B, S, A_WIDTH), BF16),
        ],
        scratch_shapes=[pltpu.VMEM((1, LANES), F32)],
        compiler_params=_cparams(2),
        name="mla_proj",
    )(x, w_in, gq, gkv, w_uqT, w_kn, w_vT, cosT, sinT, cosk, sinka, sinkb, ktail, headsel)


def _mla_attn_kernel(qT_ref, kn_ref, kr_ref, vT_ref, mrow_ref, o_ref, *scratch, nq, tq):
    *p_refs, acc_ref = scratch
    n_buf = len(p_refs)

    def shifted_scores(qi, t):
        k0 = pl.multiple_of(t * tq, tq)
        kcat = jnp.concatenate([kn_ref[pl.ds(k0, tq), :], kr_ref[pl.ds(k0, tq), :]], axis=1)
        tail = qT_ref[qi, A_QK:, :] + mrow_ref[(t == qi).astype(jnp.int32)]
        q_full = jnp.concatenate([qT_ref[qi, :A_QK, :], tail], axis=0)
        return jnp.dot(kcat, q_full, preferred_element_type=F32)

    def weighted_values(qi, t, buf):
        acc_ref[qi] += jnp.dot(vT_ref[t], p_refs[buf][...], preferred_element_type=F32)

    acc_ref[...] = jnp.zeros(acc_ref.shape, F32)
    for p_ref in p_refs[-2:]:
        p_ref[...] = jnp.zeros(p_ref.shape, BF16)

    def step(carry, buf):
        qi, t, qi_1, t_1, qi_2, t_2 = carry
        weighted_values(qi_2, t_2, (buf - 2) % n_buf)
        p_refs[buf][...] = jnp.exp2(shifted_scores(qi, t)).astype(BF16)
        last = t == qi
        return jnp.where(last, qi + 1, qi), jnp.where(last, 0, t + 1), qi, t, qi_1, t_1

    carry = (jnp.int32(0),) * 6
    n_steps = nq * (nq + 1) // 2
    n_lead = n_steps % UNROLL
    for u in range(n_lead):
        carry = step(carry, u % n_buf)

    def steps(it, carry):
        for u in range(UNROLL):
            carry = step(carry, (n_lead + u) % n_buf)
        return carry

    _, _, qi_1, t_1, qi_2, t_2 = lax.fori_loop(0, n_steps // UNROLL, steps, carry)
    weighted_values(qi_2, t_2, (n_steps - 2) % n_buf)
    weighted_values(qi_1, t_1, (n_steps - 1) % n_buf)

    def write_out(qi, acc):
        r0 = pl.multiple_of(qi * tq, tq)
        o_ref[pl.ds(r0, tq), :] = (acc[:A_VDIM] / acc[A_VDIM:A_VDIM + 1]).T.astype(o_ref.dtype)

    def finish(qi, denom_min):
        acc = acc_ref[qi]
        write_out(qi, acc)
        return jnp.minimum(denom_min, acc[A_VDIM:A_VDIM + 1])

    denom_min = lax.fori_loop(0, nq, finish, jnp.full((1, tq), jnp.inf, F32))

    @pl.when(jnp.logical_not(jnp.min(denom_min) >= DENOM_FLOOR))
    def _redo_with_running_max():
        def query_tile(qi, carry):
            def key_tile(t, mc):
                m, acc = mc
                s = shifted_scores(qi, t)
                m_new = jnp.maximum(m, jnp.max(s, axis=0, keepdims=True))
                p = jnp.exp2(s - m_new).astype(BF16)
                acc = acc * jnp.exp2(m - m_new) + jnp.dot(vT_ref[t], p, preferred_element_type=F32)
                return m_new, acc

            init = (jnp.full((1, tq), MASK_VALUE, F32), jnp.zeros((A_VROWS, tq), F32))
            _, acc = lax.fori_loop(0, qi + 1, key_tile, init)
            write_out(qi, acc)
            return carry

        lax.fori_loop(0, nq, query_tile, 0)


def _mla_attn(qT, kn, kr, vT):
    B, H, nq, _, tq = qT.shape
    S = kn.shape[2]
    assert vT.shape[2:] == (nq, A_VROWS, tq) and tq // CHUNK < A_QPAD - A_QK
    a = jnp.arange(A_QPAD - A_QK)[:, None]
    qchunk = (jnp.arange(tq) // CHUNK)[None, :]
    diagonal = jnp.where((qchunk < a) & (a < tq // CHUNK), MASK_VALUE, 0.0)
    mrow = jnp.stack([jnp.zeros_like(diagonal), diagonal]).astype(BF16)
    return pl.pallas_call(
        functools.partial(_mla_attn_kernel, nq=nq, tq=tq),
        grid=(B, H),
        in_specs=[
            pl.BlockSpec((None, None, nq, A_QPAD, tq), lambda b, h: (b, h, 0, 0, 0)),
            pl.BlockSpec((None, None, S, A_NOPE), lambda b, h: (b, h, 0, 0)),
            pl.BlockSpec((None, S, LANES), lambda b, h: (b, 0, 0)),
            pl.BlockSpec((None, None, nq, A_VROWS, tq), lambda b, h: (b, h, 0, 0, 0)),
            pl.BlockSpec(mrow.shape, lambda b, h: (0, 0, 0)),
        ],
        out_specs=pl.BlockSpec((None, None, S, A_VDIM), lambda b, h: (b, h, 0, 0)),
        out_shape=jax.ShapeDtypeStruct((B, H, S, A_VDIM), BF16),
        scratch_shapes=[
            *([pltpu.VMEM((tq, tq), BF16)] * UNROLL),
            pltpu.VMEM((nq, A_VROWS, tq), F32),
        ],
        compiler_params=_cparams(2),
        name="mla_attn",
    )(qT, kn, kr, vT, mrow)


def _gated(o_ref, sg_ref):
    o = jnp.concatenate([o_ref[hh] for hh in range(o_ref.shape[0])], axis=1)
    return o * sg_ref[...]


def _out_ln_kernel(o_ref, sg_ref, x_ref, wo_ref, g_ref, b_ref, xo_ref):
    y = jnp.dot(_gated(o_ref, sg_ref), wo_ref[...], preferred_element_type=F32)
    xo_ref[...] = _deepnorm_ln(x_ref[...], y, g_ref[...], b_ref[...])


def _out_ln_kv_kernel(o_ref, sg_ref, x_ref, wo_ref, g_ref, b_ref, wk_ref, wvT_ref,
                      xo_ref, ks_ref, vsT_ref):
    y = jnp.dot(_gated(o_ref, sg_ref), wo_ref[...], preferred_element_type=F32)
    xn = _deepnorm_ln(x_ref[...], y, g_ref[...], b_ref[...])
    xo_ref[...] = xn
    xb = xn.astype(BF16)
    ks_ref[...] = jnp.dot(xb, wk_ref[...], preferred_element_type=F32).astype(BF16)
    vT = lax.dot_general(wvT_ref[...], xb, _NT, preferred_element_type=F32).astype(BF16)
    for n in range(vsT_ref.shape[0]):
        vsT_ref[n] = vT[:, n * Q_BLOCK:(n + 1) * Q_BLOCK]


def _out_ln(o, sg, x, wo, g, b, tm, kv_weights=None):
    B, S, _ = x.shape
    row = pl.BlockSpec((None, tm, D_MODEL), lambda b_, i: (b_, i, 0))
    const = lambda *shape: pl.BlockSpec(shape, lambda b_, i: (0,) * len(shape))
    heads = pl.BlockSpec((None, A_HEADS, tm, A_VDIM), lambda b_, i: (b_, 0, i, 0))
    in_specs = [heads, row, row, const(*wo.shape), const(*g.shape), const(*b.shape)]
    if kv_weights is None:
        return pl.pallas_call(
            _out_ln_kernel, grid=(B, S // tm), in_specs=in_specs, out_specs=row,
            out_shape=jax.ShapeDtypeStruct(x.shape, F32),
            compiler_params=_cparams(2), name="out_ln",
        )(o, sg, x, wo, g, b)
    wk, wvT = kv_weights
    nb = tm // Q_BLOCK
    return pl.pallas_call(
        _out_ln_kv_kernel, grid=(B, S // tm),
        in_specs=in_specs + [const(*wk.shape), const(*wvT.shape)],
        out_specs=[
            row,
            pl.BlockSpec((None, tm, B_KV_WIDTH), lambda b_, i: (b_, i, 0)),
            pl.BlockSpec((None, nb, B_KV_WIDTH, Q_BLOCK), lambda b_, i: (b_, i, 0, 0)),
        ],
        out_shape=[
            jax.ShapeDtypeStruct(x.shape, F32),
            jax.ShapeDtypeStruct((B, S, B_KV_WIDTH), BF16),
            jax.ShapeDtypeStruct((B, S // Q_BLOCK, B_KV_WIDTH, Q_BLOCK), BF16),
        ],
        compiler_params=_cparams(2), name="out_ln_kv",
    )(o, sg, x, wo, g, b, wk, wvT)


def _band_bias_kernel(tab_ref, bkt_ref, ok_ref, out_ref):
    bkt = bkt_ref[...]
    ok = ok_ref[...] != 0
    has_prev = lax.broadcasted_iota(jnp.int32, bkt.shape, 0) >= Q_BLOCK
    for h in range(B_Q_HEADS):
        acc = jnp.zeros(bkt.shape, F32)
        for bucket in range(NUM_BUCKETS):
            acc = jnp.where(bkt == bucket, tab_ref[bucket, h], acc)
        generic = jnp.where(ok, acc, MASK_VALUE)
        g, hi = divmod(h, B_GROUP)
        out_ref[0, g, :, hi * Q_BLOCK:(hi + 1) * Q_BLOCK] = jnp.where(has_prev, generic, MASK_VALUE)
        out_ref[1, g, :, hi * Q_BLOCK:(hi + 1) * Q_BLOCK] = generic


def _t5_bucket_map(rel):
    half = NUM_BUCKETS // 2
    ret = jnp.where(rel > 0, half, 0)
    n = jnp.abs(rel)
    max_exact = half // 2
    large = max_exact + (jnp.log(jnp.maximum(n, 1).astype(F32) / max_exact)
                         / math.log(MAX_DISTANCE / max_exact) * (half - max_exact)).astype(jnp.int32)
    large = jnp.minimum(large, half - 1)
    return ret + jnp.where(n < max_exact, n, large)


def _band_bias(rel_bias_table):
    j = jnp.arange(2 * Q_BLOCK)[:, None]
    r = jnp.arange(Q_BLOCK)[None, :]
    rel = j - Q_BLOCK - r
    d = (r // CHUNK) - ((j - Q_BLOCK) // CHUNK)
    ok = ((d >= 0) & (d <= WIN_CHUNKS)).astype(jnp.int32)
    bkt = _t5_bucket_map(rel).astype(jnp.int32)
    return pl.pallas_call(
        _band_bias_kernel,
        in_specs=[
            pl.BlockSpec(memory_space=pltpu.SMEM),
            pl.BlockSpec(memory_space=pltpu.VMEM),
            pl.BlockSpec(memory_space=pltpu.VMEM),
        ],
        out_specs=pl.BlockSpec(memory_space=pltpu.VMEM),
        out_shape=jax.ShapeDtypeStruct((2, B_KV_HEADS, 2 * Q_BLOCK, B_GROUP * Q_BLOCK), F32),
        name="band_bias",
    )(rel_bias_table, bkt, ok)


def _swa_layer_kernel(x_ref, wqT_ref, wg_ref, wo_ref, ks_ref, vsT_ref, bias_ref, sink_ref,
                      g_ref, b_ref, xo_ref, og_ref, *, nblk):
    i = pl.program_id(1)
    x = x_ref[...]
    xb = x.astype(BF16)
    qT = (lax.dot_general(wqT_ref[...], xb, _NT, preferred_element_type=F32)
          * (B_HEAD_DIM ** -0.5)).astype(BF16)
    gate = jnp.dot(xb, wg_ref[...], preferred_element_type=F32)
    sg = gate * _sigmoid(gate)
    zeros = jnp.zeros((B_HEAD_DIM, B_GROUP * Q_BLOCK), BF16)

    for n in range(nblk):
        blk = i * nblk + n
        cur = pl.multiple_of(blk * Q_BLOCK, Q_BLOCK)
        prev_blk = jnp.maximum(blk - 1, 0)
        prev = pl.multiple_of(prev_blk * Q_BLOCK, Q_BLOCK)
        kband = jnp.concatenate([ks_ref[pl.ds(prev, Q_BLOCK), :], ks_ref[pl.ds(cur, Q_BLOCK), :]], axis=0)
        vband = jnp.concatenate([vsT_ref[prev_blk], vsT_ref[blk]], axis=1)
        table = jnp.minimum(blk, 1)
        lanes = slice(n * Q_BLOCK, (n + 1) * Q_BLOCK)
        o_parts = []
        for g in range(B_KV_HEADS):
            qg = jnp.concatenate(
                [qT[(g * B_GROUP + hi) * B_HEAD_DIM:(g * B_GROUP + hi + 1) * B_HEAD_DIM, lanes]
                 for hi in range(B_GROUP)], axis=1)
            rhs = jnp.concatenate([qg, zeros] if g == 0 else [zeros, qg], axis=0)
            sT = jnp.dot(kband, rhs, preferred_element_type=F32) + bias_ref[table, g]
            sink = sink_ref[g]
            m = jnp.maximum(jnp.max(sT, axis=0, keepdims=True), sink)
            e = jnp.exp(sT - m)
            den = jnp.sum(e, axis=0, keepdims=True) + jnp.exp(sink - m)
            v_rows = slice(g * B_HEAD_DIM, (g + 1) * B_HEAD_DIM)
            oT = jnp.dot(vband[v_rows, :], e.astype(BF16), preferred_element_type=F32)
            o_parts.append(oT / den)
        for hi in range(B_GROUP):
            cols = slice(hi * Q_BLOCK, (hi + 1) * Q_BLOCK)
            o_hi = jnp.concatenate([o_parts[0][:, cols], o_parts[1][:, cols]], axis=0).T
            og_ref[lanes, cols] = (o_hi * sg[lanes, cols]).astype(BF16)

    y = jnp.dot(og_ref[...], wo_ref[...], preferred_element_type=F32)
    xo_ref[...] = _deepnorm_ln(x, y, g_ref[...], b_ref[...])


def _swa_layer(x, wqT, wg, wo, ks, vsT, bias, sink, g, b, tm):
    B, S, _ = x.shape
    nblk = tm // Q_BLOCK
    row = pl.BlockSpec((None, tm, D_MODEL), lambda b_, i: (b_, i, 0))
    const = lambda *shape: pl.BlockSpec(shape, lambda b_, i: (0,) * len(shape))
    return pl.pallas_call(
        functools.partial(_swa_layer_kernel, nblk=nblk),
        grid=(B, S // tm),
        in_specs=[
            row, const(*wqT.shape), const(*wg.shape), const(*wo.shape),
            pl.BlockSpec((None, S, B_KV_WIDTH), lambda b_, i: (b_, 0, 0)),
            pl.BlockSpec((None, S // Q_BLOCK, B_KV_WIDTH, Q_BLOCK), lambda b_, i: (b_, 0, 0, 0)),
            const(*bias.shape), const(*sink.shape), const(*g.shape), const(*b.shape),
        ],
        out_specs=row,
        out_shape=jax.ShapeDtypeStruct(x.shape, F32),
        scratch_shapes=[pltpu.VMEM((tm, B_WIDTH), BF16)],
        compiler_params=_cparams(2),
        name="swa_layer",
    )(x, wqT, wg, wo, ks, vsT, bias, sink, g, b)


def _rope_tables(S):
    inv = ROPE_THETA ** (-jnp.arange(ROPE_HALF, dtype=F32) / ROPE_HALF)
    ang = jnp.arange(S, dtype=F32)[:, None] * inv[None, :]
    cos, sin = jnp.cos(ang), jnp.sin(ang)
    zeros = jnp.zeros((S, LANES - A_ROPE), F32)
    z32 = jnp.zeros((S, ROPE_HALF), F32)
    cosk = jnp.concatenate([cos, cos, zeros], axis=1)
    sinka = jnp.concatenate([-sin, z32, zeros], axis=1)
    sinkb = jnp.concatenate([z32, sin, zeros], axis=1)
    return cos.T, sin.T, cosk, sinka, sinkb


def _mla_weights(w_in, w_uq, w_ukv):
    o_kr = A_Q_LORA + A_KV_LORA
    w_in_pad = jnp.concatenate(
        [w_in[:, :o_kr + A_ROPE], jnp.zeros((D_MODEL, LANES - A_ROPE), w_in.dtype),
         w_in[:, o_kr + A_ROPE:]], axis=1).astype(BF16)
    w_uq_h = w_uq.reshape(A_Q_LORA, A_HEADS, A_QK)
    w_uq_pad = jnp.pad(w_uq_h, ((0, 0), (0, 0), (0, A_QPAD - A_QK))).reshape(A_Q_LORA, A_HEADS * A_QPAD)
    w_ukv_h = w_ukv.reshape(A_KV_LORA, A_HEADS, A_NOPE + A_VDIM)
    w_kn = w_ukv_h[:, :, :A_NOPE].reshape(A_KV_LORA, A_HEADS * A_NOPE)
    w_v = w_ukv_h[:, :, A_NOPE:].reshape(A_KV_LORA, A_WIDTH)
    return w_in_pad, w_uq_pad.T.astype(BF16), w_kn.astype(BF16), w_v.T.astype(BF16)


def _swa_feature_perm():
    hi, g, d = jnp.meshgrid(jnp.arange(B_GROUP), jnp.arange(B_KV_HEADS), jnp.arange(B_HEAD_DIM),
                            indexing="ij")
    return ((g * B_GROUP + hi) * B_HEAD_DIM + d).reshape(-1)


def kernel(x, w_in_a, q_norm_a, kv_norm_a, w_uq_a, w_ukv_a, w_o_a, w_kv_shared, w_in_b, sinks_b,
           w_o_b, rel_bias_table, ln_gain, ln_bias):
    B, S, _ = x.shape
    tm = min(512, S)
    tq = min(512, S)
    rope = _rope_tables(S)

    ks = vsT = None
    for i in range(N_A_LAYERS):
        w_in, w_uqT, w_kn, w_vT = _mla_weights(w_in_a[i], w_uq_a[i], w_ukv_a[i])
        qT, kn, kr, vT, sg = _mla_proj(
            x, w_in, q_norm_a[i][None, :], kv_norm_a[i][None, :], w_uqT, w_kn, w_vT, rope, tm, tq, tq)
        o = _mla_attn(qT, kn, kr, vT)
        g, b = ln_gain[i][None, :], ln_bias[i][None, :]
        wo = w_o_a[i].astype(BF16)
        if i < N_A_LAYERS - 1:
            x = _out_ln(o, sg, x, wo, g, b, tm)
        else:
            x, ks, vsT = _out_ln(o, sg, x, wo, g, b, tm,
                                 kv_weights=(w_kv_shared[:, :B_KV_WIDTH].astype(BF16),
                                             w_kv_shared[:, B_KV_WIDTH:].T.astype(BF16)))

    bias = _band_bias(rel_bias_table)
    perm = _swa_feature_perm()
    for j in range(DEPTH - N_A_LAYERS):
        layer = N_A_LAYERS + j
        wqT = w_in_b[j][:, :B_WIDTH].T.astype(BF16)
        wg = w_in_b[j][:, B_WIDTH:][:, perm].astype(BF16)
        wo = w_o_b[j][perm, :].astype(BF16)
        sink = jnp.repeat(sinks_b[j].astype(F32).reshape(B_KV_HEADS, 1, B_GROUP), Q_BLOCK, axis=2)
        x = _swa_layer(x, wqT, wg, wo, ks, vsT, bias, sink,
                       ln_gain[layer][None, :], ln_bias[layer][None, :], min(512, S))
    return x
```

```python
import functools
import math

import jax
import jax.numpy as jnp
import numpy as np
from jax import lax
from jax.experimental import pallas as pl
from jax.experimental.pallas import tpu as pltpu

F32 = jnp.float32
BF16 = jnp.bfloat16

D_MODEL = 1024
DEPTH = 4
CHUNK = 64
Q_BLOCK = 128
N_A_LAYERS = DEPTH // 2
A_HEADS = 8
A_NOPE = 128
A_ROPE = 64
A_VDIM = 128
A_QK = A_NOPE + A_ROPE
A_Q_LORA = 384
A_KV_LORA = 256
A_WIDTH = A_HEADS * A_VDIM
A_QPAD = 256
A_VROWS = A_VDIM + 16
ROPE_THETA = 10000.0
ROPE_HALF = A_ROPE // 2
B_Q_HEADS = 16
B_KV_HEADS = 2
B_GROUP = B_Q_HEADS // B_KV_HEADS
B_HEAD_DIM = 64
B_WIDTH = B_Q_HEADS * B_HEAD_DIM
B_KV_WIDTH = B_KV_HEADS * B_HEAD_DIM
WINDOW = 128
WIN_CHUNKS = WINDOW // CHUNK
NUM_BUCKETS = 32
MAX_DISTANCE = 128
DEEPNORM_ALPHA = (2 * DEPTH) ** 0.25
NORM_EPS = 1e-5
MASK_VALUE = -1e30

LANES = 128
VMEM_LIMIT_BYTES = 56 * 1024 * 1024

_NT = (((1,), (1,)), ((), ()))
UNROLL = 8
BOUND_MARGIN = 1.0 + 2.0 ** -6
DENOM_FLOOR = 2.0 ** -80


def _cparams(n_axes, flags=None):
    return pltpu.CompilerParams(
        dimension_semantics=("arbitrary",) * n_axes, vmem_limit_bytes=VMEM_LIMIT_BYTES, flags=flags)


def _sigmoid(x):
    return 1.0 / (1.0 + jnp.exp(-x))


def _rms_scale(c, g):
    return c * lax.rsqrt(jnp.mean(c * c, axis=-1, keepdims=True) + NORM_EPS) * g


def _deepnorm_ln(x, y, g, b):
    z = DEEPNORM_ALPHA * x + y
    mu = jnp.mean(z, axis=-1, keepdims=True)
    zc = z - mu
    var = jnp.mean(zc * zc, axis=-1, keepdims=True)
    return zc * lax.rsqrt(var + NORM_EPS) * g + b


def _mla_proj_kernel(x_ref, w_in_ref, gq_ref, gkv_ref, w_uqT_ref, w_kn_ref, w_vT_ref,
                     cosT_ref, sinT_ref, cosk_ref, sinka_ref, sinkb_ref, ktail_ref, headsel_ref,
                     qT_ref, kn_ref, kr_ref, vT_ref, sg_ref, kmax_ref):
    @pl.when(pl.program_id(1) == 0)
    def _start_of_sequence():
        kmax_ref[...] = jnp.zeros(kmax_ref.shape, F32)

    xb = x_ref[...].astype(BF16)
    h = jnp.dot(xb, w_in_ref[...], preferred_element_type=F32)
    o_kv = A_Q_LORA
    o_kr = A_Q_LORA + A_KV_LORA
    o_g = o_kr + LANES
    cq = _rms_scale(h[:, :o_kv], gq_ref[...]).astype(BF16)
    ckv = _rms_scale(h[:, o_kv:o_kr], gkv_ref[...]).astype(BF16)

    scale = A_QK ** -0.5 * math.log2(math.e)
    qT = lax.dot_general(w_uqT_ref[...], cq, _NT, preferred_element_type=F32) * scale

    krs = h[:, o_kr:o_g]
    kr = (krs * cosk_ref[...]
          + pltpu.roll(krs, LANES - ROPE_HALF, axis=1) * sinka_ref[...]
          + pltpu.roll(krs, ROPE_HALF, axis=1) * sinkb_ref[...])
    kr_ref[...] = (kr + ktail_ref[...]).astype(BF16)
    kn = jnp.dot(ckv, w_kn_ref[...], preferred_element_type=F32)

    kn2 = jnp.dot((kn * kn).astype(BF16), headsel_ref[...], preferred_element_type=F32)
    k2 = kn2 + jnp.sum(kr * kr, axis=1, keepdims=True)
    kmax2 = jnp.maximum(kmax_ref[...], jnp.max(k2, axis=0, keepdims=True))
    kmax_ref[...] = kmax2

    cos = cosT_ref[...]
    sin = sinT_ref[...]
    n_r1 = A_NOPE
    n_r2 = n_r1 + ROPE_HALF
    n_r3 = n_r2 + ROPE_HALF
    tail_row = lax.broadcasted_iota(jnp.int32, (A_QPAD - A_QK, qT.shape[1]), 0)
    bound_row = vT_ref.shape[-1] // CHUNK
    for hh in range(A_HEADS):
        r0 = hh * A_QPAD
        t1 = qT[r0 + n_r1:r0 + n_r2]
        t2 = qT[r0 + n_r2:r0 + n_r3]
        qT_ref[hh, :n_r1, :] = qT[r0:r0 + n_r1].astype(BF16)
        qT_ref[hh, n_r1:n_r2, :] = (t1 * cos - t2 * sin).astype(BF16)
        qT_ref[hh, n_r2:n_r3, :] = (t2 * cos + t1 * sin).astype(BF16)
        qh = qT[r0:r0 + A_QK]
        q2 = jnp.sum(qh * qh, axis=0, keepdims=True)
        bound = jnp.sqrt(q2 * kmax2[:, hh:hh + 1]) * BOUND_MARGIN
        qT_ref[hh, n_r3:, :] = jnp.where(tail_row == bound_row, -bound, 0.0).astype(BF16)

    kn = kn.astype(BF16)
    vT = lax.dot_general(w_vT_ref[...], ckv, _NT, preferred_element_type=F32).astype(BF16)
    tk = vT_ref.shape[-1]
    extra = (lax.broadcasted_iota(jnp.int32, (A_VROWS - A_VDIM, tk), 0) == 0).astype(BF16)
    for hh in range(A_HEADS):
        kn_ref[hh] = kn[:, hh * A_NOPE:(hh + 1) * A_NOPE]
        for n in range(vT_ref.shape[1]):
            vT_ref[hh, n, :A_VDIM, :] = vT[hh * A_VDIM:(hh + 1) * A_VDIM, n * tk:(n + 1) * tk]
            vT_ref[hh, n, A_VDIM:, :] = extra

    gate = h[:, o_g:]
    sg_ref[...] = (gate * _sigmoid(gate)).astype(BF16)


def _mla_proj(x, w_in, gq, gkv, w_uqT, w_kn, w_vT, rope, tm, tq, tk):
    B, S, _ = x.shape
    nt = S // tm
    cosT, sinT, cosk, sinka, sinkb = rope
    key = np.arange(S)[:, None]
    lane = np.arange(LANES)[None, :] - A_ROPE
    ktail = jnp.asarray((lane == (key % tk) // CHUNK) | (lane == tk // CHUNK), F32)
    headsel = jnp.asarray(
        np.arange(A_HEADS * A_NOPE)[:, None] // A_NOPE == np.arange(LANES)[None, :], BF16)
    const = lambda *shape: pl.BlockSpec(shape, lambda b, i: (0,) * len(shape))
    tok_lanes = pl.BlockSpec((tm, LANES), lambda b, i: (i, 0))
    return pl.pallas_call(
        _mla_proj_kernel,
        grid=(B, nt),
        in_specs=[
            pl.BlockSpec((None, tm, D_MODEL), lambda b, i: (b, i, 0)),
            const(*w_in.shape), const(*gq.shape), const(*gkv.shape),
            const(*w_uqT.shape), const(*w_kn.shape), const(*w_vT.shape),
            pl.BlockSpec((ROPE_HALF, tm), lambda b, i: (0, i)),
            pl.BlockSpec((ROPE_HALF, tm), lambda b, i: (0, i)),
            tok_lanes, tok_lanes, tok_lanes, tok_lanes, const(*headsel.shape),
        ],
        out_specs=[
            pl.BlockSpec((None, A_HEADS, None, A_QPAD, tm),
                         lambda b, i: (b, 0, i // (tq // tm), 0, i % (tq // tm))),
            pl.BlockSpec((None, A_HEADS, tm, A_NOPE), lambda b, i: (b, 0, i, 0)),
            pl.BlockSpec((None, tm, LANES), lambda b, i: (b, i, 0)),
            pl.BlockSpec((None, A_HEADS, tm // tk, A_VROWS, tk), lambda b, i: (b, 0, i, 0, 0)),
            pl.BlockSpec((None, tm, A_WIDTH), lambda b, i: (b, i, 0)),
        ],
        out_shape=[
            jax.ShapeDtypeStruct((B, A_HEADS, S // tq, A_QPAD, tq), BF16),
            jax.ShapeDtypeStruct((B, A_HEADS, S, A_NOPE), BF16),
            jax.ShapeDtypeStruct((B, S, LANES), BF16),
            jax.ShapeDtypeStruct((B, A_HEADS, S // tk, A_VROWS, tk), BF16),
            jax.ShapeDtypeStruct((B, S, A_WIDTH), BF16),
        ],
        scratch_shapes=[pltpu.VMEM((1, LANES), F32)],
        compiler_params=_cparams(2),
        name="mla_proj",
    )(x, w_in, gq, gkv, w_uqT, w_kn, w_vT, cosT, sinT, cosk, sinka, sinkb, ktail, headsel)


def _mla_attn_kernel(qT_ref, kn_ref, kr_ref, vT_ref, mrow_ref, o_ref, *scratch, nq, tq):
    *p_refs, acc_ref = scratch
    n_buf = len(p_refs)

    def shifted_scores(qi, t):
        k0 = pl.multiple_of(t * tq, tq)
        kcat = jnp.concatenate([kn_ref[pl.ds(k0, tq), :], kr_ref[pl.ds(k0, tq), :]], axis=1)
        tail = qT_ref[qi, A_QK:, :] + mrow_ref[(t == qi).astype(jnp.int32)]
        q_full = jnp.concatenate([qT_ref[qi, :A_QK, :], tail], axis=0)
        return jnp.dot(kcat, q_full, preferred_element_type=F32)

    def weighted_values(qi, t, buf):
        acc_ref[qi] += jnp.dot(vT_ref[t], p_refs[buf][...], preferred_element_type=F32)

    acc_ref[...] = jnp.zeros(acc_ref.shape, F32)
    for p_ref in p_refs[-2:]:
        p_ref[...] = jnp.zeros(p_ref.shape, BF16)

    def step(carry, buf):
        qi, t, qi_1, t_1, qi_2, t_2 = carry
        weighted_values(qi_2, t_2, (buf - 2) % n_buf)
        p_refs[buf][...] = jnp.exp2(shifted_scores(qi, t)).astype(BF16)
        last = t == qi
        return jnp.where(last, qi + 1, qi), jnp.where(last, 0, t + 1), qi, t, qi_1, t_1

    carry = (jnp.int32(0),) * 6
    n_steps = nq * (nq + 1) // 2
    n_lead = n_steps % UNROLL
    for u in range(n_lead):
        carry = step(carry, u % n_buf)

    def steps(it, carry):
        for u in range(UNROLL):
            carry = step(carry, (n_lead + u) % n_buf)
        return carry

    _, _, qi_1, t_1, qi_2, t_2 = lax.fori_loop(0, n_steps // UNROLL, steps, carry)
    weighted_values(qi_2, t_2, (n_steps - 2) % n_buf)
    weighted_values(qi_1, t_1, (n_steps - 1) % n_buf)

    def write_out(qi, acc):
        r0 = pl.multiple_of(qi * tq, tq)
        o_ref[pl.ds(r0, tq), :] = (acc[:A_VDIM] / acc[A_VDIM:A_VDIM + 1]).T.astype(o_ref.dtype)

    def finish(qi, denom_min):
        acc = acc_ref[qi]
        write_out(qi, acc)
        return jnp.minimum(denom_min, acc[A_VDIM:A_VDIM + 1])

    denom_min = lax.fori_loop(0, nq, finish, jnp.full((1, tq), jnp.inf, F32))

    @pl.when(jnp.logical_not(jnp.min(denom_min) >= DENOM_FLOOR))
    def _redo_with_running_max():
        def query_tile(qi, carry):
            def key_tile(t, mc):
                m, acc = mc
                s = shifted_scores(qi, t)
                m_new = jnp.maximum(m, jnp.max(s, axis=0, keepdims=True))
                p = jnp.exp2(s - m_new).astype(BF16)
                acc = acc * jnp.exp2(m - m_new) + jnp.dot(vT_ref[t], p, preferred_element_type=F32)
                return m_new, acc

            init = (jnp.full((1, tq), MASK_VALUE, F32), jnp.zeros((A_VROWS, tq), F32))
            _, acc = lax.fori_loop(0, qi + 1, key_tile, init)
            write_out(qi, acc)
            return carry

        lax.fori_loop(0, nq, query_tile, 0)


def _mla_attn(qT, kn, kr, vT):
    B, H, nq, _, tq = qT.shape
    S = kn.shape[2]
    assert vT.shape[2:] == (nq, A_VROWS, tq) and tq // CHUNK < A_QPAD - A_QK
    a = np.arange(A_QPAD - A_QK)[:, None]
    qchunk = (np.arange(tq) // CHUNK)[None, :]
    diagonal = np.where((qchunk < a) & (a < tq // CHUNK), MASK_VALUE, 0.0).astype(np.float32)
    mrow = jnp.asarray(np.stack([np.zeros_like(diagonal), diagonal]), BF16)
    return pl.pallas_call(
        functools.partial(_mla_attn_kernel, nq=nq, tq=tq),
        grid=(B, H),
        in_specs=[
            pl.BlockSpec((None, None, nq, A_QPAD, tq), lambda b, h: (b, h, 0, 0, 0)),
            pl.BlockSpec((None, None, S, A_NOPE), lambda b, h: (b, h, 0, 0)),
            pl.BlockSpec((None, S, LANES), lambda b, h: (b, 0, 0)),
            pl.BlockSpec((None, None, nq, A_VROWS, tq), lambda b, h: (b, h, 0, 0, 0)),
            pl.BlockSpec(mrow.shape, lambda b, h: (0, 0, 0)),
        ],
        out_specs=pl.BlockSpec((None, None, S, A_VDIM), lambda b, h: (b, h, 0, 0)),
        out_shape=jax.ShapeDtypeStruct((B, H, S, A_VDIM), BF16),
        scratch_shapes=[
            *([pltpu.VMEM((tq, tq), BF16)] * UNROLL),
            pltpu.VMEM((nq, A_VROWS, tq), F32),
        ],
        compiler_params=_cparams(2),
        name="mla_attn",
    )(qT, kn, kr, vT, mrow)


def _gated(o_ref, sg_ref):
    o = jnp.concatenate([o_ref[hh] for hh in range(o_ref.shape[0])], axis=1)
    return o * sg_ref[...]


def _out_ln_kernel(o_ref, sg_ref, x_ref, wo_ref, g_ref, b_ref, xo_ref):
    y = jnp.dot(_gated(o_ref, sg_ref), wo_ref[...], preferred_element_type=F32)
    xo_ref[...] = _deepnorm_ln(x_ref[...], y, g_ref[...], b_ref[...])


def _out_ln_kv_kernel(o_ref, sg_ref, x_ref, wo_ref, g_ref, b_ref, wk_ref, wvT_ref,
                      xo_ref, ks_ref, vsT_ref):
    y = jnp.dot(_gated(o_ref, sg_ref), wo_ref[...], preferred_element_type=F32)
    xn = _deepnorm_ln(x_ref[...], y, g_ref[...], b_ref[...])
    xo_ref[...] = xn
    xb = xn.astype(BF16)
    ks_ref[...] = jnp.dot(xb, wk_ref[...], preferred_element_type=F32).astype(BF16)
    vT = lax.dot_general(wvT_ref[...], xb, _NT, preferred_element_type=F32).astype(BF16)
    for n in range(vsT_ref.shape[0]):
        vsT_ref[n] = vT[:, n * Q_BLOCK:(n + 1) * Q_BLOCK]


def _out_ln(o, sg, x, wo, g, b, tm, kv_weights=None):
    B, S, _ = x.shape
    row = pl.BlockSpec((None, tm, D_MODEL), lambda b_, i: (b_, i, 0))
    const = lambda *shape: pl.BlockSpec(shape, lambda b_, i: (0,) * len(shape))
    heads = pl.BlockSpec((None, A_HEADS, tm, A_VDIM), lambda b_, i: (b_, 0, i, 0))
    in_specs = [heads, row, row, const(*wo.shape), const(*g.shape), const(*b.shape)]
    if kv_weights is None:
        return pl.pallas_call(
            _out_ln_kernel, grid=(B, S // tm), in_specs=in_specs, out_specs=row,
            out_shape=jax.ShapeDtypeStruct(x.shape, F32),
            compiler_params=_cparams(2), name="out_ln",
        )(o, sg, x, wo, g, b)
    wk, wvT = kv_weights
    nb = tm // Q_BLOCK
    return pl.pallas_call(
        _out_ln_kv_kernel, grid=(B, S // tm),
        in_specs=in_specs + [const(*wk.shape), const(*wvT.shape)],
        out_specs=[
            row,
            pl.BlockSpec((None, tm, B_KV_WIDTH), lambda b_, i: (b_, i, 0)),
            pl.BlockSpec((None, nb, B_KV_WIDTH, Q_BLOCK), lambda b_, i: (b_, i, 0, 0)),
        ],
        out_shape=[
            jax.ShapeDtypeStruct(x.shape, F32),
            jax.ShapeDtypeStruct((B, S, B_KV_WIDTH), BF16),
            jax.ShapeDtypeStruct((B, S // Q_BLOCK, B_KV_WIDTH, Q_BLOCK), BF16),
        ],
        compiler_params=_cparams(2), name="out_ln_kv",
    )(o, sg, x, wo, g, b, wk, wvT)


def _band_bias_kernel(tab_ref, bkt_ref, ok_ref, out_ref):
    bkt = bkt_ref[...]
    ok = ok_ref[...] != 0
    has_prev = lax.broadcasted_iota(jnp.int32, bkt.shape, 0) >= Q_BLOCK
    for h in range(B_Q_HEADS):
        acc = jnp.zeros(bkt.shape, F32)
        for bucket in range(NUM_BUCKETS):
            acc = jnp.where(bkt == bucket, tab_ref[bucket, h], acc)
        generic = jnp.where(ok, acc, MASK_VALUE)
        g, hi = divmod(h, B_GROUP)
        out_ref[0, g, :, hi * Q_BLOCK:(hi + 1) * Q_BLOCK] = jnp.where(has_prev, generic, MASK_VALUE)
        out_ref[1, g, :, hi * Q_BLOCK:(hi + 1) * Q_BLOCK] = generic


def _t5_bucket_map(rel):
    half = NUM_BUCKETS // 2
    ret = jnp.where(rel > 0, half, 0)
    n = jnp.abs(rel)
    max_exact = half // 2
    large = max_exact + (jnp.log(jnp.maximum(n, 1).astype(F32) / max_exact)
                         / math.log(MAX_DISTANCE / max_exact) * (half - max_exact)).astype(jnp.int32)
    large = jnp.minimum(large, half - 1)
    return ret + jnp.where(n < max_exact, n, large)


def _band_bias(rel_bias_table):
    j = jnp.arange(2 * Q_BLOCK)[:, None]
    r = jnp.arange(Q_BLOCK)[None, :]
    rel = j - Q_BLOCK - r
    d = (r // CHUNK) - ((j - Q_BLOCK) // CHUNK)
    ok = ((d >= 0) & (d <= WIN_CHUNKS)).astype(jnp.int32)
    bkt = _t5_bucket_map(rel).astype(jnp.int32)
    return pl.pallas_call(
        _band_bias_kernel,
        in_specs=[
            pl.BlockSpec(memory_space=pltpu.SMEM),
            pl.BlockSpec(memory_space=pltpu.VMEM),
            pl.BlockSpec(memory_space=pltpu.VMEM),
        ],
        out_specs=pl.BlockSpec(memory_space=pltpu.VMEM),
        out_shape=jax.ShapeDtypeStruct((2, B_KV_HEADS, 2 * Q_BLOCK, B_GROUP * Q_BLOCK), F32),
        name="band_bias",
    )(rel_bias_table, bkt, ok)


def _swa_layer_kernel(xn_ref, x_ref, wqT_ref, wg_ref, wo_ref, ks_ref, vsT_ref, bias_ref, sink_ref,
                      g_ref, b_ref, xo_ref, qT0_ref, sg0_ref, qT1_ref, sg1_ref, og_ref, *, nblk):
    step = pl.program_id(1)

    @pl.when((pl.program_id(0) == 0) & (step == 0))
    def _first_step():
        qT0_ref[...] = jnp.zeros(qT0_ref.shape, qT0_ref.dtype)
        sg0_ref[...] = jnp.zeros(sg0_ref.shape, sg0_ref.dtype)

    def body(qT_ref, sg_ref, qT_next_ref, sg_next_ref):
        tile = jnp.maximum(step - 1, 0)
        x = x_ref[...]
        xb_next = xn_ref[...].astype(BF16)
        zeros = jnp.zeros((B_HEAD_DIM, B_GROUP * Q_BLOCK), BF16)
        chunk = B_WIDTH // nblk

        for n in range(nblk):
            blk = tile * nblk + n
            cur = pl.multiple_of(blk * Q_BLOCK, Q_BLOCK)
            prev_blk = jnp.maximum(blk - 1, 0)
            prev = pl.multiple_of(prev_blk * Q_BLOCK, Q_BLOCK)
            kband = jnp.concatenate(
                [ks_ref[pl.ds(prev, Q_BLOCK), :], ks_ref[pl.ds(cur, Q_BLOCK), :]], axis=0)
            vband = jnp.concatenate([vsT_ref[prev_blk], vsT_ref[blk]], axis=1)
            table = jnp.minimum(blk, 1)
            lanes = slice(n * Q_BLOCK, (n + 1) * Q_BLOCK)
            scores = []
            for g in range(B_KV_HEADS):
                qg = jnp.concatenate(
                    [qT_ref[(g * B_GROUP + hi) * B_HEAD_DIM:(g * B_GROUP + hi + 1) * B_HEAD_DIM, lanes]
                     for hi in range(B_GROUP)], axis=1)
                rhs = jnp.concatenate([qg, zeros] if g == 0 else [zeros, qg], axis=0)
                scores.append(jnp.dot(kband, rhs, preferred_element_type=F32) + bias_ref[table, g])

            part = slice(n * chunk, (n + 1) * chunk)
            qT_next_ref[part, :] = (
                lax.dot_general(wqT_ref[part, :], xb_next, _NT, preferred_element_type=F32)
                * (B_HEAD_DIM ** -0.5)).astype(BF16)
            gate = jnp.dot(xb_next, wg_ref[:, part], preferred_element_type=F32)
            sg_next_ref[:, part] = gate * _sigmoid(gate)

            o_parts = []
            for g in range(B_KV_HEADS):
                sT = scores[g]
                sink = sink_ref[g]
                m = jnp.maximum(jnp.max(sT, axis=0, keepdims=True), sink)
                e = jnp.exp(sT - m)
                den = jnp.sum(e, axis=0, keepdims=True) + jnp.exp(sink - m)
                v_rows = slice(g * B_HEAD_DIM, (g + 1) * B_HEAD_DIM)
                oT = jnp.dot(vband[v_rows, :], e.astype(BF16), preferred_element_type=F32)
                o_parts.append(oT / den)
            for hi in range(B_GROUP):
                cols = slice(hi * Q_BLOCK, (hi + 1) * Q_BLOCK)
                o_hi = jnp.concatenate([o_parts[0][:, cols], o_parts[1][:, cols]], axis=0).T
                og_ref[lanes, cols] = (o_hi * sg_ref[lanes, cols]).astype(BF16)

        y = jnp.dot(og_ref[...], wo_ref[...], preferred_element_type=F32)
        xo_ref[...] = _deepnorm_ln(x, y, g_ref[...], b_ref[...])

    @pl.when(step % 2 == 0)
    def _even():
        body(qT0_ref, sg0_ref, qT1_ref, sg1_ref)

    @pl.when(step % 2 == 1)
    def _odd():
        body(qT1_ref, sg1_ref, qT0_ref, sg0_ref)


def _swa_layer(x, wqT, wg, wo, ks, vsT, bias, sink, g, b, tm):
    B, S, _ = x.shape
    nt = S // tm
    nblk = tm // Q_BLOCK
    once = pl.Buffered(1)
    const = lambda *shape: pl.BlockSpec(shape, lambda b_, i: (0,) * len(shape), pipeline_mode=once)
    done = pl.BlockSpec((None, tm, D_MODEL), lambda b_, i: (b_, jnp.maximum(i - 1, 0), 0))
    return pl.pallas_call(
        functools.partial(_swa_layer_kernel, nblk=nblk),
        grid=(B, nt + 1),
        in_specs=[
            pl.BlockSpec((None, tm, D_MODEL), lambda b_, i: (b_, jnp.minimum(i, nt - 1), 0)),
            done, const(*wqT.shape), const(*wg.shape), const(*wo.shape),
            pl.BlockSpec((None, S, B_KV_WIDTH), lambda b_, i: (b_, 0, 0), pipeline_mode=once),
            pl.BlockSpec((None, S // Q_BLOCK, B_KV_WIDTH, Q_BLOCK), lambda b_, i: (b_, 0, 0, 0),
                         pipeline_mode=once),
            const(*bias.shape), const(*sink.shape), const(*g.shape), const(*b.shape),
        ],
        out_specs=done,
        out_shape=jax.ShapeDtypeStruct(x.shape, F32),
        scratch_shapes=[
            pltpu.VMEM((B_WIDTH, tm), BF16), pltpu.VMEM((tm, B_WIDTH), F32),
            pltpu.VMEM((B_WIDTH, tm), BF16), pltpu.VMEM((tm, B_WIDTH), F32),
            pltpu.VMEM((tm, B_WIDTH), BF16),
        ],
        compiler_params=_cparams(2),
        name="swa_layer",
    )(x, x, wqT, wg, wo, ks, vsT, bias, sink, g, b)


def _rope_tables(S):
    inv = ROPE_THETA ** (-jnp.arange(ROPE_HALF, dtype=F32) / ROPE_HALF)
    ang = jnp.arange(S, dtype=F32)[:, None] * inv[None, :]
    cos, sin = jnp.cos(ang), jnp.sin(ang)
    zeros = jnp.zeros((S, LANES - A_ROPE), F32)
    z32 = jnp.zeros((S, ROPE_HALF), F32)
    cosk = jnp.concatenate([cos, cos, zeros], axis=1)
    sinka = jnp.concatenate([-sin, z32, zeros], axis=1)
    sinkb = jnp.concatenate([z32, sin, zeros], axis=1)
    return cos.T, sin.T, cosk, sinka, sinkb


def _mla_weights(w_in, w_uq, w_ukv):
    o_kr = A_Q_LORA + A_KV_LORA
    w_in_pad = jnp.concatenate(
        [w_in[:, :o_kr + A_ROPE], jnp.zeros((D_MODEL, LANES - A_ROPE), w_in.dtype),
         w_in[:, o_kr + A_ROPE:]], axis=1).astype(BF16)
    w_uq_h = w_uq.reshape(A_Q_LORA, A_HEADS, A_QK)
    w_uq_pad = jnp.pad(w_uq_h, ((0, 0), (0, 0), (0, A_QPAD - A_QK))).reshape(A_Q_LORA, A_HEADS * A_QPAD)
    w_ukv_h = w_ukv.reshape(A_KV_LORA, A_HEADS, A_NOPE + A_VDIM)
    w_kn = w_ukv_h[:, :, :A_NOPE].reshape(A_KV_LORA, A_HEADS * A_NOPE)
    w_v = w_ukv_h[:, :, A_NOPE:].reshape(A_KV_LORA, A_WIDTH)
    return w_in_pad, w_uq_pad.T.astype(BF16), w_kn.astype(BF16), w_v.T.astype(BF16)


def _swa_feature_perm():
    hi, g, d = np.meshgrid(np.arange(B_GROUP), np.arange(B_KV_HEADS), np.arange(B_HEAD_DIM),
                           indexing="ij")
    return ((g * B_GROUP + hi) * B_HEAD_DIM + d).reshape(-1)


def kernel(x, w_in_a, q_norm_a, kv_norm_a, w_uq_a, w_ukv_a, w_o_a, w_kv_shared, w_in_b, sinks_b,
           w_o_b, rel_bias_table, ln_gain, ln_bias):
    B, S, _ = x.shape
    tm = min(512, S)
    tq = min(512, S)
    rope = _rope_tables(S)

    ks = vsT = None
    for i in range(N_A_LAYERS):
        w_in, w_uqT, w_kn, w_vT = _mla_weights(w_in_a[i], w_uq_a[i], w_ukv_a[i])
        qT, kn, kr, vT, sg = _mla_proj(
            x, w_in, q_norm_a[i][None, :], kv_norm_a[i][None, :], w_uqT, w_kn, w_vT, rope, tm, tq, tq)
        o = _mla_attn(qT, kn, kr, vT)
        g, b = ln_gain[i][None, :], ln_bias[i][None, :]
        wo = w_o_a[i].astype(BF16)
        if i < N_A_LAYERS - 1:
            x = _out_ln(o, sg, x, wo, g, b, tm)
        else:
            x, ks, vsT = _out_ln(o, sg, x, wo, g, b, tm,
                                 kv_weights=(w_kv_shared[:, :B_KV_WIDTH].astype(BF16),
                                             w_kv_shared[:, B_KV_WIDTH:].T.astype(BF16)))

    bias = _band_bias(rel_bias_table)
    perm = _swa_feature_perm()
    for j in range(DEPTH - N_A_LAYERS):
        layer = N_A_LAYERS + j
        wqT = w_in_b[j][:, :B_WIDTH].T.astype(BF16)
        wg = w_in_b[j][:, B_WIDTH:][:, perm].astype(BF16)
        wo = w_o_b[j][perm, :].astype(BF16)
        sink = jnp.repeat(sinks_b[j].astype(F32).reshape(B_KV_HEADS, 1, B_GROUP), Q_BLOCK, axis=2)
        x = _swa_layer(x, wqT, wg, wo, ks, vsT, bias, sink,
                       ln_gain[layer][None, :], ln_bias[layer][None, :], min(512, S))
    return x
```

```python
import functools
import math

import jax
import jax.numpy as jnp
import numpy as np
from jax import lax
from jax.experimental import pallas as pl
from jax.experimental.pallas import tpu as pltpu

F32 = jnp.float32
BF16 = jnp.bfloat16

D_MODEL = 1024
DEPTH = 4
CHUNK = 64
Q_BLOCK = 128
N_A_LAYERS = DEPTH // 2
A_HEADS = 8
A_NOPE = 128
A_ROPE = 64
A_VDIM = 128
A_QK = A_NOPE + A_ROPE
A_Q_LORA = 384
A_KV_LORA = 256
A_WIDTH = A_HEADS * A_VDIM
A_QPAD = 256
A_VROWS = A_VDIM + 16
ROPE_THETA = 10000.0
ROPE_HALF = A_ROPE // 2
B_Q_HEADS = 16
B_KV_HEADS = 2
B_GROUP = B_Q_HEADS // B_KV_HEADS
B_HEAD_DIM = 64
B_WIDTH = B_Q_HEADS * B_HEAD_DIM
B_KV_WIDTH = B_KV_HEADS * B_HEAD_DIM
WINDOW = 128
WIN_CHUNKS = WINDOW // CHUNK
NUM_BUCKETS = 32
MAX_DISTANCE = 128
DEEPNORM_ALPHA = (2 * DEPTH) ** 0.25
NORM_EPS = 1e-5
MASK_VALUE = -1e30

LANES = 128
VMEM_LIMIT_BYTES = 56 * 1024 * 1024

_NT = (((1,), (1,)), ((), ()))
UNROLL = 8
BOUND_MARGIN = 1.0 + 2.0 ** -6
DENOM_FLOOR = 2.0 ** -80


def _cparams(n_axes, flags=None):
    return pltpu.CompilerParams(
        dimension_semantics=("arbitrary",) * n_axes, vmem_limit_bytes=VMEM_LIMIT_BYTES, flags=flags)


def _sigmoid(x):
    return 1.0 / (1.0 + jnp.exp(-x))


def _rms_scale(c, g):
    return c * lax.rsqrt(jnp.mean(c * c, axis=-1, keepdims=True) + NORM_EPS) * g


def _deepnorm_ln(x, y, g, b):
    z = DEEPNORM_ALPHA * x + y
    mu = jnp.mean(z, axis=-1, keepdims=True)
    zc = z - mu
    var = jnp.mean(zc * zc, axis=-1, keepdims=True)
    return zc * lax.rsqrt(var + NORM_EPS) * g + b


def _mla_proj_kernel(x_ref, w_in_ref, gq_ref, gkv_ref, w_uqT_ref, w_kn_ref, w_vT_ref,
                     cosT_ref, sinT_ref, cosk_ref, sinka_ref, sinkb_ref, ktail_ref, headsel_ref,
                     qT_ref, kn_ref, kr_ref, vT_ref, sg_ref, kmax_ref):
    @pl.when(pl.program_id(1) == 0)
    def _start_of_sequence():
        kmax_ref[...] = jnp.zeros(kmax_ref.shape, F32)

    xb = x_ref[...].astype(BF16)
    h = jnp.dot(xb, w_in_ref[...], preferred_element_type=F32)
    o_kv = A_Q_LORA
    o_kr = A_Q_LORA + A_KV_LORA
    o_g = o_kr + LANES
    cq = _rms_scale(h[:, :o_kv], gq_ref[...]).astype(BF16)
    ckv = _rms_scale(h[:, o_kv:o_kr], gkv_ref[...]).astype(BF16)

    scale = A_QK ** -0.5 * math.log2(math.e)
    qT = lax.dot_general(w_uqT_ref[...], cq, _NT, preferred_element_type=F32) * scale

    krs = h[:, o_kr:o_g]
    kr = (krs * cosk_ref[...]
          + pltpu.roll(krs, LANES - ROPE_HALF, axis=1) * sinka_ref[...]
          + pltpu.roll(krs, ROPE_HALF, axis=1) * sinkb_ref[...])
    kr_ref[...] = (kr + ktail_ref[...]).astype(BF16)
    kn = jnp.dot(ckv, w_kn_ref[...], preferred_element_type=F32)

    kn2 = jnp.dot((kn * kn).astype(BF16), headsel_ref[...], preferred_element_type=F32)
    k2 = kn2 + jnp.sum(kr * kr, axis=1, keepdims=True)
    kmax2 = jnp.maximum(kmax_ref[...], jnp.max(k2, axis=0, keepdims=True))
    kmax_ref[...] = kmax2

    cos = cosT_ref[...]
    sin = sinT_ref[...]
    n_r1 = A_NOPE
    n_r2 = n_r1 + ROPE_HALF
    n_r3 = n_r2 + ROPE_HALF
    tail_row = lax.broadcasted_iota(jnp.int32, (A_QPAD - A_QK, qT.shape[1]), 0)
    bound_row = vT_ref.shape[-1] // CHUNK
    for hh in range(A_HEADS):
        r0 = hh * A_QPAD
        t1 = qT[r0 + n_r1:r0 + n_r2]
        t2 = qT[r0 + n_r2:r0 + n_r3]
        qT_ref[hh, :n_r1, :] = qT[r0:r0 + n_r1].astype(BF16)
        qT_ref[hh, n_r1:n_r2, :] = (t1 * cos - t2 * sin).astype(BF16)
        qT_ref[hh, n_r2:n_r3, :] = (t2 * cos + t1 * sin).astype(BF16)
        qh = qT[r0:r0 + A_QK]
        q2 = jnp.sum(qh * qh, axis=0, keepdims=True)
        bound = jnp.sqrt(q2 * kmax2[:, hh:hh + 1]) * BOUND_MARGIN
        qT_ref[hh, n_r3:, :] = jnp.where(tail_row == bound_row, -bound, 0.0).astype(BF16)

    kn = kn.astype(BF16)
    vT = lax.dot_general(w_vT_ref[...], ckv, _NT, preferred_element_type=F32).astype(BF16)
    tk = vT_ref.shape[-1]
    extra = (lax.broadcasted_iota(jnp.int32, (A_VROWS - A_VDIM, tk), 0) == 0).astype(BF16)
    for hh in range(A_HEADS):
        kn_ref[hh] = kn[:, hh * A_NOPE:(hh + 1) * A_NOPE]
        for n in range(vT_ref.shape[1]):
            vT_ref[hh, n, :A_VDIM, :] = vT[hh * A_VDIM:(hh + 1) * A_VDIM, n * tk:(n + 1) * tk]
            vT_ref[hh, n, A_VDIM:, :] = extra

    gate = h[:, o_g:]
    sg_ref[...] = (gate * _sigmoid(gate)).astype(BF16)


def _mla_proj(x, w_in, gq, gkv, w_uqT, w_kn, w_vT, rope, tm, tq, tk):
    B, S, _ = x.shape
    nt = S // tm
    cosT, sinT, cosk, sinka, sinkb = rope
    key = np.arange(S)[:, None]
    lane = np.arange(LANES)[None, :] - A_ROPE
    ktail = jnp.asarray((lane == (key % tk) // CHUNK) | (lane == tk // CHUNK), F32)
    headsel = jnp.asarray(
        np.arange(A_HEADS * A_NOPE)[:, None] // A_NOPE == np.arange(LANES)[None, :], BF16)
    const = lambda *shape: pl.BlockSpec(shape, lambda b, i: (0,) * len(shape))
    tok_lanes = pl.BlockSpec((tm, LANES), lambda b, i: (i, 0))
    return pl.pallas_call(
        _mla_proj_kernel,
        grid=(B, nt),
        in_specs=[
            pl.BlockSpec((None, tm, D_MODEL), lambda b, i: (b, i, 0)),
            const(*w_in.shape), const(*gq.shape), const(*gkv.shape),
            const(*w_uqT.shape), const(*w_kn.shape), const(*w_vT.shape),
            pl.BlockSpec((ROPE_HALF, tm), lambda b, i: (0, i)),
            pl.BlockSpec((ROPE_HALF, tm), lambda b, i: (0, i)),
            tok_lanes, tok_lanes, tok_lanes, tok_lanes, const(*headsel.shape),
        ],
        out_specs=[
            pl.BlockSpec((None, A_HEADS, None, A_QPAD, tm),
                         lambda b, i: (b, 0, i // (tq // tm), 0, i % (tq // tm))),
            pl.BlockSpec((None, A_HEADS, tm, A_NOPE), lambda b, i: (b, 0, i, 0)),
            pl.BlockSpec((None, tm, LANES), lambda b, i: (b, i, 0)),
            pl.BlockSpec((None, A_HEADS, tm // tk, A_VROWS, tk), lambda b, i: (b, 0, i, 0, 0)),
            pl.BlockSpec((None, tm, A_WIDTH), lambda b, i: (b, i, 0)),
        ],
        out_shape=[
            jax.ShapeDtypeStruct((B, A_HEADS, S // tq, A_QPAD, tq), BF16),
            jax.ShapeDtypeStruct((B, A_HEADS, S, A_NOPE), BF16),
            jax.ShapeDtypeStruct((B, S, LANES), BF16),
            jax.ShapeDtypeStruct((B, A_HEADS, S // tk, A_VROWS, tk), BF16),
            jax.ShapeDtypeStruct((B, S, A_WIDTH), BF16),
        ],
        scratch_shapes=[pltpu.VMEM((1, LANES), F32)],
        compiler_params=_cparams(2),
        name="mla_proj",
    )(x, w_in, gq, gkv, w_uqT, w_kn, w_vT, cosT, sinT, cosk, sinka, sinkb, ktail, headsel)


def _mla_attn_kernel(qT_ref, kn_ref, kr_ref, vT_ref, mrow_ref, o_ref, *scratch, nq, tq):
    *p_refs, acc_ref = scratch
    n_buf = len(p_refs)

    def shifted_scores(qi, t):
        k0 = pl.multiple_of(t * tq, tq)
        kcat = jnp.concatenate([kn_ref[pl.ds(k0, tq), :], kr_ref[pl.ds(k0, tq), :]], axis=1)
        tail = qT_ref[qi, A_QK:, :] + mrow_ref[jnp.where(t == qi, 1, 0)]
        q_full = jnp.concatenate([qT_ref[qi, :A_QK, :], tail], axis=0)
        return jnp.dot(kcat, q_full, preferred_element_type=F32)

    def weighted_values(qi, t, buf):
        acc_ref[qi] += jnp.dot(vT_ref[t], p_refs[buf][...], preferred_element_type=F32)

    acc_ref[...] = jnp.zeros(acc_ref.shape, F32)
    for p_ref in p_refs[-2:]:
        p_ref[...] = jnp.zeros(p_ref.shape, BF16)

    def step(carry, buf):
        qi, t, qi_1, t_1, qi_2, t_2 = carry
        weighted_values(qi_2, t_2, (buf - 2) % n_buf)
        p_refs[buf][...] = jnp.exp2(shifted_scores(qi, t)).astype(BF16)
        last = t == qi
        return jnp.where(last, qi + 1, qi), jnp.where(last, 0, t + 1), qi, t, qi_1, t_1

    carry = (jnp.int32(0),) * 6
    n_steps = nq * (nq + 1) // 2
    n_lead = n_steps % UNROLL
    for u in range(n_lead):
        carry = step(carry, u % n_buf)

    def steps(it, carry):
        for u in range(UNROLL):
            carry = step(carry, (n_lead + u) % n_buf)
        return carry

    _, _, qi_1, t_1, qi_2, t_2 = lax.fori_loop(0, n_steps // UNROLL, steps, carry)
    weighted_values(qi_2, t_2, (n_steps - 2) % n_buf)
    weighted_values(qi_1, t_1, (n_steps - 1) % n_buf)

    def write_out(qi, acc):
        r0 = pl.multiple_of(qi * tq, tq)
        o_ref[pl.ds(r0, tq), :] = (acc[:A_VDIM] / acc[A_VDIM:A_VDIM + 1]).T.astype(o_ref.dtype)

    def finish(qi, denom_min):
        acc = acc_ref[qi]
        write_out(qi, acc)
        return jnp.minimum(denom_min, acc[A_VDIM:A_VDIM + 1])

    denom_min = lax.fori_loop(0, nq, finish, jnp.full((1, tq), jnp.inf, F32), unroll=4)

    @pl.when(jnp.logical_not(jnp.min(denom_min) >= DENOM_FLOOR))
    def _redo_with_running_max():
        def query_tile(qi, carry):
            def key_tile(t, mc):
                m, acc = mc
                s = shifted_scores(qi, t)
                m_new = jnp.maximum(m, jnp.max(s, axis=0, keepdims=True))
                p = jnp.exp2(s - m_new).astype(BF16)
                acc = acc * jnp.exp2(m - m_new) + jnp.dot(vT_ref[t], p, preferred_element_type=F32)
                return m_new, acc

            init = (jnp.full((1, tq), MASK_VALUE, F32), jnp.zeros((A_VROWS, tq), F32))
            _, acc = lax.fori_loop(0, qi + 1, key_tile, init)
            write_out(qi, acc)
            return carry

        lax.fori_loop(0, nq, query_tile, 0)


def _mla_attn(qT, kn, kr, vT):
    B, H, nq, _, tq = qT.shape
    S = kn.shape[2]
    assert vT.shape[2:] == (nq, A_VROWS, tq) and tq // CHUNK < A_QPAD - A_QK
    a = np.arange(A_QPAD - A_QK)[:, None]
    qchunk = (np.arange(tq) // CHUNK)[None, :]
    diagonal = np.where((qchunk < a) & (a < tq // CHUNK), MASK_VALUE, 0.0).astype(np.float32)
    mrow = jnp.asarray(np.stack([np.zeros_like(diagonal), diagonal]), BF16)
    return pl.pallas_call(
        functools.partial(_mla_attn_kernel, nq=nq, tq=tq),
        grid=(B, H),
        in_specs=[
            pl.BlockSpec((None, None, nq, A_QPAD, tq), lambda b, h: (b, h, 0, 0, 0)),
            pl.BlockSpec((None, None, S, A_NOPE), lambda b, h: (b, h, 0, 0)),
            pl.BlockSpec((None, S, LANES), lambda b, h: (b, 0, 0)),
            pl.BlockSpec((None, None, nq, A_VROWS, tq), lambda b, h: (b, h, 0, 0, 0)),
            pl.BlockSpec(mrow.shape, lambda b, h: (0, 0, 0)),
        ],
        out_specs=pl.BlockSpec((None, None, S, A_VDIM), lambda b, h: (b, h, 0, 0)),
        out_shape=jax.ShapeDtypeStruct((B, H, S, A_VDIM), BF16),
        scratch_shapes=[
            *([pltpu.VMEM((tq, tq), BF16)] * UNROLL),
            pltpu.VMEM((nq, A_VROWS, tq), F32),
        ],
        compiler_params=_cparams(2),
        name="mla_attn",
    )(qT, kn, kr, vT, mrow)


def _gated(o_ref, sg_ref):
    o = jnp.concatenate([o_ref[hh] for hh in range(o_ref.shape[0])], axis=1)
    return o * sg_ref[...]


def _out_ln_kernel(o_ref, sg_ref, x_ref, wo_ref, g_ref, b_ref, xo_ref):
    y = jnp.dot(_gated(o_ref, sg_ref), wo_ref[...], preferred_element_type=F32)
    xo_ref[...] = _deepnorm_ln(x_ref[...], y, g_ref[...], b_ref[...])


def _out_ln_kv_kernel(o_ref, sg_ref, x_ref, wo_ref, g_ref, b_ref, wk_ref, wvT_ref,
                      xo_ref, ks_ref, vsT_ref):
    y = jnp.dot(_gated(o_ref, sg_ref), wo_ref[...], preferred_element_type=F32)
    xn = _deepnorm_ln(x_ref[...], y, g_ref[...], b_ref[...])
    xo_ref[...] = xn
    xb = xn.astype(BF16)
    ks_ref[...] = jnp.dot(xb, wk_ref[...], preferred_element_type=F32).astype(BF16)
    vT = lax.dot_general(wvT_ref[...], xb, _NT, preferred_element_type=F32).astype(BF16)
    for n in range(vsT_ref.shape[0]):
        vsT_ref[n] = vT[:, n * Q_BLOCK:(n + 1) * Q_BLOCK]


def _out_ln(o, sg, x, wo, g, b, tm, kv_weights=None):
    B, S, _ = x.shape
    row = pl.BlockSpec((None, tm, D_MODEL), lambda b_, i: (b_, i, 0))
    const = lambda *shape: pl.BlockSpec(shape, lambda b_, i: (0,) * len(shape))
    heads = pl.BlockSpec((None, A_HEADS, tm, A_VDIM), lambda b_, i: (b_, 0, i, 0))
    in_specs = [heads, row, row, const(*wo.shape), const(*g.shape), const(*b.shape)]
    if kv_weights is None:
        return pl.pallas_call(
            _out_ln_kernel, grid=(B, S // tm), in_specs=in_specs, out_specs=row,
            out_shape=jax.ShapeDtypeStruct(x.shape, F32),
            compiler_params=_cparams(2), name="out_ln",
        )(o, sg, x, wo, g, b)
    wk, wvT = kv_weights
    nb = tm // Q_BLOCK
    return pl.pallas_call(
        _out_ln_kv_kernel, grid=(B, S // tm),
        in_specs=in_specs + [const(*wk.shape), const(*wvT.shape)],
        out_specs=[
            row,
            pl.BlockSpec((None, tm, B_KV_WIDTH), lambda b_, i: (b_, i, 0)),
            pl.BlockSpec((None, nb, B_KV_WIDTH, Q_BLOCK), lambda b_, i: (b_, i, 0, 0)),
        ],
        out_shape=[
            jax.ShapeDtypeStruct(x.shape, F32),
            jax.ShapeDtypeStruct((B, S, B_KV_WIDTH), BF16),
            jax.ShapeDtypeStruct((B, S // Q_BLOCK, B_KV_WIDTH, Q_BLOCK), BF16),
        ],
        compiler_params=_cparams(2), name="out_ln_kv",
    )(o, sg, x, wo, g, b, wk, wvT)


def _band_bias_kernel(tab_ref, bkt_ref, ok_ref, out_ref):
    bkt = bkt_ref[...]
    ok = ok_ref[...] != 0
    has_prev = lax.broadcasted_iota(jnp.int32, bkt.shape, 0) >= Q_BLOCK
    for h in range(B_Q_HEADS):
        acc = jnp.zeros(bkt.shape, F32)
        for bucket in range(NUM_BUCKETS):
            acc = jnp.where(bkt == bucket, tab_ref[bucket, h], acc)
        generic = jnp.where(ok, acc, MASK_VALUE)
        g, hi = divmod(h, B_GROUP)
        out_ref[0, g, :, hi * Q_BLOCK:(hi + 1) * Q_BLOCK] = jnp.where(has_prev, generic, MASK_VALUE)
        out_ref[1, g, :, hi * Q_BLOCK:(hi + 1) * Q_BLOCK] = generic


def _t5_bucket_map(rel):
    half = NUM_BUCKETS // 2
    ret = jnp.where(rel > 0, half, 0)
    n = jnp.abs(rel)
    max_exact = half // 2
    large = max_exact + (jnp.log(jnp.maximum(n, 1).astype(F32) / max_exact)
                         / math.log(MAX_DISTANCE / max_exact) * (half - max_exact)).astype(jnp.int32)
    large = jnp.minimum(large, half - 1)
    return ret + jnp.where(n < max_exact, n, large)


def _band_bias(rel_bias_table):
    j = jnp.arange(2 * Q_BLOCK)[:, None]
    r = jnp.arange(Q_BLOCK)[None, :]
    rel = j - Q_BLOCK - r
    d = (r // CHUNK) - ((j - Q_BLOCK) // CHUNK)
    ok = ((d >= 0) & (d <= WIN_CHUNKS)).astype(jnp.int32)
    bkt = _t5_bucket_map(rel).astype(jnp.int32)
    return pl.pallas_call(
        _band_bias_kernel,
        in_specs=[
            pl.BlockSpec(memory_space=pltpu.SMEM),
            pl.BlockSpec(memory_space=pltpu.VMEM),
            pl.BlockSpec(memory_space=pltpu.VMEM),
        ],
        out_specs=pl.BlockSpec(memory_space=pltpu.VMEM),
        out_shape=jax.ShapeDtypeStruct((2, B_KV_HEADS, 2 * Q_BLOCK, B_GROUP * Q_BLOCK), F32),
        name="band_bias",
    )(rel_bias_table, bkt, ok)


def _swa_layer_kernel(xn_ref, x_ref, wqT_ref, wg_ref, wo_ref, ks_ref, vsT_ref, bias_ref, sink_ref,
                      g_ref, b_ref, xo_ref, qT0_ref, sg0_ref, qT1_ref, sg1_ref, og_ref, *, nblk):
    step = pl.program_id(1)

    @pl.when((pl.program_id(0) == 0) & (step == 0))
    def _first_step():
        qT0_ref[...] = jnp.zeros(qT0_ref.shape, qT0_ref.dtype)
        sg0_ref[...] = jnp.zeros(sg0_ref.shape, sg0_ref.dtype)

    def body(qT_ref, sg_ref, qT_next_ref, sg_next_ref):
        tile = jnp.maximum(step - 1, 0)
        xb_next = xn_ref[...].astype(BF16)
        zeros = jnp.zeros((B_HEAD_DIM, B_GROUP * Q_BLOCK), BF16)
        chunk = B_WIDTH // nblk
        half = max(nblk // 2, 1)

        for n in range(nblk):
            blk = tile * nblk + n
            cur = pl.multiple_of(blk * Q_BLOCK, Q_BLOCK)
            prev_blk = jnp.maximum(blk - 1, 0)
            prev = pl.multiple_of(prev_blk * Q_BLOCK, Q_BLOCK)
            kband = jnp.concatenate(
                [ks_ref[pl.ds(prev, Q_BLOCK), :], ks_ref[pl.ds(cur, Q_BLOCK), :]], axis=0)
            vband = jnp.concatenate([vsT_ref[prev_blk], vsT_ref[blk]], axis=1)
            table = jnp.minimum(blk, 1)
            lanes = slice(n * Q_BLOCK, (n + 1) * Q_BLOCK)
            scores = []
            for g in range(B_KV_HEADS):
                qg = jnp.concatenate(
                    [qT_ref[(g * B_GROUP + hi) * B_HEAD_DIM:(g * B_GROUP + hi + 1) * B_HEAD_DIM, lanes]
                     for hi in range(B_GROUP)], axis=1)
                rhs = jnp.concatenate([qg, zeros] if g == 0 else [zeros, qg], axis=0)
                scores.append(jnp.dot(kband, rhs, preferred_element_type=F32) + bias_ref[table, g])

            part = slice(n * chunk, (n + 1) * chunk)
            qT_next_ref[part, :] = (
                lax.dot_general(wqT_ref[part, :], xb_next, _NT, preferred_element_type=F32)
                * (B_HEAD_DIM ** -0.5)).astype(BF16)
            gate = jnp.dot(xb_next, wg_ref[:, part], preferred_element_type=F32)
            sg_next_ref[:, part] = gate * _sigmoid(gate)

            o_parts = []
            for g in range(B_KV_HEADS):
                sT = scores[g]
                sink = sink_ref[g]
                m = jnp.maximum(jnp.max(sT, axis=0, keepdims=True), sink)
                e = jnp.exp(sT - m)
                den = jnp.sum(e, axis=0, keepdims=True) + jnp.exp(sink - m)
                v_rows = slice(g * B_HEAD_DIM, (g + 1) * B_HEAD_DIM)
                oT = jnp.dot(vband[v_rows, :], e.astype(BF16), preferred_element_type=F32)
                o_parts.append(oT / den)
            for hi in range(B_GROUP):
                cols = slice(hi * Q_BLOCK, (hi + 1) * Q_BLOCK)
                o_hi = jnp.concatenate([o_parts[0][:, cols], o_parts[1][:, cols]], axis=0).T
                og_ref[lanes, cols] = (o_hi * sg_ref[lanes, cols]).astype(BF16)

            if (n + 1) % half == 0:
                rows = slice((n + 1 - half) * Q_BLOCK, (n + 1) * Q_BLOCK)
                y = jnp.dot(og_ref[rows, :], wo_ref[...], preferred_element_type=F32)
                xo_ref[rows, :] = _deepnorm_ln(x_ref[rows, :], y, g_ref[...], b_ref[...])

    @pl.when(step % 2 == 0)
    def _even():
        body(qT0_ref, sg0_ref, qT1_ref, sg1_ref)

    @pl.when(step % 2 == 1)
    def _odd():
        body(qT1_ref, sg1_ref, qT0_ref, sg0_ref)


def _swa_layer(x, wqT, wg, wo, ks, vsT, bias, sink, g, b, tm):
    B, S, _ = x.shape
    nt = S // tm
    nblk = tm // Q_BLOCK
    once = pl.Buffered(1)
    const = lambda *shape: pl.BlockSpec(shape, lambda b_, i: (0,) * len(shape), pipeline_mode=once)
    done = pl.BlockSpec((None, tm, D_MODEL), lambda b_, i: (b_, jnp.maximum(i - 1, 0), 0))
    return pl.pallas_call(
        functools.partial(_swa_layer_kernel, nblk=nblk),
        grid=(B, nt + 1),
        in_specs=[
            pl.BlockSpec((None, tm, D_MODEL), lambda b_, i: (b_, jnp.minimum(i, nt - 1), 0)),
            done, const(*wqT.shape), const(*wg.shape), const(*wo.shape),
            pl.BlockSpec((None, S, B_KV_WIDTH), lambda b_, i: (b_, 0, 0), pipeline_mode=once),
            pl.BlockSpec((None, S // Q_BLOCK, B_KV_WIDTH, Q_BLOCK), lambda b_, i: (b_, 0, 0, 0),
                         pipeline_mode=once),
            const(*bias.shape), const(*sink.shape), const(*g.shape), const(*b.shape),
        ],
        out_specs=done,
        out_shape=jax.ShapeDtypeStruct(x.shape, F32),
        scratch_shapes=[
            pltpu.VMEM((B_WIDTH, tm), BF16), pltpu.VMEM((tm, B_WIDTH), F32),
            pltpu.VMEM((B_WIDTH, tm), BF16), pltpu.VMEM((tm, B_WIDTH), F32),
            pltpu.VMEM((tm, B_WIDTH), BF16),
        ],
        compiler_params=_cparams(2),
        name="swa_layer",
    )(x, x, wqT, wg, wo, ks, vsT, bias, sink, g, b)


def _rope_tables(S):
    inv = ROPE_THETA ** (-jnp.arange(ROPE_HALF, dtype=F32) / ROPE_HALF)
    ang = jnp.arange(S, dtype=F32)[:, None] * inv[None, :]
    cos, sin = jnp.cos(ang), jnp.sin(ang)
    zeros = jnp.zeros((S, LANES - A_ROPE), F32)
    z32 = jnp.zeros((S, ROPE_HALF), F32)
    cosk = jnp.concatenate([cos, cos, zeros], axis=1)
    sinka = jnp.concatenate([-sin, z32, zeros], axis=1)
    sinkb = jnp.concatenate([z32, sin, zeros], axis=1)
    return cos.T, sin.T, cosk, sinka, sinkb


def _mla_weights(w_in, w_uq, w_ukv):
    o_kr = A_Q_LORA + A_KV_LORA
    w_in_pad = jnp.concatenate(
        [w_in[:, :o_kr + A_ROPE], jnp.zeros((D_MODEL, LANES - A_ROPE), w_in.dtype),
         w_in[:, o_kr + A_ROPE:]], axis=1).astype(BF16)
    w_uq_h = w_uq.reshape(A_Q_LORA, A_HEADS, A_QK)
    w_uq_pad = jnp.pad(w_uq_h, ((0, 0), (0, 0), (0, A_QPAD - A_QK))).reshape(A_Q_LORA, A_HEADS * A_QPAD)
    w_ukv_h = w_ukv.reshape(A_KV_LORA, A_HEADS, A_NOPE + A_VDIM)
    w_kn = w_ukv_h[:, :, :A_NOPE].reshape(A_KV_LORA, A_HEADS * A_NOPE)
    w_v = w_ukv_h[:, :, A_NOPE:].reshape(A_KV_LORA, A_WIDTH)
    return w_in_pad, w_uq_pad.T.astype(BF16), w_kn.astype(BF16), w_v.T.astype(BF16)


def _swa_feature_perm():
    hi, g, d = np.meshgrid(np.arange(B_GROUP), np.arange(B_KV_HEADS), np.arange(B_HEAD_DIM),
                           indexing="ij")
    return ((g * B_GROUP + hi) * B_HEAD_DIM + d).reshape(-1)


def kernel(x, w_in_a, q_norm_a, kv_norm_a, w_uq_a, w_ukv_a, w_o_a, w_kv_shared, w_in_b, sinks_b,
           w_o_b, rel_bias_table, ln_gain, ln_bias):
    B, S, _ = x.shape
    tm = min(512, S)
    tq = min(512, S)
    rope = _rope_tables(S)

    ks = vsT = None
    for i in range(N_A_LAYERS):
        w_in, w_uqT, w_kn, w_vT = _mla_weights(w_in_a[i], w_uq_a[i], w_ukv_a[i])
        qT, kn, kr, vT, sg = _mla_proj(
            x, w_in, q_norm_a[i][None, :], kv_norm_a[i][None, :], w_uqT, w_kn, w_vT, rope, tm, tq, tq)
        o = _mla_attn(qT, kn, kr, vT)
        g, b = ln_gain[i][None, :], ln_bias[i][None, :]
        wo = w_o_a[i].astype(BF16)
        if i < N_A_LAYERS - 1:
            x = _out_ln(o, sg, x, wo, g, b, tm)
        else:
            x, ks, vsT = _out_ln(o, sg, x, wo, g, b, tm,
                                 kv_weights=(w_kv_shared[:, :B_KV_WIDTH].astype(BF16),
                                             w_kv_shared[:, B_KV_WIDTH:].T.astype(BF16)))

    bias = _band_bias(rel_bias_table)
    perm = _swa_feature_perm()
    for j in range(DEPTH - N_A_LAYERS):
        layer = N_A_LAYERS + j
        wqT = w_in_b[j][:, :B_WIDTH].T.astype(BF16)
        wg = w_in_b[j][:, B_WIDTH:][:, perm].astype(BF16)
        wo = w_o_b[j][perm, :].astype(BF16)
        sink = jnp.repeat(sinks_b[j].astype(F32).reshape(B_KV_HEADS, 1, B_GROUP), Q_BLOCK, axis=2)
        x = _swa_layer(x, wqT, wg, wo, ks, vsT, bias, sink,
                       ln_gain[layer][None, :], ln_bias[layer][None, :], min(512, S))
    return x
```

```python
import functools
import math

import jax
import jax.numpy as jnp
import numpy as np
from jax import lax
from jax.experimental import pallas as pl
from jax.experimental.pallas import tpu as pltpu

F32 = jnp.float32
BF16 = jnp.bfloat16

D_MODEL = 1024
DEPTH = 4
CHUNK = 64
Q_BLOCK = 128
N_A_LAYERS = DEPTH // 2
A_HEADS = 8
A_NOPE = 128
A_ROPE = 64
A_VDIM = 128
A_QK = A_NOPE + A_ROPE
A_Q_LORA = 384
A_KV_LORA = 256
A_WIDTH = A_HEADS * A_VDIM
A_QPAD = 256
ROPE_THETA = 10000.0
ROPE_HALF = A_ROPE // 2
B_Q_HEADS = 16
B_KV_HEADS = 2
B_GROUP = B_Q_HEADS // B_KV_HEADS
B_HEAD_DIM = 64
B_WIDTH = B_Q_HEADS * B_HEAD_DIM
B_KV_WIDTH = B_KV_HEADS * B_HEAD_DIM
WINDOW = 128
WIN_CHUNKS = WINDOW // CHUNK
NUM_BUCKETS = 32
MAX_DISTANCE = 128
DEEPNORM_ALPHA = (2 * DEPTH) ** 0.25
NORM_EPS = 1e-5
MASK_VALUE = -1e30
LOG2_E = math.log2(math.e)

LANES = 128
VMEM_LIMIT_BYTES = 56 * 1024 * 1024

_NT = (((1,), (1,)), ((), ()))
UNROLL = 17
BOUND_MARGIN = 1.0 + 2.0 ** -6
DENOM_FLOOR = 2.0 ** -80


def _cparams(n_axes, flags=None):
    return pltpu.CompilerParams(
        dimension_semantics=("arbitrary",) * n_axes, vmem_limit_bytes=VMEM_LIMIT_BYTES, flags=flags)


def _sigmoid(x):
    return 1.0 / (1.0 + jnp.exp(-x))


def _rms_scale(c, g):
    return c * lax.rsqrt(jnp.mean(c * c, axis=-1, keepdims=True) + NORM_EPS) * g


def _deepnorm_ln(x, y, g, b):
    z = DEEPNORM_ALPHA * x + y
    mu = jnp.mean(z, axis=-1, keepdims=True)
    zc = z - mu
    var = jnp.mean(zc * zc, axis=-1, keepdims=True)
    return zc * lax.rsqrt(var + NORM_EPS) * g + b


def _mla_proj_kernel(x_ref, w_in_ref, gq_ref, gkv_ref, w_uqT_ref, w_kn_ref, w_vT_ref,
                     cosT_ref, sinT_ref, cosk_ref, sinka_ref, sinkb_ref, ktail_ref, headsel_ref,
                     qT_ref, kn_ref, kr_ref, vT_ref, sg_ref, kmax_ref):
    @pl.when(pl.program_id(1) == 0)
    def _start_of_sequence():
        kmax_ref[...] = jnp.zeros(kmax_ref.shape, F32)

    xb = x_ref[...].astype(BF16)
    h = jnp.dot(xb, w_in_ref[...], preferred_element_type=F32)
    o_kv = A_Q_LORA
    o_kr = A_Q_LORA + A_KV_LORA
    o_g = o_kr + LANES
    cq = _rms_scale(h[:, :o_kv], gq_ref[...]).astype(BF16)
    ckv = _rms_scale(h[:, o_kv:o_kr], gkv_ref[...]).astype(BF16)

    scale = A_QK ** -0.5 * math.log2(math.e)
    qT = lax.dot_general(w_uqT_ref[...], cq, _NT, preferred_element_type=F32) * scale

    krs = h[:, o_kr:o_g]
    kr = (krs * cosk_ref[...]
          + pltpu.roll(krs, LANES - ROPE_HALF, axis=1) * sinka_ref[...]
          + pltpu.roll(krs, ROPE_HALF, axis=1) * sinkb_ref[...])
    kr_ref[...] = (kr + ktail_ref[...]).astype(BF16)
    kn = jnp.dot(ckv, w_kn_ref[...], preferred_element_type=F32)

    kn2 = jnp.dot((kn * kn).astype(BF16), headsel_ref[...], preferred_element_type=F32)
    k2 = kn2 + jnp.sum(kr * kr, axis=1, keepdims=True)
    kmax2 = jnp.maximum(kmax_ref[...], jnp.max(k2, axis=0, keepdims=True))
    kmax_ref[...] = kmax2

    cos = cosT_ref[...]
    sin = sinT_ref[...]
    n_r1 = A_NOPE
    n_r2 = n_r1 + ROPE_HALF
    n_r3 = n_r2 + ROPE_HALF
    tail_row = lax.broadcasted_iota(jnp.int32, (A_QPAD - A_QK, qT.shape[1]), 0)
    bound_row = vT_ref.shape[-1] // CHUNK
    for hh in range(A_HEADS):
        r0 = hh * A_QPAD
        t1 = qT[r0 + n_r1:r0 + n_r2]
        t2 = qT[r0 + n_r2:r0 + n_r3]
        qT_ref[hh, :n_r1, :] = qT[r0:r0 + n_r1].astype(BF16)
        qT_ref[hh, n_r1:n_r2, :] = (t1 * cos - t2 * sin).astype(BF16)
        qT_ref[hh, n_r2:n_r3, :] = (t2 * cos + t1 * sin).astype(BF16)
        qh = qT[r0:r0 + A_QK]
        q2 = jnp.sum(qh * qh, axis=0, keepdims=True)
        bound = jnp.sqrt(q2 * kmax2[:, hh:hh + 1]) * BOUND_MARGIN
        qT_ref[hh, n_r3:, :] = jnp.where(tail_row == bound_row, -bound, 0.0).astype(BF16)

    kn = kn.astype(BF16)
    vT = lax.dot_general(w_vT_ref[...], ckv, _NT, preferred_element_type=F32).astype(BF16)
    tk = vT_ref.shape[-1]
    for hh in range(A_HEADS):
        kn_ref[hh] = kn[:, hh * A_NOPE:(hh + 1) * A_NOPE]
        for n in range(vT_ref.shape[1]):
            vT_ref[hh, n] = vT[hh * A_VDIM:(hh + 1) * A_VDIM, n * tk:(n + 1) * tk]

    gate = h[:, o_g:]
    sg_ref[...] = (gate * _sigmoid(gate)).astype(BF16)


def _mla_proj(x, w_in, gq, gkv, w_uqT, w_kn, w_vT, rope, tm, tq, tk):
    B, S, _ = x.shape
    nt = S // tm
    cosT, sinT, cosk, sinka, sinkb = rope
    key = np.arange(S)[:, None]
    lane = np.arange(LANES)[None, :] - A_ROPE
    ktail = jnp.asarray((lane == (key % tk) // CHUNK) | (lane == tk // CHUNK), F32)
    headsel = jnp.asarray(
        np.arange(A_HEADS * A_NOPE)[:, None] // A_NOPE == np.arange(LANES)[None, :], BF16)
    const = lambda *shape: pl.BlockSpec(shape, lambda b, i: (0,) * len(shape))
    tok_lanes = pl.BlockSpec((tm, LANES), lambda b, i: (i, 0))
    return pl.pallas_call(
        _mla_proj_kernel,
        grid=(B, nt),
        in_specs=[
            pl.BlockSpec((None, tm, D_MODEL), lambda b, i: (b, i, 0)),
            const(*w_in.shape), const(*gq.shape), const(*gkv.shape),
            const(*w_uqT.shape), const(*w_kn.shape), const(*w_vT.shape),
            pl.BlockSpec((ROPE_HALF, tm), lambda b, i: (0, i)),
            pl.BlockSpec((ROPE_HALF, tm), lambda b, i: (0, i)),
            tok_lanes, tok_lanes, tok_lanes, tok_lanes, const(*headsel.shape),
        ],
        out_specs=[
            pl.BlockSpec((None, A_HEADS, None, A_QPAD, tm),
                         lambda b, i: (b, 0, i // (tq // tm), 0, i % (tq // tm))),
            pl.BlockSpec((None, A_HEADS, tm, A_NOPE), lambda b, i: (b, 0, i, 0)),
            pl.BlockSpec((None, tm, LANES), lambda b, i: (b, i, 0)),
            pl.BlockSpec((None, A_HEADS, tm // tk, A_VDIM, tk), lambda b, i: (b, 0, i, 0, 0)),
            pl.BlockSpec((None, tm, A_WIDTH), lambda b, i: (b, i, 0)),
        ],
        out_shape=[
            jax.ShapeDtypeStruct((B, A_HEADS, S // tq, A_QPAD, tq), BF16),
            jax.ShapeDtypeStruct((B, A_HEADS, S, A_NOPE), BF16),
            jax.ShapeDtypeStruct((B, S, LANES), BF16),
            jax.ShapeDtypeStruct((B, A_HEADS, S // tk, A_VDIM, tk), BF16),
            jax.ShapeDtypeStruct((B, S, A_WIDTH), BF16),
        ],
        scratch_shapes=[pltpu.VMEM((1, LANES), F32)],
        compiler_params=_cparams(2),
        name="mla_proj",
    )(x, w_in, gq, gkv, w_uqT, w_kn, w_vT, cosT, sinT, cosk, sinka, sinkb, ktail, headsel)


def _mla_attn_kernel(qT_ref, kn_ref, kr_ref, vT_ref, mrow_ref, o_ref, *scratch, nq, tq):
    *p_refs, l_ref, acc_ref = scratch
    n_buf = len(p_refs)

    def shifted_scores(qi, t):
        k0 = pl.multiple_of(t * tq, tq)
        kcat = jnp.concatenate([kn_ref[pl.ds(k0, tq), :], kr_ref[pl.ds(k0, tq), :]], axis=1)
        tail = qT_ref[qi, A_QK:, :] + mrow_ref[jnp.where(t == qi, 1, 0)]
        q_full = jnp.concatenate([qT_ref[qi, :A_QK, :], tail], axis=0)
        return jnp.dot(kcat, q_full, preferred_element_type=F32)

    def weighted_values(qi, t, buf):
        acc_ref[qi] += jnp.dot(vT_ref[t], p_refs[buf][...], preferred_element_type=F32)

    acc_ref[...] = jnp.zeros(acc_ref.shape, F32)
    l_ref[...] = jnp.zeros(l_ref.shape, F32)
    for p_ref in p_refs[-2:]:
        p_ref[...] = jnp.zeros(p_ref.shape, BF16)

    def step(carry, buf):
        qi, t, qi_1, t_1, qi_2, t_2 = carry
        weighted_values(qi_2, t_2, (buf - 2) % n_buf)
        p = jnp.exp2(shifted_scores(qi, t))
        l_ref[qi] += jnp.sum(p, axis=0, keepdims=True)
        p_refs[buf][...] = p.astype(BF16)
        last = t == qi
        return jnp.where(last, qi + 1, qi), jnp.where(last, 0, t + 1), qi, t, qi_1, t_1

    carry = (jnp.int32(0),) * 6
    n_steps = nq * (nq + 1) // 2
    n_lead = n_steps % UNROLL
    for u in range(n_lead):
        carry = step(carry, u % n_buf)

    def steps(it, carry):
        for u in range(UNROLL):
            carry = step(carry, (n_lead + u) % n_buf)
        return carry

    _, _, qi_1, t_1, qi_2, t_2 = lax.fori_loop(0, n_steps // UNROLL, steps, carry)
    weighted_values(qi_2, t_2, (n_steps - 2) % n_buf)
    weighted_values(qi_1, t_1, (n_steps - 1) % n_buf)

    def write_out(qi, acc, l):
        r0 = pl.multiple_of(qi * tq, tq)
        o_ref[pl.ds(r0, tq), :] = (acc / l).T.astype(o_ref.dtype)

    def finish(qi, denom_min):
        l = l_ref[qi]
        write_out(qi, acc_ref[qi], l)
        return jnp.minimum(denom_min, l)

    denom_min = lax.fori_loop(0, nq, finish, jnp.full((1, tq), jnp.inf, F32), unroll=4)

    @pl.when(jnp.logical_not(jnp.min(denom_min) >= DENOM_FLOOR))
    def _redo_with_running_max():
        def query_tile(qi, carry):
            def key_tile(t, mla):
                m, l, acc = mla
                s = shifted_scores(qi, t)
                m_new = jnp.maximum(m, jnp.max(s, axis=0, keepdims=True))
                alpha = jnp.exp2(m - m_new)
                p = jnp.exp2(s - m_new)
                l = l * alpha + jnp.sum(p, axis=0, keepdims=True)
                acc = acc * alpha + jnp.dot(vT_ref[t], p.astype(BF16), preferred_element_type=F32)
                return m_new, l, acc

            init = (jnp.full((1, tq), MASK_VALUE, F32), jnp.zeros((1, tq), F32),
                    jnp.zeros((A_VDIM, tq), F32))
            _, l, acc = lax.fori_loop(0, qi + 1, key_tile, init)
            write_out(qi, acc, l)
            return carry

        lax.fori_loop(0, nq, query_tile, 0)


def _mla_attn(qT, kn, kr, vT):
    B, H, nq, _, tq = qT.shape
    S = kn.shape[2]
    assert vT.shape[2:] == (nq, A_VDIM, tq) and tq // CHUNK < A_QPAD - A_QK
    a = np.arange(A_QPAD - A_QK)[:, None]
    qchunk = (np.arange(tq) // CHUNK)[None, :]
    diagonal = np.where((qchunk < a) & (a < tq // CHUNK), MASK_VALUE, 0.0).astype(np.float32)
    mrow = jnp.asarray(np.stack([np.zeros_like(diagonal), diagonal]), BF16)
    return pl.pallas_call(
        functools.partial(_mla_attn_kernel, nq=nq, tq=tq),
        grid=(B, H),
        in_specs=[
            pl.BlockSpec((None, None, nq, A_QPAD, tq), lambda b, h: (b, h, 0, 0, 0)),
            pl.BlockSpec((None, None, S, A_NOPE), lambda b, h: (b, h, 0, 0)),
            pl.BlockSpec((None, S, LANES), lambda b, h: (b, 0, 0)),
            pl.BlockSpec((None, None, nq, A_VDIM, tq), lambda b, h: (b, h, 0, 0, 0)),
            pl.BlockSpec(mrow.shape, lambda b, h: (0, 0, 0)),
        ],
        out_specs=pl.BlockSpec((None, None, S, A_VDIM), lambda b, h: (b, h, 0, 0)),
        out_shape=jax.ShapeDtypeStruct((B, H, S, A_VDIM), BF16),
        scratch_shapes=[
            *([pltpu.VMEM((tq, tq), BF16)] * UNROLL),
            pltpu.VMEM((nq, 1, tq), F32),
            pltpu.VMEM((nq, A_VDIM, tq), F32),
        ],
        compiler_params=_cparams(2),
        name="mla_attn",
    )(qT, kn, kr, vT, mrow)


def _gated_out_ln(o_ref, sg_ref, x_ref, wo_ref, g_ref, b_ref, xo_ref):
    half = xo_ref.shape[0] // 2
    for rows in (slice(0, half), slice(half, 2 * half)):
        o = jnp.concatenate([o_ref[hh, rows, :] for hh in range(o_ref.shape[0])], axis=1)
        y = jnp.dot(o * sg_ref[rows, :], wo_ref[...], preferred_element_type=F32)
        xo_ref[rows, :] = _deepnorm_ln(x_ref[rows, :], y, g_ref[...], b_ref[...])


def _out_ln_kernel(o_ref, sg_ref, x_ref, wo_ref, g_ref, b_ref, xo_ref):
    _gated_out_ln(o_ref, sg_ref, x_ref, wo_ref, g_ref, b_ref, xo_ref)


def _out_ln_kv_kernel(o_ref, sg_ref, x_ref, wo_ref, g_ref, b_ref, wk_ref, wvT_ref,
                      xo_ref, ks_ref, vsT_ref):
    _gated_out_ln(o_ref, sg_ref, x_ref, wo_ref, g_ref, b_ref, xo_ref)
    xb = xo_ref[...].astype(BF16)
    ks_ref[...] = jnp.dot(xb, wk_ref[...], preferred_element_type=F32).astype(BF16)
    vT = lax.dot_general(wvT_ref[...], xb, _NT, preferred_element_type=F32).astype(BF16)
    for n in range(vsT_ref.shape[0]):
        vsT_ref[n] = vT[:, n * Q_BLOCK:(n + 1) * Q_BLOCK]


def _out_ln(o, sg, x, wo, g, b, tm, kv_weights=None):
    B, S, _ = x.shape
    row = pl.BlockSpec((None, tm, D_MODEL), lambda b_, i: (b_, i, 0))
    const = lambda *shape: pl.BlockSpec(shape, lambda b_, i: (0,) * len(shape))
    heads = pl.BlockSpec((None, A_HEADS, tm, A_VDIM), lambda b_, i: (b_, 0, i, 0))
    in_specs = [heads, row, row, const(*wo.shape), const(*g.shape), const(*b.shape)]
    if kv_weights is None:
        return pl.pallas_call(
            _out_ln_kernel, grid=(B, S // tm), in_specs=in_specs, out_specs=row,
            out_shape=jax.ShapeDtypeStruct(x.shape, F32),
            compiler_params=_cparams(2), name="out_ln",
        )(o, sg, x, wo, g, b)
    wk, wvT = kv_weights
    nb = tm // Q_BLOCK
    return pl.pallas_call(
        _out_ln_kv_kernel, grid=(B, S // tm),
        in_specs=in_specs + [const(*wk.shape), const(*wvT.shape)],
        out_specs=[
            row,
            pl.BlockSpec((None, tm, B_KV_WIDTH), lambda b_, i: (b_, i, 0)),
            pl.BlockSpec((None, nb, B_KV_WIDTH, Q_BLOCK), lambda b_, i: (b_, i, 0, 0)),
        ],
        out_shape=[
            jax.ShapeDtypeStruct(x.shape, F32),
            jax.ShapeDtypeStruct((B, S, B_KV_WIDTH), BF16),
            jax.ShapeDtypeStruct((B, S // Q_BLOCK, B_KV_WIDTH, Q_BLOCK), BF16),
        ],
        compiler_params=_cparams(2), name="out_ln_kv",
    )(o, sg, x, wo, g, b, wk, wvT)


def _band_bias_kernel(tab_ref, bkt_ref, ok_ref, out_ref):
    bkt = bkt_ref[...]
    ok = ok_ref[...] != 0
    has_prev = lax.broadcasted_iota(jnp.int32, bkt.shape, 0) >= Q_BLOCK
    for h in range(B_Q_HEADS):
        acc = jnp.zeros(bkt.shape, F32)
        for bucket in range(NUM_BUCKETS):
            acc = jnp.where(bkt == bucket, tab_ref[bucket, h], acc)
        generic = jnp.where(ok, acc * LOG2_E, MASK_VALUE)
        g, hi = divmod(h, B_GROUP)
        out_ref[0, g, :, hi * Q_BLOCK:(hi + 1) * Q_BLOCK] = jnp.where(has_prev, generic, MASK_VALUE)
        out_ref[1, g, :, hi * Q_BLOCK:(hi + 1) * Q_BLOCK] = generic


def _t5_bucket_map(rel):
    half = NUM_BUCKETS // 2
    ret = jnp.where(rel > 0, half, 0)
    n = jnp.abs(rel)
    max_exact = half // 2
    large = max_exact + (jnp.log(jnp.maximum(n, 1).astype(F32) / max_exact)
                         / math.log(MAX_DISTANCE / max_exact) * (half - max_exact)).astype(jnp.int32)
    large = jnp.minimum(large, half - 1)
    return ret + jnp.where(n < max_exact, n, large)


def _band_bias(rel_bias_table):
    j = jnp.arange(2 * Q_BLOCK)[:, None]
    r = jnp.arange(Q_BLOCK)[None, :]
    rel = j - Q_BLOCK - r
    d = (r // CHUNK) - ((j - Q_BLOCK) // CHUNK)
    ok = ((d >= 0) & (d <= WIN_CHUNKS)).astype(jnp.int32)
    bkt = _t5_bucket_map(rel).astype(jnp.int32)
    return pl.pallas_call(
        _band_bias_kernel,
        in_specs=[
            pl.BlockSpec(memory_space=pltpu.SMEM),
            pl.BlockSpec(memory_space=pltpu.VMEM),
            pl.BlockSpec(memory_space=pltpu.VMEM),
        ],
        out_specs=pl.BlockSpec(memory_space=pltpu.VMEM),
        out_shape=jax.ShapeDtypeStruct((2, B_KV_HEADS, 2 * Q_BLOCK, B_GROUP * Q_BLOCK), F32),
        name="band_bias",
    )(rel_bias_table, bkt, ok)


def _swa_layer_kernel(xn_ref, x_ref, wqT_ref, wg_ref, wo_ref, ks_ref, vsT_ref, bias_ref, sink_ref,
                      g_ref, b_ref, xo_ref, qT0_ref, sg0_ref, qT1_ref, sg1_ref, og_ref, *, nblk):
    step = pl.program_id(1)

    @pl.when((pl.program_id(0) == 0) & (step == 0))
    def _first_step():
        qT0_ref[...] = jnp.zeros(qT0_ref.shape, qT0_ref.dtype)
        sg0_ref[...] = jnp.zeros(sg0_ref.shape, sg0_ref.dtype)

    def body(qT_ref, sg_ref, qT_next_ref, sg_next_ref):
        tile = jnp.maximum(step - 1, 0)
        xb_next = xn_ref[...].astype(BF16)
        zeros = jnp.zeros((B_HEAD_DIM, B_GROUP * Q_BLOCK), BF16)
        chunk = B_WIDTH // nblk
        half = max(nblk // 2, 1)

        for n in range(nblk):
            blk = tile * nblk + n
            cur = pl.multiple_of(blk * Q_BLOCK, Q_BLOCK)
            prev_blk = jnp.maximum(blk - 1, 0)
            prev = pl.multiple_of(prev_blk * Q_BLOCK, Q_BLOCK)
            kband = jnp.concatenate(
                [ks_ref[pl.ds(prev, Q_BLOCK), :], ks_ref[pl.ds(cur, Q_BLOCK), :]], axis=0)
            vband = jnp.concatenate([vsT_ref[prev_blk], vsT_ref[blk]], axis=1)
            table = jnp.minimum(blk, 1)
            lanes = slice(n * Q_BLOCK, (n + 1) * Q_BLOCK)
            scores = []
            for g in range(B_KV_HEADS):
                qg = jnp.concatenate(
                    [qT_ref[(g * B_GROUP + hi) * B_HEAD_DIM:(g * B_GROUP + hi + 1) * B_HEAD_DIM, lanes]
                     for hi in range(B_GROUP)], axis=1)
                rhs = jnp.concatenate([qg, zeros] if g == 0 else [zeros, qg], axis=0)
                scores.append(jnp.dot(kband, rhs, preferred_element_type=F32) + bias_ref[table, g])

            part = slice(n * chunk, (n + 1) * chunk)
            qT_next_ref[part, :] = (
                lax.dot_general(wqT_ref[part, :], xb_next, _NT, preferred_element_type=F32)
                * (B_HEAD_DIM ** -0.5 * LOG2_E)).astype(BF16)
            gate = jnp.dot(xb_next, wg_ref[:, part], preferred_element_type=F32)
            sg_next_ref[:, part] = gate * _sigmoid(gate)

            o_parts = []
            for g in range(B_KV_HEADS):
                sT = scores[g]
                sink = sink_ref[g]
                m = jnp.maximum(jnp.max(sT, axis=0, keepdims=True), sink)
                e = jnp.exp2(sT - m)
                den = jnp.sum(e, axis=0, keepdims=True) + jnp.exp2(sink - m)
                v_rows = slice(g * B_HEAD_DIM, (g + 1) * B_HEAD_DIM)
                oT = jnp.dot(vband[v_rows, :], e.astype(BF16), preferred_element_type=F32)
                o_parts.append(oT / den)
            for hi in range(B_GROUP):
                cols = slice(hi * Q_BLOCK, (hi + 1) * Q_BLOCK)
                o_hi = jnp.concatenate([o_parts[0][:, cols], o_parts[1][:, cols]], axis=0).T
                og_ref[lanes, cols] = (o_hi * sg_ref[lanes, cols]).astype(BF16)

            if (n + 1) % half == 0:
                rows = slice((n + 1 - half) * Q_BLOCK, (n + 1) * Q_BLOCK)
                y = jnp.dot(og_ref[rows, :], wo_ref[...], preferred_element_type=F32)
                xo_ref[rows, :] = _deepnorm_ln(x_ref[rows, :], y, g_ref[...], b_ref[...])

    @pl.when(step % 2 == 0)
    def _even():
        body(qT0_ref, sg0_ref, qT1_ref, sg1_ref)

    @pl.when(step % 2 == 1)
    def _odd():
        body(qT1_ref, sg1_ref, qT0_ref, sg0_ref)


def _swa_layer(x, wqT, wg, wo, ks, vsT, bias, sink, g, b, tm):
    B, S, _ = x.shape
    nt = S // tm
    nblk = tm // Q_BLOCK
    once = pl.Buffered(1)
    const = lambda *shape: pl.BlockSpec(shape, lambda b_, i: (0,) * len(shape), pipeline_mode=once)
    done = pl.BlockSpec((None, tm, D_MODEL), lambda b_, i: (b_, jnp.maximum(i - 1, 0), 0))
    return pl.pallas_call(
        functools.partial(_swa_layer_kernel, nblk=nblk),
        grid=(B, nt + 1),
        in_specs=[
            pl.BlockSpec((None, tm, D_MODEL), lambda b_, i: (b_, jnp.minimum(i, nt - 1), 0)),
            done, const(*wqT.shape), const(*wg.shape), const(*wo.shape),
            pl.BlockSpec((None, S, B_KV_WIDTH), lambda b_, i: (b_, 0, 0), pipeline_mode=once),
            pl.BlockSpec((None, S // Q_BLOCK, B_KV_WIDTH, Q_BLOCK), lambda b_, i: (b_, 0, 0, 0),
                         pipeline_mode=once),
            const(*bias.shape), const(*sink.shape), const(*g.shape), const(*b.shape),
        ],
        out_specs=done,
        out_shape=jax.ShapeDtypeStruct(x.shape, F32),
        scratch_shapes=[
            pltpu.VMEM((B_WIDTH, tm), BF16), pltpu.VMEM((tm, B_WIDTH), F32),
            pltpu.VMEM((B_WIDTH, tm), BF16), pltpu.VMEM((tm, B_WIDTH), F32),
            pltpu.VMEM((tm, B_WIDTH), BF16),
        ],
        compiler_params=_cparams(2),
        name="swa_layer",
    )(x, x, wqT, wg, wo, ks, vsT, bias, sink, g, b)


def _rope_tables(S):
    inv = ROPE_THETA ** (-jnp.arange(ROPE_HALF, dtype=F32) / ROPE_HALF)
    ang = jnp.arange(S, dtype=F32)[:, None] * inv[None, :]
    cos, sin = jnp.cos(ang), jnp.sin(ang)
    zeros = jnp.zeros((S, LANES - A_ROPE), F32)
    z32 = jnp.zeros((S, ROPE_HALF), F32)
    cosk = jnp.concatenate([cos, cos, zeros], axis=1)
    sinka = jnp.concatenate([-sin, z32, zeros], axis=1)
    sinkb = jnp.concatenate([z32, sin, zeros], axis=1)
    return cos.T, sin.T, cosk, sinka, sinkb


def _mla_weights(w_in, w_uq, w_ukv):
    o_kr = A_Q_LORA + A_KV_LORA
    w_in_pad = jnp.concatenate(
        [w_in[:, :o_kr + A_ROPE], jnp.zeros((D_MODEL, LANES - A_ROPE), w_in.dtype),
         w_in[:, o_kr + A_ROPE:]], axis=1).astype(BF16)
    w_uq_h = w_uq.reshape(A_Q_LORA, A_HEADS, A_QK)
    w_uq_pad = jnp.pad(w_uq_h, ((0, 0), (0, 0), (0, A_QPAD - A_QK))).reshape(A_Q_LORA, A_HEADS * A_QPAD)
    w_ukv_h = w_ukv.reshape(A_KV_LORA, A_HEADS, A_NOPE + A_VDIM)
    w_kn = w_ukv_h[:, :, :A_NOPE].reshape(A_KV_LORA, A_HEADS * A_NOPE)
    w_v = w_ukv_h[:, :, A_NOPE:].reshape(A_KV_LORA, A_WIDTH)
    return w_in_pad, w_uq_pad.T.astype(BF16), w_kn.astype(BF16), w_v.T.astype(BF16)


def _swa_feature_perm():
    hi, g, d = np.meshgrid(np.arange(B_GROUP), np.arange(B_KV_HEADS), np.arange(B_HEAD_DIM),
                           indexing="ij")
    return ((g * B_GROUP + hi) * B_HEAD_DIM + d).reshape(-1)


def kernel(x, w_in_a, q_norm_a, kv_norm_a, w_uq_a, w_ukv_a, w_o_a, w_kv_shared, w_in_b, sinks_b,
           w_o_b, rel_bias_table, ln_gain, ln_bias):
    B, S, _ = x.shape
    tm = min(512, S)
    tq = min(512, S)
    rope = _rope_tables(S)

    ks = vsT = None
    for i in range(N_A_LAYERS):
        w_in, w_uqT, w_kn, w_vT = _mla_weights(w_in_a[i], w_uq_a[i], w_ukv_a[i])
        qT, kn, kr, vT, sg = _mla_proj(
            x, w_in, q_norm_a[i][None, :], kv_norm_a[i][None, :], w_uqT, w_kn, w_vT, rope, tm, tq, tq)
        o = _mla_attn(qT, kn, kr, vT)
        g, b = ln_gain[i][None, :], ln_bias[i][None, :]
        wo = w_o_a[i].astype(BF16)
        if i < N_A_LAYERS - 1:
            x = _out_ln(o, sg, x, wo, g, b, tm)
        else:
            x, ks, vsT = _out_ln(o, sg, x, wo, g, b, tm,
                                 kv_weights=(w_kv_shared[:, :B_KV_WIDTH].astype(BF16),
                                             w_kv_shared[:, B_KV_WIDTH:].T.astype(BF16)))

    bias = _band_bias(rel_bias_table)
    perm = _swa_feature_perm()
    for j in range(DEPTH - N_A_LAYERS):
        layer = N_A_LAYERS + j
        wqT = w_in_b[j][:, :B_WIDTH].T.astype(BF16)
        wg = w_in_b[j][:, B_WIDTH:][:, perm].astype(BF16)
        wo = w_o_b[j][perm, :].astype(BF16)
        sink = jnp.repeat(sinks_b[j].astype(F32).reshape(B_KV_HEADS, 1, B_GROUP) * LOG2_E, Q_BLOCK, axis=2)
        x = _swa_layer(x, wqT, wg, wo, ks, vsT, bias, sink,
                       ln_gain[layer][None, :], ln_bias[layer][None, :], min(512, S))
    return x
```

```python
import functools
import math

import jax
import jax.numpy as jnp
import numpy as np
from jax import lax
from jax.experimental import pallas as pl
from jax.experimental.pallas import tpu as pltpu

F32 = jnp.float32
BF16 = jnp.bfloat16

D_MODEL = 1024
DEPTH = 4
CHUNK = 64
Q_BLOCK = 128
N_A_LAYERS = DEPTH // 2
A_HEADS = 8
A_NOPE = 128
A_ROPE = 64
A_VDIM = 128
A_QK = A_NOPE + A_ROPE
A_Q_LORA = 384
A_KV_LORA = 256
A_WIDTH = A_HEADS * A_VDIM
A_QPAD = 256
ROPE_THETA = 10000.0
ROPE_HALF = A_ROPE // 2
B_Q_HEADS = 16
B_KV_HEADS = 2
B_GROUP = B_Q_HEADS // B_KV_HEADS
B_HEAD_DIM = 64
B_WIDTH = B_Q_HEADS * B_HEAD_DIM
B_KV_WIDTH = B_KV_HEADS * B_HEAD_DIM
WINDOW = 128
WIN_CHUNKS = WINDOW // CHUNK
NUM_BUCKETS = 32
MAX_DISTANCE = 128
DEEPNORM_ALPHA = (2 * DEPTH) ** 0.25
NORM_EPS = 1e-5
MASK_VALUE = -1e30
LOG2_E = math.log2(math.e)

LANES = 128
VMEM_LIMIT_BYTES = 56 * 1024 * 1024

_NT = (((1,), (1,)), ((), ()))
UNROLL = 17
BOUND_MARGIN = 1.0 + 2.0 ** -6
DENOM_FLOOR = 2.0 ** -80


def _cparams(n_axes, flags=None):
    return pltpu.CompilerParams(
        dimension_semantics=("arbitrary",) * n_axes, vmem_limit_bytes=VMEM_LIMIT_BYTES, flags=flags)


def _sigmoid(x):
    return 1.0 / (1.0 + jnp.exp(-x))


def _rms_scale(c, g):
    return c * lax.rsqrt(jnp.mean(c * c, axis=-1, keepdims=True) + NORM_EPS) * g


def _deepnorm_ln(x, y, g, b):
    z = DEEPNORM_ALPHA * x + y
    mu = jnp.mean(z, axis=-1, keepdims=True)
    zc = z - mu
    var = jnp.mean(zc * zc, axis=-1, keepdims=True)
    return zc * lax.rsqrt(var + NORM_EPS) * g + b


def _mla_proj_kernel(*refs, after_layer):
    if after_layer:
        o_ref, sg_prev_ref, x_prev_ref, wo_ref, g_ref, b_ref, *refs = refs
        *refs, xo_ref, qT_ref, kn_ref, kr_ref, vT_ref, sg_ref, kmax_ref = refs
    else:
        x_ref, *refs = refs
        *refs, qT_ref, kn_ref, kr_ref, vT_ref, sg_ref, kmax_ref = refs
    (w_in_ref, gq_ref, gkv_ref, w_uqT_ref, w_kn_ref, w_vT_ref,
     cosT_ref, sinT_ref, cosk_ref, sinka_ref, sinkb_ref, ktail_ref, headsel_ref) = refs

    @pl.when(pl.program_id(1) == 0)
    def _start_of_sequence():
        kmax_ref[...] = jnp.zeros(kmax_ref.shape, F32)

    if after_layer:
        _gated_out_ln(o_ref, sg_prev_ref, x_prev_ref, wo_ref, g_ref, b_ref, xo_ref)
        x_ref = xo_ref
    xb = x_ref[...].astype(BF16)
    h = jnp.dot(xb, w_in_ref[...], preferred_element_type=F32)
    o_kv = A_Q_LORA
    o_kr = A_Q_LORA + A_KV_LORA
    o_g = o_kr + LANES
    cq = _rms_scale(h[:, :o_kv], gq_ref[...]).astype(BF16)
    ckv = _rms_scale(h[:, o_kv:o_kr], gkv_ref[...]).astype(BF16)

    scale = A_QK ** -0.5 * math.log2(math.e)
    qT = lax.dot_general(w_uqT_ref[...], cq, _NT, preferred_element_type=F32) * scale

    krs = h[:, o_kr:o_g]
    kr = (krs * cosk_ref[...]
          + pltpu.roll(krs, LANES - ROPE_HALF, axis=1) * sinka_ref[...]
          + pltpu.roll(krs, ROPE_HALF, axis=1) * sinkb_ref[...])
    kr_ref[...] = (kr + ktail_ref[...]).astype(BF16)
    kn = jnp.dot(ckv, w_kn_ref[...], preferred_element_type=F32)

    kn2 = jnp.dot((kn * kn).astype(BF16), headsel_ref[...], preferred_element_type=F32)
    k2 = kn2 + jnp.sum(kr * kr, axis=1, keepdims=True)
    kmax2 = jnp.maximum(kmax_ref[...], jnp.max(k2, axis=0, keepdims=True))
    kmax_ref[...] = kmax2

    cos = cosT_ref[...]
    sin = sinT_ref[...]
    n_r1 = A_NOPE
    n_r2 = n_r1 + ROPE_HALF
    n_r3 = n_r2 + ROPE_HALF
    tail_row = lax.broadcasted_iota(jnp.int32, (A_QPAD - A_QK, qT.shape[1]), 0)
    bound_row = vT_ref.shape[-1] // CHUNK
    for hh in range(A_HEADS):
        r0 = hh * A_QPAD
        t1 = qT[r0 + n_r1:r0 + n_r2]
        t2 = qT[r0 + n_r2:r0 + n_r3]
        qT_ref[hh, :n_r1, :] = qT[r0:r0 + n_r1].astype(BF16)
        qT_ref[hh, n_r1:n_r2, :] = (t1 * cos - t2 * sin).astype(BF16)
        qT_ref[hh, n_r2:n_r3, :] = (t2 * cos + t1 * sin).astype(BF16)
        qh = qT[r0:r0 + A_QK]
        q2 = jnp.sum(qh * qh, axis=0, keepdims=True)
        bound = jnp.sqrt(q2 * kmax2[:, hh:hh + 1]) * BOUND_MARGIN
        qT_ref[hh, n_r3:, :] = jnp.where(tail_row == bound_row, -bound, 0.0).astype(BF16)

    kn = kn.astype(BF16)
    vT = lax.dot_general(w_vT_ref[...], ckv, _NT, preferred_element_type=F32).astype(BF16)
    tk = vT_ref.shape[-1]
    for hh in range(A_HEADS):
        kn_ref[hh] = kn[:, hh * A_NOPE:(hh + 1) * A_NOPE]
        for n in range(vT_ref.shape[1]):
            vT_ref[hh, n] = vT[hh * A_VDIM:(hh + 1) * A_VDIM, n * tk:(n + 1) * tk]

    gate = h[:, o_g:]
    sg_ref[...] = (gate * _sigmoid(gate)).astype(BF16)


def _mla_proj(x, w_in, gq, gkv, w_uqT, w_kn, w_vT, rope, tm, tq, tk, prev=None):
    B, S, _ = x.shape
    nt = S // tm
    cosT, sinT, cosk, sinka, sinkb = rope
    key = np.arange(S)[:, None]
    lane = np.arange(LANES)[None, :] - A_ROPE
    ktail = jnp.asarray((lane == (key % tk) // CHUNK) | (lane == tk // CHUNK), F32)
    headsel = jnp.asarray(
        np.arange(A_HEADS * A_NOPE)[:, None] // A_NOPE == np.arange(LANES)[None, :], BF16)
    once = pl.Buffered(1)
    const = lambda *shape: pl.BlockSpec(shape, lambda b, i: (0,) * len(shape), pipeline_mode=once)
    tok_lanes = pl.BlockSpec((tm, LANES), lambda b, i: (i, 0))
    row = pl.BlockSpec((None, tm, D_MODEL), lambda b, i: (b, i, 0))
    in_specs = [
        const(*w_in.shape), const(*gq.shape), const(*gkv.shape),
        const(*w_uqT.shape), const(*w_kn.shape), const(*w_vT.shape),
        pl.BlockSpec((ROPE_HALF, tm), lambda b, i: (0, i)),
        pl.BlockSpec((ROPE_HALF, tm), lambda b, i: (0, i)),
        tok_lanes, tok_lanes, tok_lanes, tok_lanes, const(*headsel.shape),
    ]
    operands = (w_in, gq, gkv, w_uqT, w_kn, w_vT, cosT, sinT, cosk, sinka, sinkb, ktail, headsel)
    out_specs = [
        pl.BlockSpec((None, A_HEADS, None, A_QPAD, tm),
                     lambda b, i: (b, 0, i // (tq // tm), 0, i % (tq // tm))),
        pl.BlockSpec((None, A_HEADS, tm, A_NOPE), lambda b, i: (b, 0, i, 0)),
        pl.BlockSpec((None, tm, LANES), lambda b, i: (b, i, 0)),
        pl.BlockSpec((None, A_HEADS, tm // tk, A_VDIM, tk), lambda b, i: (b, 0, i, 0, 0)),
        pl.BlockSpec((None, tm, A_WIDTH), lambda b, i: (b, i, 0)),
    ]
    out_shape = [
        jax.ShapeDtypeStruct((B, A_HEADS, S // tq, A_QPAD, tq), BF16),
        jax.ShapeDtypeStruct((B, A_HEADS, S, A_NOPE), BF16),
        jax.ShapeDtypeStruct((B, S, LANES), BF16),
        jax.ShapeDtypeStruct((B, A_HEADS, S // tk, A_VDIM, tk), BF16),
        jax.ShapeDtypeStruct((B, S, A_WIDTH), BF16),
    ]
    if prev is None:
        in_specs = [row] + in_specs
        operands = (x,) + operands
    else:
        o, sg, wo, g, b = prev
        heads = pl.BlockSpec((None, A_HEADS, tm, A_VDIM), lambda b_, i: (b_, 0, i, 0))
        in_specs = [heads, row, row, const(*wo.shape), const(*g.shape), const(*b.shape)] + in_specs
        operands = (o, sg, x, wo, g, b) + operands
        out_specs = [row] + out_specs
        out_shape = [jax.ShapeDtypeStruct(x.shape, F32)] + out_shape
    return pl.pallas_call(
        functools.partial(_mla_proj_kernel, after_layer=prev is not None),
        grid=(B, nt),
        in_specs=in_specs,
        out_specs=out_specs,
        out_shape=out_shape,
        scratch_shapes=[pltpu.VMEM((1, LANES), F32)],
        compiler_params=_cparams(2),
        name="mla_proj" if prev is None else "out_ln_mla_proj",
    )(*operands)


def _mla_attn_kernel(qT_ref, kn_ref, kr_ref, vT_ref, mrow_ref, o_ref, *scratch, nq, tq):
    *p_refs, l_ref, acc_ref = scratch
    n_buf = len(p_refs)

    def shifted_scores(qi, t):
        k0 = pl.multiple_of(t * tq, tq)
        kcat = jnp.concatenate([kn_ref[pl.ds(k0, tq), :], kr_ref[pl.ds(k0, tq), :]], axis=1)
        tail = qT_ref[qi, A_QK:, :] + mrow_ref[jnp.where(t == qi, 1, 0)]
        q_full = jnp.concatenate([qT_ref[qi, :A_QK, :], tail], axis=0)
        return jnp.dot(kcat, q_full, preferred_element_type=F32)

    def weighted_values(qi, t, buf):
        acc_ref[qi] += jnp.dot(vT_ref[t], p_refs[buf][...], preferred_element_type=F32)

    acc_ref[...] = jnp.zeros(acc_ref.shape, F32)
    l_ref[...] = jnp.zeros(l_ref.shape, F32)
    for p_ref in p_refs[-2:]:
        p_ref[...] = jnp.zeros(p_ref.shape, BF16)

    def step(carry, buf):
        qi, t, qi_1, t_1, qi_2, t_2 = carry
        weighted_values(qi_2, t_2, (buf - 2) % n_buf)
        p = jnp.exp2(shifted_scores(qi, t))
        l_ref[qi] += jnp.sum(p, axis=0, keepdims=True)
        p_refs[buf][...] = p.astype(BF16)
        last = t == qi
        return jnp.where(last, qi + 1, qi), jnp.where(last, 0, t + 1), qi, t, qi_1, t_1

    carry = (jnp.int32(0),) * 6
    n_steps = nq * (nq + 1) // 2
    n_lead = n_steps % UNROLL
    for u in range(n_lead):
        carry = step(carry, u % n_buf)

    def steps(it, carry):
        for u in range(UNROLL):
            carry = step(carry, (n_lead + u) % n_buf)
        return carry

    _, _, qi_1, t_1, qi_2, t_2 = lax.fori_loop(0, n_steps // UNROLL, steps, carry)
    weighted_values(qi_2, t_2, (n_steps - 2) % n_buf)
    weighted_values(qi_1, t_1, (n_steps - 1) % n_buf)

    def write_out(qi, acc, l):
        r0 = pl.multiple_of(qi * tq, tq)
        o_ref[pl.ds(r0, tq), :] = (acc / l).T.astype(o_ref.dtype)

    def finish(qi, denom_min):
        l = l_ref[qi]
        write_out(qi, acc_ref[qi], l)
        return jnp.minimum(denom_min, l)

    denom_min = lax.fori_loop(0, nq, finish, jnp.full((1, tq), jnp.inf, F32), unroll=4)

    @pl.when(jnp.logical_not(jnp.min(denom_min) >= DENOM_FLOOR))
    def _redo_with_running_max():
        def query_tile(qi, carry):
            def key_tile(t, mla):
                m, l, acc = mla
                s = shifted_scores(qi, t)
                m_new = jnp.maximum(m, jnp.max(s, axis=0, keepdims=True))
                alpha = jnp.exp2(m - m_new)
                p = jnp.exp2(s - m_new)
                l = l * alpha + jnp.sum(p, axis=0, keepdims=True)
                acc = acc * alpha + jnp.dot(vT_ref[t], p.astype(BF16), preferred_element_type=F32)
                return m_new, l, acc

            init = (jnp.full((1, tq), MASK_VALUE, F32), jnp.zeros((1, tq), F32),
                    jnp.zeros((A_VDIM, tq), F32))
            _, l, acc = lax.fori_loop(0, qi + 1, key_tile, init)
            write_out(qi, acc, l)
            return carry

        lax.fori_loop(0, nq, query_tile, 0)


def _mla_attn(qT, kn, kr, vT):
    B, H, nq, _, tq = qT.shape
    S = kn.shape[2]
    assert vT.shape[2:] == (nq, A_VDIM, tq) and tq // CHUNK < A_QPAD - A_QK
    a = np.arange(A_QPAD - A_QK)[:, None]
    qchunk = (np.arange(tq) // CHUNK)[None, :]
    diagonal = np.where((qchunk < a) & (a < tq // CHUNK), MASK_VALUE, 0.0).astype(np.float32)
    mrow = jnp.asarray(np.stack([np.zeros_like(diagonal), diagonal]), BF16)
    return pl.pallas_call(
        functools.partial(_mla_attn_kernel, nq=nq, tq=tq),
        grid=(B, H),
        in_specs=[
            pl.BlockSpec((None, None, nq, A_QPAD, tq), lambda b, h: (b, h, 0, 0, 0)),
            pl.BlockSpec((None, None, S, A_NOPE), lambda b, h: (b, h, 0, 0)),
            pl.BlockSpec((None, S, LANES), lambda b, h: (b, 0, 0)),
            pl.BlockSpec((None, None, nq, A_VDIM, tq), lambda b, h: (b, h, 0, 0, 0)),
            pl.BlockSpec(mrow.shape, lambda b, h: (0, 0, 0)),
        ],
        out_specs=pl.BlockSpec((None, None, S, A_VDIM), lambda b, h: (b, h, 0, 0)),
        out_shape=jax.ShapeDtypeStruct((B, H, S, A_VDIM), BF16),
        scratch_shapes=[
            *([pltpu.VMEM((tq, tq), BF16)] * UNROLL),
            pltpu.VMEM((nq, 1, tq), F32),
            pltpu.VMEM((nq, A_VDIM, tq), F32),
        ],
        compiler_params=_cparams(2),
        name="mla_attn",
    )(qT, kn, kr, vT, mrow)


def _gated_out_ln(o_ref, sg_ref, x_ref, wo_ref, g_ref, b_ref, xo_ref):
    half = xo_ref.shape[0] // 2
    for rows in (slice(0, half), slice(half, 2 * half)):
        o = jnp.concatenate([o_ref[hh, rows, :] for hh in range(o_ref.shape[0])], axis=1)
        y = jnp.dot(o * sg_ref[rows, :], wo_ref[...], preferred_element_type=F32)
        xo_ref[rows, :] = _deepnorm_ln(x_ref[rows, :], y, g_ref[...], b_ref[...])


def _out_ln_kv_kernel(o_ref, sg_ref, x_ref, wo_ref, g_ref, b_ref, wk_ref, wvT_ref,
                      xo_ref, ks_ref, vsT_ref):
    _gated_out_ln(o_ref, sg_ref, x_ref, wo_ref, g_ref, b_ref, xo_ref)
    xb = xo_ref[...].astype(BF16)
    ks_ref[...] = jnp.dot(xb, wk_ref[...], preferred_element_type=F32).astype(BF16)
    vT = lax.dot_general(wvT_ref[...], xb, _NT, preferred_element_type=F32).astype(BF16)
    for n in range(vsT_ref.shape[0]):
        vsT_ref[n] = vT[:, n * Q_BLOCK:(n + 1) * Q_BLOCK]


def _out_ln_kv(o, sg, x, wo, g, b, wk, wvT, tm):
    B, S, _ = x.shape
    row = pl.BlockSpec((None, tm, D_MODEL), lambda b_, i: (b_, i, 0))
    const = lambda *shape: pl.BlockSpec(shape, lambda b_, i: (0,) * len(shape))
    heads = pl.BlockSpec((None, A_HEADS, tm, A_VDIM), lambda b_, i: (b_, 0, i, 0))
    nb = tm // Q_BLOCK
    return pl.pallas_call(
        _out_ln_kv_kernel, grid=(B, S // tm),
        in_specs=[heads, row, row, const(*wo.shape), const(*g.shape), const(*b.shape),
                  const(*wk.shape), const(*wvT.shape)],
        out_specs=[
            row,
            pl.BlockSpec((None, tm, B_KV_WIDTH), lambda b_, i: (b_, i, 0)),
            pl.BlockSpec((None, nb, B_KV_WIDTH, Q_BLOCK), lambda b_, i: (b_, i, 0, 0)),
        ],
        out_shape=[
            jax.ShapeDtypeStruct(x.shape, F32),
            jax.ShapeDtypeStruct((B, S, B_KV_WIDTH), BF16),
            jax.ShapeDtypeStruct((B, S // Q_BLOCK, B_KV_WIDTH, Q_BLOCK), BF16),
        ],
        compiler_params=_cparams(2), name="out_ln_kv",
    )(o, sg, x, wo, g, b, wk, wvT)


def _band_bias_kernel(tab_ref, bkt_ref, ok_ref, out_ref):
    bkt = bkt_ref[...]
    ok = ok_ref[...] != 0
    has_prev = lax.broadcasted_iota(jnp.int32, bkt.shape, 0) >= Q_BLOCK
    for h in range(B_Q_HEADS):
        acc = jnp.zeros(bkt.shape, F32)
        for bucket in range(NUM_BUCKETS):
            acc = jnp.where(bkt == bucket, tab_ref[bucket, h], acc)
        generic = jnp.where(ok, acc * LOG2_E, MASK_VALUE)
        g, hi = divmod(h, B_GROUP)
        out_ref[0, g, :, hi * Q_BLOCK:(hi + 1) * Q_BLOCK] = jnp.where(has_prev, generic, MASK_VALUE)
        out_ref[1, g, :, hi * Q_BLOCK:(hi + 1) * Q_BLOCK] = generic


def _t5_bucket_map(rel):
    half = NUM_BUCKETS // 2
    ret = jnp.where(rel > 0, half, 0)
    n = jnp.abs(rel)
    max_exact = half // 2
    large = max_exact + (jnp.log(jnp.maximum(n, 1).astype(F32) / max_exact)
                         / math.log(MAX_DISTANCE / max_exact) * (half - max_exact)).astype(jnp.int32)
    large = jnp.minimum(large, half - 1)
    return ret + jnp.where(n < max_exact, n, large)


def _band_bias(rel_bias_table):
    j = jnp.arange(2 * Q_BLOCK)[:, None]
    r = jnp.arange(Q_BLOCK)[None, :]
    rel = j - Q_BLOCK - r
    d = (r // CHUNK) - ((j - Q_BLOCK) // CHUNK)
    ok = ((d >= 0) & (d <= WIN_CHUNKS)).astype(jnp.int32)
    bkt = _t5_bucket_map(rel).astype(jnp.int32)
    return pl.pallas_call(
        _band_bias_kernel,
        in_specs=[
            pl.BlockSpec(memory_space=pltpu.SMEM),
            pl.BlockSpec(memory_space=pltpu.VMEM),
            pl.BlockSpec(memory_space=pltpu.VMEM),
        ],
        out_specs=pl.BlockSpec(memory_space=pltpu.VMEM),
        out_shape=jax.ShapeDtypeStruct((2, B_KV_HEADS, 2 * Q_BLOCK, B_GROUP * Q_BLOCK), F32),
        name="band_bias",
    )(rel_bias_table, bkt, ok)


def _swa_layer_kernel(xn_ref, x_ref, wqT_ref, wg_ref, wo_ref, ks_ref, vsT_ref, bias_ref, sink_ref,
                      g_ref, b_ref, xo_ref, qT0_ref, sg0_ref, qT1_ref, sg1_ref, og_ref, s_ref, *, nblk):
    step = pl.program_id(1)

    @pl.when((pl.program_id(0) == 0) & (step == 0))
    def _first_step():
        qT0_ref[...] = jnp.zeros(qT0_ref.shape, qT0_ref.dtype)
        sg0_ref[...] = jnp.zeros(sg0_ref.shape, sg0_ref.dtype)

    def body(qT_ref, sg_ref, qT_next_ref, sg_next_ref):
        tile = jnp.maximum(step - 1, 0)
        xb_next = xn_ref[...].astype(BF16)
        zeros = jnp.zeros((B_HEAD_DIM, B_GROUP * Q_BLOCK), BF16)
        chunk = B_WIDTH // nblk
        half = max(nblk // 2, 1)

        for n in range(nblk):
            blk = tile * nblk + n
            cur = pl.multiple_of(blk * Q_BLOCK, Q_BLOCK)
            prev_blk = jnp.maximum(blk - 1, 0)
            prev = pl.multiple_of(prev_blk * Q_BLOCK, Q_BLOCK)
            kband = jnp.concatenate(
                [ks_ref[pl.ds(prev, Q_BLOCK), :], ks_ref[pl.ds(cur, Q_BLOCK), :]], axis=0)
            vband = jnp.concatenate([vsT_ref[prev_blk], vsT_ref[blk]], axis=1)
            table = jnp.minimum(blk, 1)
            lanes = slice(n * Q_BLOCK, (n + 1) * Q_BLOCK)
            score_max = []
            for g in range(B_KV_HEADS):
                qg = jnp.concatenate(
                    [qT_ref[(g * B_GROUP + hi) * B_HEAD_DIM:(g * B_GROUP + hi + 1) * B_HEAD_DIM, lanes]
                     for hi in range(B_GROUP)], axis=1)
                rhs = jnp.concatenate([qg, zeros] if g == 0 else [zeros, qg], axis=0)
                sT = jnp.dot(kband, rhs, preferred_element_type=F32) + bias_ref[table, g]
                s_ref[g] = sT
                score_max.append(jnp.max(sT, axis=0, keepdims=True))

            part = slice(n * chunk, (n + 1) * chunk)
            qT_next_ref[part, :] = (
                lax.dot_general(wqT_ref[part, :], xb_next, _NT, preferred_element_type=F32)
                * (B_HEAD_DIM ** -0.5 * LOG2_E)).astype(BF16)
            gate = jnp.dot(xb_next, wg_ref[:, part], preferred_element_type=F32)
            sg_next_ref[:, part] = gate * _sigmoid(gate)

            o_parts = []
            for g in range(B_KV_HEADS):
                sink = sink_ref[g]
                m = jnp.maximum(score_max[g], sink)
                e = jnp.exp2(s_ref[g] - m)
                den = jnp.sum(e, axis=0, keepdims=True) + jnp.exp2(sink - m)
                v_rows = slice(g * B_HEAD_DIM, (g + 1) * B_HEAD_DIM)
                oT = jnp.dot(vband[v_rows, :], e.astype(BF16), preferred_element_type=F32)
                o_parts.append(oT / den)
            for hi in range(B_GROUP):
                cols = slice(hi * Q_BLOCK, (hi + 1) * Q_BLOCK)
                o_hi = jnp.concatenate([o_parts[0][:, cols], o_parts[1][:, cols]], axis=0).T
                og_ref[lanes, cols] = (o_hi * sg_ref[lanes, cols]).astype(BF16)

            if (n + 1) % half == 0:
                rows = slice((n + 1 - half) * Q_BLOCK, (n + 1) * Q_BLOCK)
                y = jnp.dot(og_ref[rows, :], wo_ref[...], preferred_element_type=F32)
                xo_ref[rows, :] = _deepnorm_ln(x_ref[rows, :], y, g_ref[...], b_ref[...])

    @pl.when(step % 2 == 0)
    def _even():
        body(qT0_ref, sg0_ref, qT1_ref, sg1_ref)

    @pl.when(step % 2 == 1)
    def _odd():
        body(qT1_ref, sg1_ref, qT0_ref, sg0_ref)


def _swa_layer(x, wqT, wg, wo, ks, vsT, bias, sink, g, b, tm):
    B, S, _ = x.shape
    nt = S // tm
    nblk = tm // Q_BLOCK
    once = pl.Buffered(1)
    const = lambda *shape: pl.BlockSpec(shape, lambda b_, i: (0,) * len(shape), pipeline_mode=once)
    done = pl.BlockSpec((None, tm, D_MODEL), lambda b_, i: (b_, jnp.maximum(i - 1, 0), 0))
    return pl.pallas_call(
        functools.partial(_swa_layer_kernel, nblk=nblk),
        grid=(B, nt + 1),
        in_specs=[
            pl.BlockSpec((None, tm, D_MODEL), lambda b_, i: (b_, jnp.minimum(i, nt - 1), 0)),
            done, const(*wqT.shape), const(*wg.shape), const(*wo.shape),
            pl.BlockSpec((None, S, B_KV_WIDTH), lambda b_, i: (b_, 0, 0), pipeline_mode=once),
            pl.BlockSpec((None, S // Q_BLOCK, B_KV_WIDTH, Q_BLOCK), lambda b_, i: (b_, 0, 0, 0),
                         pipeline_mode=once),
            const(*bias.shape), const(*sink.shape), const(*g.shape), const(*b.shape),
        ],
        out_specs=done,
        out_shape=jax.ShapeDtypeStruct(x.shape, F32),
        scratch_shapes=[
            pltpu.VMEM((B_WIDTH, tm), BF16), pltpu.VMEM((tm, B_WIDTH), F32),
            pltpu.VMEM((B_WIDTH, tm), BF16), pltpu.VMEM((tm, B_WIDTH), F32),
            pltpu.VMEM((tm, B_WIDTH), BF16),
            pltpu.VMEM((B_KV_HEADS, 2 * Q_BLOCK, B_GROUP * Q_BLOCK), F32),
        ],
        compiler_params=_cparams(2),
        name="swa_layer",
    )(x, x, wqT, wg, wo, ks, vsT, bias, sink, g, b)


def _rope_tables(S):
    inv = ROPE_THETA ** (-jnp.arange(ROPE_HALF, dtype=F32) / ROPE_HALF)
    ang = jnp.arange(S, dtype=F32)[:, None] * inv[None, :]
    cos, sin = jnp.cos(ang), jnp.sin(ang)
    zeros = jnp.zeros((S, LANES - A_ROPE), F32)
    z32 = jnp.zeros((S, ROPE_HALF), F32)
    cosk = jnp.concatenate([cos, cos, zeros], axis=1)
    sinka = jnp.concatenate([-sin, z32, zeros], axis=1)
    sinkb = jnp.concatenate([z32, sin, zeros], axis=1)
    return cos.T, sin.T, cosk, sinka, sinkb


def _mla_weights(w_in, w_uq, w_ukv):
    o_kr = A_Q_LORA + A_KV_LORA
    w_in_pad = jnp.concatenate(
        [w_in[:, :o_kr + A_ROPE], jnp.zeros((D_MODEL, LANES - A_ROPE), w_in.dtype),
         w_in[:, o_kr + A_ROPE:]], axis=1).astype(BF16)
    w_uq_h = w_uq.reshape(A_Q_LORA, A_HEADS, A_QK)
    w_uq_pad = jnp.pad(w_uq_h, ((0, 0), (0, 0), (0, A_QPAD - A_QK))).reshape(A_Q_LORA, A_HEADS * A_QPAD)
    w_ukv_h = w_ukv.reshape(A_KV_LORA, A_HEADS, A_NOPE + A_VDIM)
    w_kn = w_ukv_h[:, :, :A_NOPE].reshape(A_KV_LORA, A_HEADS * A_NOPE)
    w_v = w_ukv_h[:, :, A_NOPE:].reshape(A_KV_LORA, A_WIDTH)
    return w_in_pad, w_uq_pad.T.astype(BF16), w_kn.astype(BF16), w_v.T.astype(BF16)


def _swa_feature_order(w, axis):
    shape = w.shape
    split = shape[:axis] + (B_KV_HEADS, B_GROUP, B_HEAD_DIM) + shape[axis + 1:]
    return jnp.swapaxes(w.reshape(split), axis, axis + 1).reshape(shape)


def kernel(x, w_in_a, q_norm_a, kv_norm_a, w_uq_a, w_ukv_a, w_o_a, w_kv_shared, w_in_b, sinks_b,
           w_o_b, rel_bias_table, ln_gain, ln_bias):
    B, S, _ = x.shape
    tm = min(512, S)
    tq = min(512, S)
    rope = _rope_tables(S)

    prev = None
    for i in range(N_A_LAYERS):
        w_in, w_uqT, w_kn, w_vT = _mla_weights(w_in_a[i], w_uq_a[i], w_ukv_a[i])
        outs = _mla_proj(x, w_in, q_norm_a[i][None, :], kv_norm_a[i][None, :], w_uqT, w_kn, w_vT,
                         rope, tm, tq, tq, prev)
        if prev is not None:
            x, *outs = outs
        qT, kn, kr, vT, sg = outs
        o = _mla_attn(qT, kn, kr, vT)
        prev = (o, sg, w_o_a[i].astype(BF16), ln_gain[i][None, :], ln_bias[i][None, :])
    o, sg, wo, g, b = prev
    x, ks, vsT = _out_ln_kv(o, sg, x, wo, g, b, w_kv_shared[:, :B_KV_WIDTH].astype(BF16),
                            w_kv_shared[:, B_KV_WIDTH:].T.astype(BF16), tm)

    bias = _band_bias(rel_bias_table)
    for j in range(DEPTH - N_A_LAYERS):
        layer = N_A_LAYERS + j
        wqT = w_in_b[j][:, :B_WIDTH].T.astype(BF16)
        wg = _swa_feature_order(w_in_b[j][:, B_WIDTH:], 1).astype(BF16)
        wo = _swa_feature_order(w_o_b[j], 0).astype(BF16)
        sink = jnp.repeat(sinks_b[j].astype(F32).reshape(B_KV_HEADS, 1, B_GROUP) * LOG2_E, Q_BLOCK, axis=2)
        x = _swa_layer(x, wqT, wg, wo, ks, vsT, bias, sink,
                       ln_gain[layer][None, :], ln_bias[layer][None, :], min(512, S))
    return x
```

```python
import functools
import math

import jax
import jax.numpy as jnp
import numpy as np
from jax import lax
from jax.experimental import pallas as pl
from jax.experimental.pallas import tpu as pltpu

F32 = jnp.float32
BF16 = jnp.bfloat16

D_MODEL = 1024
DEPTH = 4
CHUNK = 64
Q_BLOCK = 128
N_A_LAYERS = DEPTH // 2
A_HEADS = 8
A_NOPE = 128
A_ROPE = 64
A_VDIM = 128
A_QK = A_NOPE + A_ROPE
A_Q_LORA = 384
A_KV_LORA = 256
A_WIDTH = A_HEADS * A_VDIM
A_QPAD = 256
ROPE_THETA = 10000.0
ROPE_HALF = A_ROPE // 2
B_Q_HEADS = 16
B_KV_HEADS = 2
B_GROUP = B_Q_HEADS // B_KV_HEADS
B_HEAD_DIM = 64
B_WIDTH = B_Q_HEADS * B_HEAD_DIM
B_KV_WIDTH = B_KV_HEADS * B_HEAD_DIM
WINDOW = 128
WIN_CHUNKS = WINDOW // CHUNK
NUM_BUCKETS = 32
MAX_DISTANCE = 128
DEEPNORM_ALPHA = (2 * DEPTH) ** 0.25
NORM_EPS = 1e-5
MASK_VALUE = -1e30
LOG2_E = math.log2(math.e)

LANES = 128
VMEM_LIMIT_BYTES = 56 * 1024 * 1024

_NT = (((1,), (1,)), ((), ()))
UNROLL = 34
BOUND_MARGIN = 1.0 + 2.0 ** -6
DENOM_FLOOR = 2.0 ** -80


def _cparams(n_axes, flags=None):
    return pltpu.CompilerParams(
        dimension_semantics=("arbitrary",) * n_axes, vmem_limit_bytes=VMEM_LIMIT_BYTES, flags=flags)


def _sigmoid(x):
    return 1.0 / (1.0 + jnp.exp(-x))


def _rms_scale(c, g):
    return c * lax.rsqrt(jnp.mean(c * c, axis=-1, keepdims=True) + NORM_EPS) * g


def _deepnorm_ln(x, y, g, b):
    z = DEEPNORM_ALPHA * x + y
    mu = jnp.mean(z, axis=-1, keepdims=True)
    zc = z - mu
    var = jnp.mean(zc * zc, axis=-1, keepdims=True)
    return zc * lax.rsqrt(var + NORM_EPS) * g + b


def _mla_proj_kernel(*refs, after_layer):
    if after_layer:
        o_ref, sg_prev_ref, x_prev_ref, wo_ref, g_ref, b_ref, *refs = refs
        *refs, xo_ref, qT_ref, kn_ref, kr_ref, vT_ref, sg_ref, kmax_ref = refs
    else:
        x_ref, *refs = refs
        *refs, qT_ref, kn_ref, kr_ref, vT_ref, sg_ref, kmax_ref = refs
    (w_in_ref, gq_ref, gkv_ref, w_uqT_ref, w_kn_ref, w_vT_ref,
     cosT_ref, sinT_ref, cosk_ref, sinka_ref, sinkb_ref, ktail_ref) = refs

    @pl.when(pl.program_id(1) == 0)
    def _start_of_sequence():
        kmax_ref[...] = jnp.zeros(kmax_ref.shape, F32)

    if after_layer:
        _gated_out_ln(o_ref, sg_prev_ref, x_prev_ref, wo_ref, g_ref, b_ref, xo_ref)
        x_ref = xo_ref
    xb = x_ref[...].astype(BF16)
    h = jnp.dot(xb, w_in_ref[...], preferred_element_type=F32)
    o_kv = A_Q_LORA
    o_kr = A_Q_LORA + A_KV_LORA
    o_g = o_kr + LANES
    cq = _rms_scale(h[:, :o_kv], gq_ref[...]).astype(BF16)
    ckv = _rms_scale(h[:, o_kv:o_kr], gkv_ref[...]).astype(BF16)

    scale = A_QK ** -0.5 * math.log2(math.e)
    qT = lax.dot_general(w_uqT_ref[...], cq, _NT, preferred_element_type=F32) * scale

    krs = h[:, o_kr:o_g]
    kr = (krs * cosk_ref[...]
          + pltpu.roll(krs, LANES - ROPE_HALF, axis=1) * sinka_ref[...]
          + pltpu.roll(krs, ROPE_HALF, axis=1) * sinkb_ref[...])
    kr_ref[...] = (kr + ktail_ref[...]).astype(BF16)
    kn = jnp.dot(ckv, w_kn_ref[...], preferred_element_type=F32)

    kr2 = jnp.sum(kr * kr, axis=1, keepdims=True)
    lane = lax.broadcasted_iota(jnp.int32, kmax_ref.shape, 1)
    kmax2 = kmax_ref[...]
    for hh in range(A_HEADS):
        knh = kn[:, hh * A_NOPE:(hh + 1) * A_NOPE]
        k2 = jnp.max(jnp.sum(knh * knh, axis=1, keepdims=True) + kr2, axis=0, keepdims=True)
        kmax2 = jnp.where(lane == hh, jnp.maximum(kmax2, k2), kmax2)
    kmax_ref[...] = kmax2

    cos = cosT_ref[...]
    sin = sinT_ref[...]
    n_r1 = A_NOPE
    n_r2 = n_r1 + ROPE_HALF
    n_r3 = n_r2 + ROPE_HALF
    tail_row = lax.broadcasted_iota(jnp.int32, (A_QPAD - A_QK, qT.shape[1]), 0)
    bound_row = vT_ref.shape[-1] // CHUNK
    for hh in range(A_HEADS):
        r0 = hh * A_QK
        t1 = qT[r0 + n_r1:r0 + n_r2]
        t2 = qT[r0 + n_r2:r0 + n_r3]
        qT_ref[hh, :n_r1, :] = qT[r0:r0 + n_r1].astype(BF16)
        qT_ref[hh, n_r1:n_r2, :] = (t1 * cos - t2 * sin).astype(BF16)
        qT_ref[hh, n_r2:n_r3, :] = (t2 * cos + t1 * sin).astype(BF16)
        qh = qT[r0:r0 + A_QK]
        q2 = jnp.sum(qh * qh, axis=0, keepdims=True)
        bound = jnp.sqrt(q2 * kmax2[:, hh:hh + 1]) * BOUND_MARGIN
        qT_ref[hh, n_r3:, :] = jnp.where(tail_row == bound_row, -bound, 0.0).astype(BF16)

    kn = kn.astype(BF16)
    vT = lax.dot_general(w_vT_ref[...], ckv, _NT, preferred_element_type=F32).astype(BF16)
    tk = vT_ref.shape[-1]
    for hh in range(A_HEADS):
        kn_ref[hh] = kn[:, hh * A_NOPE:(hh + 1) * A_NOPE]
        for n in range(vT_ref.shape[1]):
            vT_ref[hh, n] = vT[hh * A_VDIM:(hh + 1) * A_VDIM, n * tk:(n + 1) * tk]

    gate = h[:, o_g:]
    sg_ref[...] = (gate * _sigmoid(gate)).astype(BF16)


def _mla_proj(x, w_in, gq, gkv, w_uqT, w_kn, w_vT, rope, tm, tq, tk, prev=None):
    B, S, _ = x.shape
    nt = S // tm
    cosT, sinT, cosk, sinka, sinkb = rope
    key = np.arange(S)[:, None]
    lane = np.arange(LANES)[None, :] - A_ROPE
    ktail = jnp.asarray((lane == (key % tk) // CHUNK) | (lane == tk // CHUNK), F32)
    once = pl.Buffered(1)
    const = lambda *shape: pl.BlockSpec(shape, lambda b, i: (0,) * len(shape), pipeline_mode=once)
    tok_lanes = pl.BlockSpec((tm, LANES), lambda b, i: (i, 0))
    row = pl.BlockSpec((None, tm, D_MODEL), lambda b, i: (b, i, 0))
    in_specs = [
        const(*w_in.shape), const(*gq.shape), const(*gkv.shape),
        const(*w_uqT.shape), const(*w_kn.shape), const(*w_vT.shape),
        pl.BlockSpec((ROPE_HALF, tm), lambda b, i: (0, i)),
        pl.BlockSpec((ROPE_HALF, tm), lambda b, i: (0, i)),
        tok_lanes, tok_lanes, tok_lanes, tok_lanes,
    ]
    operands = (w_in, gq, gkv, w_uqT, w_kn, w_vT, cosT, sinT, cosk, sinka, sinkb, ktail)
    out_specs = [
        pl.BlockSpec((None, A_HEADS, None, A_QPAD, tm),
                     lambda b, i: (b, 0, i // (tq // tm), 0, i % (tq // tm))),
        pl.BlockSpec((None, A_HEADS, tm, A_NOPE), lambda b, i: (b, 0, i, 0)),
        pl.BlockSpec((None, tm, LANES), lambda b, i: (b, i, 0)),
        pl.BlockSpec((None, A_HEADS, tm // tk, A_VDIM, tk), lambda b, i: (b, 0, i, 0, 0)),
        pl.BlockSpec((None, tm, A_WIDTH), lambda b, i: (b, i, 0)),
    ]
    out_shape = [
        jax.ShapeDtypeStruct((B, A_HEADS, S // tq, A_QPAD, tq), BF16),
        jax.ShapeDtypeStruct((B, A_HEADS, S, A_NOPE), BF16),
        jax.ShapeDtypeStruct((B, S, LANES), BF16),
        jax.ShapeDtypeStruct((B, A_HEADS, S // tk, A_VDIM, tk), BF16),
        jax.ShapeDtypeStruct((B, S, A_WIDTH), BF16),
    ]
    if prev is None:
        in_specs = [row] + in_specs
        operands = (x,) + operands
    else:
        o, sg, wo, g, b = prev
        heads = pl.BlockSpec((None, A_HEADS, tm, A_VDIM), lambda b_, i: (b_, 0, i, 0))
        in_specs = [heads, row, row, const(*wo.shape), const(*g.shape), const(*b.shape)] + in_specs
        operands = (o, sg, x, wo, g, b) + operands
        out_specs = [row] + out_specs
        out_shape = [jax.ShapeDtypeStruct(x.shape, F32)] + out_shape
    return pl.pallas_call(
        functools.partial(_mla_proj_kernel, after_layer=prev is not None),
        grid=(B, nt),
        in_specs=in_specs,
        out_specs=out_specs,
        out_shape=out_shape,
        scratch_shapes=[pltpu.VMEM((1, LANES), F32)],
        compiler_params=_cparams(2),
        name="mla_proj" if prev is None else "out_ln_mla_proj",
    )(*operands)


def _mla_attn_kernel(qT_ref, kn_ref, kr_ref, vT_ref, mrow_ref, o_ref, *scratch, nq, tq):
    *p_refs, l_ref, acc_ref = scratch
    n_buf = len(p_refs)

    def shifted_scores(qi, t):
        k0 = pl.multiple_of(t * tq, tq)
        kcat = jnp.concatenate([kn_ref[pl.ds(k0, tq), :], kr_ref[pl.ds(k0, tq), :]], axis=1)
        tail = qT_ref[qi, A_QK:, :] + mrow_ref[jnp.where(t == qi, 1, 0)]
        q_full = jnp.concatenate([qT_ref[qi, :A_QK, :], tail], axis=0)
        return jnp.dot(kcat, q_full, preferred_element_type=F32)

    def weighted_values(qi, t, buf):
        acc_ref[qi] += jnp.dot(vT_ref[t], p_refs[buf][...], preferred_element_type=F32)

    acc_ref[...] = jnp.zeros(acc_ref.shape, F32)
    l_ref[...] = jnp.zeros(l_ref.shape, F32)
    for p_ref in p_refs[-2:]:
        p_ref[...] = jnp.zeros(p_ref.shape, BF16)

    def step(carry, buf):
        qi, t, qi_1, t_1, qi_2, t_2 = carry
        weighted_values(qi_2, t_2, (buf - 2) % n_buf)
        p = jnp.exp2(shifted_scores(qi, t))
        l_ref[qi] += jnp.sum(p, axis=0, keepdims=True)
        p_refs[buf][...] = p.astype(BF16)
        last = t == qi
        return jnp.where(last, qi + 1, qi), jnp.where(last, 0, t + 1), qi, t, qi_1, t_1

    carry = (jnp.int32(0),) * 6
    n_steps = nq * (nq + 1) // 2
    n_lead = n_steps % UNROLL
    for u in range(n_lead):
        carry = step(carry, u % n_buf)

    def steps(it, carry):
        for u in range(UNROLL):
            carry = step(carry, (n_lead + u) % n_buf)
        return carry

    _, _, qi_1, t_1, qi_2, t_2 = lax.fori_loop(0, n_steps // UNROLL, steps, carry)
    weighted_values(qi_2, t_2, (n_steps - 2) % n_buf)
    weighted_values(qi_1, t_1, (n_steps - 1) % n_buf)

    def write_out(qi, acc, l):
        r0 = pl.multiple_of(qi * tq, tq)
        o_ref[pl.ds(r0, tq), :] = (acc / l).T.astype(o_ref.dtype)

    def finish(qi, denom_min):
        l = l_ref[qi]
        write_out(qi, acc_ref[qi], l)
        return jnp.minimum(denom_min, l)

    denom_min = lax.fori_loop(0, nq, finish, jnp.full((1, tq), jnp.inf, F32), unroll=4)

    @pl.when(jnp.logical_not(jnp.min(denom_min) >= DENOM_FLOOR))
    def _redo_with_running_max():
        def query_tile(qi, carry):
            def key_tile(t, mla):
                m, l, acc = mla
                s = shifted_scores(qi, t)
                m_new = jnp.maximum(m, jnp.max(s, axis=0, keepdims=True))
                alpha = jnp.exp2(m - m_new)
                p = jnp.exp2(s - m_new)
                l = l * alpha + jnp.sum(p, axis=0, keepdims=True)
                acc = acc * alpha + jnp.dot(vT_ref[t], p.astype(BF16), preferred_element_type=F32)
                return m_new, l, acc

            init = (jnp.full((1, tq), MASK_VALUE, F32), jnp.zeros((1, tq), F32),
                    jnp.zeros((A_VDIM, tq), F32))
            _, l, acc = lax.fori_loop(0, qi + 1, key_tile, init)
            write_out(qi, acc, l)
            return carry

        lax.fori_loop(0, nq, query_tile, 0)


def _mla_attn(qT, kn, kr, vT):
    B, H, nq, _, tq = qT.shape
    S = kn.shape[2]
    assert vT.shape[2:] == (nq, A_VDIM, tq) and tq // CHUNK < A_QPAD - A_QK
    a = np.arange(A_QPAD - A_QK)[:, None]
    qchunk = (np.arange(tq) // CHUNK)[None, :]
    diagonal = np.where((qchunk < a) & (a < tq // CHUNK), MASK_VALUE, 0.0).astype(np.float32)
    mrow = jnp.asarray(np.stack([np.zeros_like(diagonal), diagonal]), BF16)
    return pl.pallas_call(
        functools.partial(_mla_attn_kernel, nq=nq, tq=tq),
        grid=(B, H),
        in_specs=[
            pl.BlockSpec((None, None, nq, A_QPAD, tq), lambda b, h: (b, h, 0, 0, 0)),
            pl.BlockSpec((None, None, S, A_NOPE), lambda b, h: (b, h, 0, 0)),
            pl.BlockSpec((None, S, LANES), lambda b, h: (b, 0, 0)),
            pl.BlockSpec((None, None, nq, A_VDIM, tq), lambda b, h: (b, h, 0, 0, 0)),
            pl.BlockSpec(mrow.shape, lambda b, h: (0, 0, 0)),
        ],
        out_specs=pl.BlockSpec((None, None, S, A_VDIM), lambda b, h: (b, h, 0, 0)),
        out_shape=jax.ShapeDtypeStruct((B, H, S, A_VDIM), BF16),
        scratch_shapes=[
            *([pltpu.VMEM((tq, tq), BF16)] * UNROLL),
            pltpu.VMEM((nq, 1, tq), F32),
            pltpu.VMEM((nq, A_VDIM, tq), F32),
        ],
        compiler_params=_cparams(2),
        name="mla_attn",
    )(qT, kn, kr, vT, mrow)


def _gated_out_ln(o_ref, sg_ref, x_ref, wo_ref, g_ref, b_ref, xo_ref):
    half = xo_ref.shape[0] // 2
    for rows in (slice(0, half), slice(half, 2 * half)):
        o = jnp.concatenate([o_ref[hh, rows, :] for hh in range(o_ref.shape[0])], axis=1)
        y = jnp.dot(o * sg_ref[rows, :], wo_ref[...], preferred_element_type=F32)
        xo_ref[rows, :] = _deepnorm_ln(x_ref[rows, :], y, g_ref[...], b_ref[...])


def _out_ln_kv_kernel(o_ref, sg_ref, x_ref, wo_ref, g_ref, b_ref, wk_ref, wvT_ref,
                      xo_ref, ks_ref, vsT_ref):
    _gated_out_ln(o_ref, sg_ref, x_ref, wo_ref, g_ref, b_ref, xo_ref)
    xb = xo_ref[...].astype(BF16)
    ks_ref[...] = jnp.dot(xb, wk_ref[...], preferred_element_type=F32).astype(BF16)
    vT = lax.dot_general(wvT_ref[...], xb, _NT, preferred_element_type=F32).astype(BF16)
    for n in range(vsT_ref.shape[0]):
        vsT_ref[n] = vT[:, n * Q_BLOCK:(n + 1) * Q_BLOCK]


def _out_ln_kv(o, sg, x, wo, g, b, wk, wvT, tm):
    B, S, _ = x.shape
    row = pl.BlockSpec((None, tm, D_MODEL), lambda b_, i: (b_, i, 0))
    const = lambda *shape: pl.BlockSpec(shape, lambda b_, i: (0,) * len(shape))
    heads = pl.BlockSpec((None, A_HEADS, tm, A_VDIM), lambda b_, i: (b_, 0, i, 0))
    nb = tm // Q_BLOCK
    return pl.pallas_call(
        _out_ln_kv_kernel, grid=(B, S // tm),
        in_specs=[heads, row, row, const(*wo.shape), const(*g.shape), const(*b.shape),
                  const(*wk.shape), const(*wvT.shape)],
        out_specs=[
            row,
            pl.BlockSpec((None, tm, B_KV_WIDTH), lambda b_, i: (b_, i, 0)),
            pl.BlockSpec((None, nb, B_KV_WIDTH, Q_BLOCK), lambda b_, i: (b_, i, 0, 0)),
        ],
        out_shape=[
            jax.ShapeDtypeStruct(x.shape, F32),
            jax.ShapeDtypeStruct((B, S, B_KV_WIDTH), BF16),
            jax.ShapeDtypeStruct((B, S // Q_BLOCK, B_KV_WIDTH, Q_BLOCK), BF16),
        ],
        compiler_params=_cparams(2), name="out_ln_kv",
    )(o, sg, x, wo, g, b, wk, wvT)


def _band_bias_kernel(tab_ref, bkt_ref, ok_ref, out_ref):
    bkt = bkt_ref[...]
    ok = ok_ref[...] != 0
    has_prev = lax.broadcasted_iota(jnp.int32, bkt.shape, 0) >= Q_BLOCK
    for h in range(B_Q_HEADS):
        acc = jnp.zeros(bkt.shape, F32)
        for bucket in range(NUM_BUCKETS):
            acc = jnp.where(bkt == bucket, tab_ref[bucket, h], acc)
        generic = jnp.where(ok, acc * LOG2_E, MASK_VALUE)
        g, hi = divmod(h, B_GROUP)
        out_ref[0, g, :, hi * Q_BLOCK:(hi + 1) * Q_BLOCK] = jnp.where(has_prev, generic, MASK_VALUE)
        out_ref[1, g, :, hi * Q_BLOCK:(hi + 1) * Q_BLOCK] = generic


def _t5_bucket_map(rel):
    half = NUM_BUCKETS // 2
    ret = jnp.where(rel > 0, half, 0)
    n = jnp.abs(rel)
    max_exact = half // 2
    large = max_exact + (jnp.log(jnp.maximum(n, 1).astype(F32) / max_exact)
                         / math.log(MAX_DISTANCE / max_exact) * (half - max_exact)).astype(jnp.int32)
    large = jnp.minimum(large, half - 1)
    return ret + jnp.where(n < max_exact, n, large)


def _band_bias(rel_bias_table):
    j = jnp.arange(2 * Q_BLOCK)[:, None]
    r = jnp.arange(Q_BLOCK)[None, :]
    rel = j - Q_BLOCK - r
    d = (r // CHUNK) - ((j - Q_BLOCK) // CHUNK)
    ok = ((d >= 0) & (d <= WIN_CHUNKS)).astype(jnp.int32)
    bkt = _t5_bucket_map(rel).astype(jnp.int32)
    return pl.pallas_call(
        _band_bias_kernel,
        in_specs=[
            pl.BlockSpec(memory_space=pltpu.SMEM),
            pl.BlockSpec(memory_space=pltpu.VMEM),
            pl.BlockSpec(memory_space=pltpu.VMEM),
        ],
        out_specs=pl.BlockSpec(memory_space=pltpu.VMEM),
        out_shape=jax.ShapeDtypeStruct((2, B_KV_HEADS, 2 * Q_BLOCK, B_GROUP * Q_BLOCK), F32),
        name="band_bias",
    )(rel_bias_table, bkt, ok)


def _swa_layer_kernel(xn_ref, x_ref, wqT_ref, wg_ref, wo_ref, ks_ref, vsT_ref, bias_ref, sink_ref,
                      g_ref, b_ref, xo_ref, qT0_ref, sg0_ref, qT1_ref, sg1_ref, og_ref, s_ref, *, nblk):
    step = pl.program_id(1)

    @pl.when((pl.program_id(0) == 0) & (step == 0))
    def _first_step():
        qT0_ref[...] = jnp.zeros(qT0_ref.shape, qT0_ref.dtype)
        sg0_ref[...] = jnp.zeros(sg0_ref.shape, sg0_ref.dtype)

    def body(qT_ref, sg_ref, qT_next_ref, sg_next_ref):
        tile = jnp.maximum(step - 1, 0)
        xb_next = xn_ref[...].astype(BF16)
        zeros = jnp.zeros((B_HEAD_DIM, B_GROUP * Q_BLOCK), BF16)
        chunk = B_WIDTH // nblk
        half = max(nblk // 2, 1)

        for n in range(nblk):
            blk = tile * nblk + n
            cur = pl.multiple_of(blk * Q_BLOCK, Q_BLOCK)
            prev_blk = jnp.maximum(blk - 1, 0)
            prev = pl.multiple_of(prev_blk * Q_BLOCK, Q_BLOCK)
            kband = jnp.concatenate(
                [ks_ref[pl.ds(prev, Q_BLOCK), :], ks_ref[pl.ds(cur, Q_BLOCK), :]], axis=0)
            vband = jnp.concatenate([vsT_ref[prev_blk], vsT_ref[blk]], axis=1)
            table = jnp.minimum(blk, 1)
            lanes = slice(n * Q_BLOCK, (n + 1) * Q_BLOCK)
            score_max = []
            for g in range(B_KV_HEADS):
                qg = jnp.concatenate(
                    [qT_ref[(g * B_GROUP + hi) * B_HEAD_DIM:(g * B_GROUP + hi + 1) * B_HEAD_DIM, lanes]
                     for hi in range(B_GROUP)], axis=1)
                rhs = jnp.concatenate([qg, zeros] if g == 0 else [zeros, qg], axis=0)
                sT = jnp.dot(kband, rhs, preferred_element_type=F32) + bias_ref[table, g]
                s_ref[g] = sT
                score_max.append(jnp.max(sT, axis=0, keepdims=True))

            part = slice(n * chunk, (n + 1) * chunk)
            qT_next_ref[part, :] = (
                lax.dot_general(wqT_ref[part, :], xb_next, _NT, preferred_element_type=F32)
                * (B_HEAD_DIM ** -0.5 * LOG2_E)).astype(BF16)
            gate = jnp.dot(xb_next, wg_ref[:, part], preferred_element_type=F32)
            sg_next_ref[:, part] = gate * _sigmoid(gate)

            o_parts = []
            for g in range(B_KV_HEADS):
                sink = sink_ref[g]
                m = jnp.maximum(score_max[g], sink)
                e = jnp.exp2(s_ref[g] - m)
                den = jnp.sum(e, axis=0, keepdims=True) + jnp.exp2(sink - m)
                v_rows = slice(g * B_HEAD_DIM, (g + 1) * B_HEAD_DIM)
                oT = jnp.dot(vband[v_rows, :], e.astype(BF16), preferred_element_type=F32)
                o_parts.append(oT / den)
            for hi in range(B_GROUP):
                cols = slice(hi * Q_BLOCK, (hi + 1) * Q_BLOCK)
                o_hi = jnp.concatenate([o_parts[0][:, cols], o_parts[1][:, cols]], axis=0).T
                og_ref[lanes, cols] = (o_hi * sg_ref[lanes, cols]).astype(BF16)

            if (n + 1) % half == 0:
                rows = slice((n + 1 - half) * Q_BLOCK, (n + 1) * Q_BLOCK)
                y = jnp.dot(og_ref[rows, :], wo_ref[...], preferred_element_type=F32)
                xo_ref[rows, :] = _deepnorm_ln(x_ref[rows, :], y, g_ref[...], b_ref[...])

    @pl.when(step % 2 == 0)
    def _even():
        body(qT0_ref, sg0_ref, qT1_ref, sg1_ref)

    @pl.when(step % 2 == 1)
    def _odd():
        body(qT1_ref, sg1_ref, qT0_ref, sg0_ref)


def _swa_layer(x, wqT, wg, wo, ks, vsT, bias, sink, g, b, tm):
    B, S, _ = x.shape
    nt = S // tm
    nblk = tm // Q_BLOCK
    once = pl.Buffered(1)
    const = lambda *shape: pl.BlockSpec(shape, lambda b_, i: (0,) * len(shape), pipeline_mode=once)
    done = pl.BlockSpec((None, tm, D_MODEL), lambda b_, i: (b_, jnp.maximum(i - 1, 0), 0))
    return pl.pallas_call(
        functools.partial(_swa_layer_kernel, nblk=nblk),
        grid=(B, nt + 1),
        in_specs=[
            pl.BlockSpec((None, tm, D_MODEL), lambda b_, i: (b_, jnp.minimum(i, nt - 1), 0)),
            done, const(*wqT.shape), const(*wg.shape), const(*wo.shape),
            pl.BlockSpec((None, S, B_KV_WIDTH), lambda b_, i: (b_, 0, 0), pipeline_mode=once),
            pl.BlockSpec((None, S // Q_BLOCK, B_KV_WIDTH, Q_BLOCK), lambda b_, i: (b_, 0, 0, 0),
                         pipeline_mode=once),
            const(*bias.shape), const(*sink.shape), const(*g.shape), const(*b.shape),
        ],
        out_specs=done,
        out_shape=jax.ShapeDtypeStruct(x.shape, F32),
        scratch_shapes=[
            pltpu.VMEM((B_WIDTH, tm), BF16), pltpu.VMEM((tm, B_WIDTH), F32),
            pltpu.VMEM((B_WIDTH, tm), BF16), pltpu.VMEM((tm, B_WIDTH), F32),
            pltpu.VMEM((tm, B_WIDTH), BF16),
            pltpu.VMEM((B_KV_HEADS, 2 * Q_BLOCK, B_GROUP * Q_BLOCK), F32),
        ],
        compiler_params=_cparams(2),
        name="swa_layer",
    )(x, x, wqT, wg, wo, ks, vsT, bias, sink, g, b)


def _rope_tables(S):
    inv = ROPE_THETA ** (-jnp.arange(ROPE_HALF, dtype=F32) / ROPE_HALF)
    ang = jnp.arange(S, dtype=F32)[:, None] * inv[None, :]
    cos, sin = jnp.cos(ang), jnp.sin(ang)
    zeros = jnp.zeros((S, LANES - A_ROPE), F32)
    z32 = jnp.zeros((S, ROPE_HALF), F32)
    cosk = jnp.concatenate([cos, cos, zeros], axis=1)
    sinka = jnp.concatenate([-sin, z32, zeros], axis=1)
    sinkb = jnp.concatenate([z32, sin, zeros], axis=1)
    return cos.T, sin.T, cosk, sinka, sinkb


def _mla_weights(w_in, w_uq, w_ukv):
    o_kr = A_Q_LORA + A_KV_LORA
    w_in_pad = jnp.concatenate(
        [w_in[:, :o_kr + A_ROPE], jnp.zeros((D_MODEL, LANES - A_ROPE), w_in.dtype),
         w_in[:, o_kr + A_ROPE:]], axis=1).astype(BF16)
    w_ukv_h = w_ukv.reshape(A_KV_LORA, A_HEADS, A_NOPE + A_VDIM)
    w_kn = w_ukv_h[:, :, :A_NOPE].reshape(A_KV_LORA, A_HEADS * A_NOPE)
    w_v = w_ukv_h[:, :, A_NOPE:].reshape(A_KV_LORA, A_WIDTH)
    return w_in_pad, w_uq.T.astype(BF16), w_kn.astype(BF16), w_v.T.astype(BF16)


def _swa_feature_order(w, axis):
    shape = w.shape
    split = shape[:axis] + (B_KV_HEADS, B_GROUP, B_HEAD_DIM) + shape[axis + 1:]
    return jnp.swapaxes(w.reshape(split), axis, axis + 1).reshape(shape)


def kernel(x, w_in_a, q_norm_a, kv_norm_a, w_uq_a, w_ukv_a, w_o_a, w_kv_shared, w_in_b, sinks_b,
           w_o_b, rel_bias_table, ln_gain, ln_bias):
    B, S, _ = x.shape
    tm = min(512, S)
    tq = min(512, S)
    rope = _rope_tables(S)

    prev = None
    for i in range(N_A_LAYERS):
        w_in, w_uqT, w_kn, w_vT = _mla_weights(w_in_a[i], w_uq_a[i], w_ukv_a[i])
        outs = _mla_proj(x, w_in, q_norm_a[i][None, :], kv_norm_a[i][None, :], w_uqT, w_kn, w_vT,
                         rope, tm, tq, tq, prev)
        if prev is not None:
            x, *outs = outs
        qT, kn, kr, vT, sg = outs
        o = _mla_attn(qT, kn, kr, vT)
        prev = (o, sg, w_o_a[i].astype(BF16), ln_gain[i][None, :], ln_bias[i][None, :])
    o, sg, wo, g, b = prev
    x, ks, vsT = _out_ln_kv(o, sg, x, wo, g, b, w_kv_shared[:, :B_KV_WIDTH].astype(BF16),
                            w_kv_shared[:, B_KV_WIDTH:].T.astype(BF16), tm)

    bias = _band_bias(rel_bias_table)
    for j in range(DEPTH - N_A_LAYERS):
        layer = N_A_LAYERS + j
        wqT = w_in_b[j][:, :B_WIDTH].T.astype(BF16)
        wg = _swa_feature_order(w_in_b[j][:, B_WIDTH:], 1).astype(BF16)
        wo = _swa_feature_order(w_o_b[j], 0).astype(BF16)
        sink = jnp.repeat(sinks_b[j].astype(F32).reshape(B_KV_HEADS, 1, B_GROUP) * LOG2_E, Q_BLOCK, axis=2)
        x = _swa_layer(x, wqT, wg, wo, ks, vsT, bias, sink,
                       ln_gain[layer][None, :], ln_bias[layer][None, :], min(512, S))
    return x
```

```python
import functools
import math

import jax
import jax.numpy as jnp
import numpy as np
from jax import lax
from jax.experimental import pallas as pl
from jax.experimental.pallas import tpu as pltpu

F32 = jnp.float32
BF16 = jnp.bfloat16

D_MODEL = 1024
DEPTH = 4
CHUNK = 64
Q_BLOCK = 128
N_A_LAYERS = DEPTH // 2
A_HEADS = 8
A_NOPE = 128
A_ROPE = 64
A_VDIM = 128
A_QK = A_NOPE + A_ROPE
A_Q_LORA = 384
A_KV_LORA = 256
A_WIDTH = A_HEADS * A_VDIM
A_QPAD = 256
ROPE_THETA = 10000.0
ROPE_HALF = A_ROPE // 2
B_Q_HEADS = 16
B_KV_HEADS = 2
B_GROUP = B_Q_HEADS // B_KV_HEADS
B_HEAD_DIM = 64
B_WIDTH = B_Q_HEADS * B_HEAD_DIM
B_KV_WIDTH = B_KV_HEADS * B_HEAD_DIM
B_VROWS = B_HEAD_DIM + 16
WINDOW = 128
WIN_CHUNKS = WINDOW // CHUNK
NUM_BUCKETS = 32
MAX_DISTANCE = 128
DEEPNORM_ALPHA = (2 * DEPTH) ** 0.25
NORM_EPS = 1e-5
MASK_VALUE = -1e30
LOG2_E = math.log2(math.e)

LANES = 128
VMEM_LIMIT_BYTES = 56 * 1024 * 1024

_NT = (((1,), (1,)), ((), ()))
UNROLL = 34
BOUND_MARGIN = 1.0 + 2.0 ** -6
DENOM_FLOOR = 2.0 ** -80


def _cparams(n_axes, flags=None):
    return pltpu.CompilerParams(
        dimension_semantics=("arbitrary",) * n_axes, vmem_limit_bytes=VMEM_LIMIT_BYTES, flags=flags)


def _sigmoid(x):
    return 1.0 / (1.0 + jnp.exp(-x))


def _rms_scale(c, g):
    return c * lax.rsqrt(jnp.mean(c * c, axis=-1, keepdims=True) + NORM_EPS) * g


def _deepnorm_ln(x, y, g, b):
    z = DEEPNORM_ALPHA * x + y
    mu = jnp.mean(z, axis=-1, keepdims=True)
    zc = z - mu
    var = jnp.mean(zc * zc, axis=-1, keepdims=True)
    return zc * lax.rsqrt(var + NORM_EPS) * g + b


def _mla_proj_kernel(*refs, after_layer):
    if after_layer:
        o_ref, sg_prev_ref, x_prev_ref, wo_ref, g_ref, b_ref, *refs = refs
        *refs, xo_ref, qT_ref, kn_ref, kr_ref, vT_ref, sg_ref, kmax_ref = refs
    else:
        x_ref, *refs = refs
        *refs, qT_ref, kn_ref, kr_ref, vT_ref, sg_ref, kmax_ref = refs
    (w_in_ref, gq_ref, gkv_ref, w_uqT_ref, w_kn_ref, w_vT_ref,
     cosT_ref, sinT_ref, cosk_ref, sinka_ref, sinkb_ref, ktail_ref) = refs

    @pl.when(pl.program_id(1) == 0)
    def _start_of_sequence():
        kmax_ref[...] = jnp.zeros(kmax_ref.shape, F32)

    if after_layer:
        _gated_out_ln(o_ref, sg_prev_ref, x_prev_ref, wo_ref, g_ref, b_ref, xo_ref)
        x_ref = xo_ref
    xb = x_ref[...].astype(BF16)
    h = jnp.dot(xb, w_in_ref[...], preferred_element_type=F32)
    o_kv = A_Q_LORA
    o_kr = A_Q_LORA + A_KV_LORA
    o_g = o_kr + LANES
    cq = _rms_scale(h[:, :o_kv], gq_ref[...]).astype(BF16)
    ckv = _rms_scale(h[:, o_kv:o_kr], gkv_ref[...]).astype(BF16)

    scale = A_QK ** -0.5 * math.log2(math.e)
    qT = lax.dot_general(w_uqT_ref[...], cq, _NT, preferred_element_type=F32) * scale

    krs = h[:, o_kr:o_g]
    kr = (krs * cosk_ref[...]
          + pltpu.roll(krs, LANES - ROPE_HALF, axis=1) * sinka_ref[...]
          + pltpu.roll(krs, ROPE_HALF, axis=1) * sinkb_ref[...])
    kr_ref[...] = (kr + ktail_ref[...]).astype(BF16)
    kn = jnp.dot(ckv, w_kn_ref[...], preferred_element_type=F32)

    kr2 = jnp.sum(kr * kr, axis=1, keepdims=True)
    lane = lax.broadcasted_iota(jnp.int32, kmax_ref.shape, 1)
    kmax2 = kmax_ref[...]
    for hh in range(A_HEADS):
        knh = kn[:, hh * A_NOPE:(hh + 1) * A_NOPE]
        k2 = jnp.max(jnp.sum(knh * knh, axis=1, keepdims=True) + kr2, axis=0, keepdims=True)
        kmax2 = jnp.where(lane == hh, jnp.maximum(kmax2, k2), kmax2)
    kmax_ref[...] = kmax2

    cos = cosT_ref[...]
    sin = sinT_ref[...]
    n_r1 = A_NOPE
    n_r2 = n_r1 + ROPE_HALF
    n_r3 = n_r2 + ROPE_HALF
    tail_row = lax.broadcasted_iota(jnp.int32, (A_QPAD - A_QK, qT.shape[1]), 0)
    bound_row = vT_ref.shape[-1] // CHUNK
    for hh in range(A_HEADS):
        r0 = hh * A_QK
        t1 = qT[r0 + n_r1:r0 + n_r2]
        t2 = qT[r0 + n_r2:r0 + n_r3]
        qT_ref[hh, :n_r1, :] = qT[r0:r0 + n_r1].astype(BF16)
        qT_ref[hh, n_r1:n_r2, :] = (t1 * cos - t2 * sin).astype(BF16)
        qT_ref[hh, n_r2:n_r3, :] = (t2 * cos + t1 * sin).astype(BF16)
        qh = qT[r0:r0 + A_QK]
        q2 = jnp.sum(qh * qh, axis=0, keepdims=True)
        bound = jnp.sqrt(q2 * kmax2[:, hh:hh + 1]) * BOUND_MARGIN
        qT_ref[hh, n_r3:, :] = jnp.where(tail_row == bound_row, -bound, 0.0).astype(BF16)

    kn = kn.astype(BF16)
    vT = lax.dot_general(w_vT_ref[...], ckv, _NT, preferred_element_type=F32).astype(BF16)
    tk = vT_ref.shape[-1]
    for hh in range(A_HEADS):
        kn_ref[hh] = kn[:, hh * A_NOPE:(hh + 1) * A_NOPE]
        for n in range(vT_ref.shape[1]):
            vT_ref[hh, n] = vT[hh * A_VDIM:(hh + 1) * A_VDIM, n * tk:(n + 1) * tk]

    gate = h[:, o_g:]
    sg_ref[...] = (gate * _sigmoid(gate)).astype(BF16)


def _mla_proj(x, w_in, gq, gkv, w_uqT, w_kn, w_vT, rope, tm, tq, tk, prev=None):
    B, S, _ = x.shape
    nt = S // tm
    cosT, sinT, cosk, sinka, sinkb = rope
    key = np.arange(S)[:, None]
    lane = np.arange(LANES)[None, :] - A_ROPE
    ktail = jnp.asarray((lane == (key % tk) // CHUNK) | (lane == tk // CHUNK), F32)
    once = pl.Buffered(1)
    const = lambda *shape: pl.BlockSpec(shape, lambda b, i: (0,) * len(shape), pipeline_mode=once)
    tok_lanes = pl.BlockSpec((tm, LANES), lambda b, i: (i, 0))
    row = pl.BlockSpec((None, tm, D_MODEL), lambda b, i: (b, i, 0))
    in_specs = [
        const(*w_in.shape), const(*gq.shape), const(*gkv.shape),
        const(*w_uqT.shape), const(*w_kn.shape), const(*w_vT.shape),
        pl.BlockSpec((ROPE_HALF, tm), lambda b, i: (0, i)),
        pl.BlockSpec((ROPE_HALF, tm), lambda b, i: (0, i)),
        tok_lanes, tok_lanes, tok_lanes, tok_lanes,
    ]
    operands = (w_in, gq, gkv, w_uqT, w_kn, w_vT, cosT, sinT, cosk, sinka, sinkb, ktail)
    out_specs = [
        pl.BlockSpec((None, A_HEADS, None, A_QPAD, tm),
                     lambda b, i: (b, 0, i // (tq // tm), 0, i % (tq // tm))),
        pl.BlockSpec((None, A_HEADS, tm, A_NOPE), lambda b, i: (b, 0, i, 0)),
        pl.BlockSpec((None, tm, LANES), lambda b, i: (b, i, 0)),
        pl.BlockSpec((None, A_HEADS, tm // tk, A_VDIM, tk), lambda b, i: (b, 0, i, 0, 0)),
        pl.BlockSpec((None, tm, A_WIDTH), lambda b, i: (b, i, 0)),
    ]
    out_shape = [
        jax.ShapeDtypeStruct((B, A_HEADS, S // tq, A_QPAD, tq), BF16),
        jax.ShapeDtypeStruct((B, A_HEADS, S, A_NOPE), BF16),
        jax.ShapeDtypeStruct((B, S, LANES), BF16),
        jax.ShapeDtypeStruct((B, A_HEADS, S // tk, A_VDIM, tk), BF16),
        jax.ShapeDtypeStruct((B, S, A_WIDTH), BF16),
    ]
    if prev is None:
        in_specs = [row] + in_specs
        operands = (x,) + operands
    else:
        o, sg, wo, g, b = prev
        heads = pl.BlockSpec((None, A_HEADS, None, A_VDIM, tm), lambda b_, i: (b_, 0, i, 0, 0))
        in_specs = [heads, row, row, const(*wo.shape), const(*g.shape), const(*b.shape)] + in_specs
        operands = (o, sg, x, wo, g, b) + operands
        out_specs = [row] + out_specs
        out_shape = [jax.ShapeDtypeStruct(x.shape, F32)] + out_shape
    return pl.pallas_call(
        functools.partial(_mla_proj_kernel, after_layer=prev is not None),
        grid=(B, nt),
        in_specs=in_specs,
        out_specs=out_specs,
        out_shape=out_shape,
        scratch_shapes=[pltpu.VMEM((1, LANES), F32)],
        compiler_params=_cparams(2),
        name="mla_proj" if prev is None else "out_ln_mla_proj",
    )(*operands)


def _mla_attn_kernel(qT_ref, kn_ref, kr_ref, vT_ref, mrow_ref, o_ref, *scratch, nq, tq):
    *p_refs, l_ref, acc_ref = scratch
    n_buf = len(p_refs)

    def shifted_scores(qi, t):
        k0 = pl.multiple_of(t * tq, tq)
        kcat = jnp.concatenate([kn_ref[pl.ds(k0, tq), :], kr_ref[pl.ds(k0, tq), :]], axis=1)
        tail = qT_ref[qi, A_QK:, :] + mrow_ref[jnp.where(t == qi, 1, 0)]
        q_full = jnp.concatenate([qT_ref[qi, :A_QK, :], tail], axis=0)
        return jnp.dot(kcat, q_full, preferred_element_type=F32)

    def weighted_values(qi, t, buf):
        acc_ref[qi] += jnp.dot(vT_ref[t], p_refs[buf][...], preferred_element_type=F32)

    acc_ref[...] = jnp.zeros(acc_ref.shape, F32)
    l_ref[...] = jnp.zeros(l_ref.shape, F32)
    for p_ref in p_refs[-2:]:
        p_ref[...] = jnp.zeros(p_ref.shape, BF16)

    def step(carry, buf):
        qi, t, qi_1, t_1, qi_2, t_2 = carry
        weighted_values(qi_2, t_2, (buf - 2) % n_buf)
        p = jnp.exp2(shifted_scores(qi, t))
        l_ref[qi] += jnp.sum(p, axis=0, keepdims=True)
        p_refs[buf][...] = p.astype(BF16)
        last = t == qi
        return jnp.where(last, qi + 1, qi), jnp.where(last, 0, t + 1), qi, t, qi_1, t_1

    carry = (jnp.int32(0),) * 6
    n_steps = nq * (nq + 1) // 2
    n_lead = n_steps % UNROLL
    for u in range(n_lead):
        carry = step(carry, u % n_buf)

    def steps(it, carry):
        for u in range(UNROLL):
            carry = step(carry, (n_lead + u) % n_buf)
        return carry

    _, _, qi_1, t_1, qi_2, t_2 = lax.fori_loop(0, n_steps // UNROLL, steps, carry)
    weighted_values(qi_2, t_2, (n_steps - 2) % n_buf)
    weighted_values(qi_1, t_1, (n_steps - 1) % n_buf)

    def write_out(qi, acc, l):
        o_ref[qi] = (acc / l).astype(o_ref.dtype)

    def finish(qi, denom_min):
        l = l_ref[qi]
        write_out(qi, acc_ref[qi], l)
        return jnp.minimum(denom_min, l)

    denom_min = lax.fori_loop(0, nq, finish, jnp.full((1, tq), jnp.inf, F32), unroll=4)

    @pl.when(jnp.logical_not(jnp.min(denom_min) >= DENOM_FLOOR))
    def _redo_with_running_max():
        def query_tile(qi, carry):
            def key_tile(t, mla):
                m, l, acc = mla
                s = shifted_scores(qi, t)
                m_new = jnp.maximum(m, jnp.max(s, axis=0, keepdims=True))
                alpha = jnp.exp2(m - m_new)
                p = jnp.exp2(s - m_new)
                l = l * alpha + jnp.sum(p, axis=0, keepdims=True)
                acc = acc * alpha + jnp.dot(vT_ref[t], p.astype(BF16), preferred_element_type=F32)
                return m_new, l, acc

            init = (jnp.full((1, tq), MASK_VALUE, F32), jnp.zeros((1, tq), F32),
                    jnp.zeros((A_VDIM, tq), F32))
            _, l, acc = lax.fori_loop(0, qi + 1, key_tile, init)
            write_out(qi, acc, l)
            return carry

        lax.fori_loop(0, nq, query_tile, 0)


def _mla_attn(qT, kn, kr, vT):
    B, H, nq, _, tq = qT.shape
    S = kn.shape[2]
    assert vT.shape[2:] == (nq, A_VDIM, tq) and tq // CHUNK < A_QPAD - A_QK
    a = np.arange(A_QPAD - A_QK)[:, None]
    qchunk = (np.arange(tq) // CHUNK)[None, :]
    diagonal = np.where((qchunk < a) & (a < tq // CHUNK), MASK_VALUE, 0.0).astype(np.float32)
    mrow = jnp.asarray(np.stack([np.zeros_like(diagonal), diagonal]), BF16)
    return pl.pallas_call(
        functools.partial(_mla_attn_kernel, nq=nq, tq=tq),
        grid=(B, H),
        in_specs=[
            pl.BlockSpec((None, None, nq, A_QPAD, tq), lambda b, h: (b, h, 0, 0, 0)),
            pl.BlockSpec((None, None, S, A_NOPE), lambda b, h: (b, h, 0, 0)),
            pl.BlockSpec((None, S, LANES), lambda b, h: (b, 0, 0)),
            pl.BlockSpec((None, None, nq, A_VDIM, tq), lambda b, h: (b, h, 0, 0, 0)),
            pl.BlockSpec(mrow.shape, lambda b, h: (0, 0, 0)),
        ],
        out_specs=pl.BlockSpec((None, None, nq, A_VDIM, tq), lambda b, h: (b, h, 0, 0, 0)),
        out_shape=jax.ShapeDtypeStruct((B, H, nq, A_VDIM, tq), BF16),
        scratch_shapes=[
            *([pltpu.VMEM((tq, tq), BF16)] * UNROLL),
            pltpu.VMEM((nq, 1, tq), F32),
            pltpu.VMEM((nq, A_VDIM, tq), F32),
        ],
        compiler_params=_cparams(2),
        name="mla_attn",
    )(qT, kn, kr, vT, mrow)


def _gated_out_ln(o_ref, sg_ref, x_ref, wo_ref, g_ref, b_ref, xo_ref):
    half = xo_ref.shape[0] // 2
    for rows in (slice(0, half), slice(half, 2 * half)):
        o = jnp.concatenate(
            [o_ref[hh, :, rows].astype(F32).T for hh in range(o_ref.shape[0])], axis=1)
        y = jnp.dot((o * sg_ref[rows, :]).astype(BF16), wo_ref[...], preferred_element_type=F32)
        xo_ref[rows, :] = _deepnorm_ln(x_ref[rows, :], y, g_ref[...], b_ref[...])


def _out_ln_kv_kernel(o_ref, sg_ref, x_ref, wo_ref, g_ref, b_ref, wk_ref, wvT_ref,
                      xo_ref, ks_ref, vsT_ref):
    _gated_out_ln(o_ref, sg_ref, x_ref, wo_ref, g_ref, b_ref, xo_ref)
    xb = xo_ref[...].astype(BF16)
    ks_ref[...] = jnp.dot(xb, wk_ref[...], preferred_element_type=F32).astype(BF16)
    vT = lax.dot_general(wvT_ref[...], xb, _NT, preferred_element_type=F32).astype(BF16)
    ones = (lax.broadcasted_iota(jnp.int32, (B_VROWS - B_HEAD_DIM, vT.shape[1]), 0) == 0).astype(BF16)
    vT = jnp.concatenate(
        [part for g in range(B_KV_HEADS) for part in (vT[g * B_HEAD_DIM:(g + 1) * B_HEAD_DIM], ones)],
        axis=0)
    for n in range(vsT_ref.shape[0]):
        vsT_ref[n] = vT[:, n * Q_BLOCK:(n + 1) * Q_BLOCK]


def _out_ln_kv(o, sg, x, wo, g, b, wk, wvT, tm):
    B, S, _ = x.shape
    row = pl.BlockSpec((None, tm, D_MODEL), lambda b_, i: (b_, i, 0))
    const = lambda *shape: pl.BlockSpec(shape, lambda b_, i: (0,) * len(shape))
    heads = pl.BlockSpec((None, A_HEADS, None, A_VDIM, tm), lambda b_, i: (b_, 0, i, 0, 0))
    nb = tm // Q_BLOCK
    return pl.pallas_call(
        _out_ln_kv_kernel, grid=(B, S // tm),
        in_specs=[heads, row, row, const(*wo.shape), const(*g.shape), const(*b.shape),
                  const(*wk.shape), const(*wvT.shape)],
        out_specs=[
            row,
            pl.BlockSpec((None, tm, B_KV_WIDTH), lambda b_, i: (b_, i, 0)),
            pl.BlockSpec((None, nb, B_KV_HEADS * B_VROWS, Q_BLOCK), lambda b_, i: (b_, i, 0, 0)),
        ],
        out_shape=[
            jax.ShapeDtypeStruct(x.shape, F32),
            jax.ShapeDtypeStruct((B, S, B_KV_WIDTH), BF16),
            jax.ShapeDtypeStruct((B, S // Q_BLOCK, B_KV_HEADS * B_VROWS, Q_BLOCK), BF16),
        ],
        compiler_params=_cparams(2), name="out_ln_kv",
    )(o, sg, x, wo, g, b, wk, wvT)


def _band_bias_kernel(tab_ref, bkt_ref, ok_ref, out_ref):
    bkt = bkt_ref[...]
    ok = ok_ref[...] != 0
    has_prev = lax.broadcasted_iota(jnp.int32, bkt.shape, 0) >= Q_BLOCK
    for h in range(B_Q_HEADS):
        acc = jnp.zeros(bkt.shape, F32)
        for bucket in range(NUM_BUCKETS):
            acc = jnp.where(bkt == bucket, tab_ref[bucket, h], acc)
        generic = jnp.where(ok, acc * LOG2_E, MASK_VALUE)
        g, hi = divmod(h, B_GROUP)
        out_ref[0, g, :, hi * Q_BLOCK:(hi + 1) * Q_BLOCK] = jnp.where(has_prev, generic, MASK_VALUE)
        out_ref[1, g, :, hi * Q_BLOCK:(hi + 1) * Q_BLOCK] = generic


def _t5_bucket_map(rel):
    half = NUM_BUCKETS // 2
    ret = jnp.where(rel > 0, half, 0)
    n = jnp.abs(rel)
    max_exact = half // 2
    large = max_exact + (jnp.log(jnp.maximum(n, 1).astype(F32) / max_exact)
                         / math.log(MAX_DISTANCE / max_exact) * (half - max_exact)).astype(jnp.int32)
    large = jnp.minimum(large, half - 1)
    return ret + jnp.where(n < max_exact, n, large)


def _band_bias(rel_bias_table):
    j = jnp.arange(2 * Q_BLOCK)[:, None]
    r = jnp.arange(Q_BLOCK)[None, :]
    rel = j - Q_BLOCK - r
    d = (r // CHUNK) - ((j - Q_BLOCK) // CHUNK)
    ok = ((d >= 0) & (d <= WIN_CHUNKS)).astype(jnp.int32)
    bkt = _t5_bucket_map(rel).astype(jnp.int32)
    return pl.pallas_call(
        _band_bias_kernel,
        in_specs=[
            pl.BlockSpec(memory_space=pltpu.SMEM),
            pl.BlockSpec(memory_space=pltpu.VMEM),
            pl.BlockSpec(memory_space=pltpu.VMEM),
        ],
        out_specs=pl.BlockSpec(memory_space=pltpu.VMEM),
        out_shape=jax.ShapeDtypeStruct((2, B_KV_HEADS, 2 * Q_BLOCK, B_GROUP * Q_BLOCK), F32),
        name="band_bias",
    )(rel_bias_table, bkt, ok)


def _swa_layer_kernel(xn_ref, x_ref, wqT_ref, wg_ref, wo_ref, ks_ref, vsT_ref, bias_ref, sink_ref,
                      g_ref, b_ref, xo_ref, qT0_ref, sg0_ref, qT1_ref, sg1_ref, og_ref, s_ref, *, nblk):
    step = pl.program_id(1)

    @pl.when((pl.program_id(0) == 0) & (step == 0))
    def _first_step():
        qT0_ref[...] = jnp.zeros(qT0_ref.shape, qT0_ref.dtype)
        sg0_ref[...] = jnp.zeros(sg0_ref.shape, sg0_ref.dtype)

    def body(qT_ref, sg_ref, qT_next_ref, sg_next_ref):
        tile = jnp.maximum(step - 1, 0)
        xb_next = xn_ref[...].astype(BF16)
        zeros = jnp.zeros((B_HEAD_DIM, B_GROUP * Q_BLOCK), BF16)
        chunk = B_WIDTH // nblk
        half = max(nblk // 2, 1)

        for n in range(nblk):
            blk = tile * nblk + n
            cur = pl.multiple_of(blk * Q_BLOCK, Q_BLOCK)
            prev_blk = jnp.maximum(blk - 1, 0)
            prev = pl.multiple_of(prev_blk * Q_BLOCK, Q_BLOCK)
            kband = jnp.concatenate(
                [ks_ref[pl.ds(prev, Q_BLOCK), :], ks_ref[pl.ds(cur, Q_BLOCK), :]], axis=0)
            vband = jnp.concatenate([vsT_ref[prev_blk], vsT_ref[blk]], axis=1)
            table = jnp.minimum(blk, 1)
            lanes = slice(n * Q_BLOCK, (n + 1) * Q_BLOCK)
            score_max = []
            for g in range(B_KV_HEADS):
                qg = jnp.concatenate(
                    [qT_ref[(g * B_GROUP + hi) * B_HEAD_DIM:(g * B_GROUP + hi + 1) * B_HEAD_DIM, lanes]
                     for hi in range(B_GROUP)], axis=1)
                rhs = jnp.concatenate([qg, zeros] if g == 0 else [zeros, qg], axis=0)
                sT = jnp.dot(kband, rhs, preferred_element_type=F32) + bias_ref[table, g]
                s_ref[g] = sT
                score_max.append(jnp.max(sT, axis=0, keepdims=True))

            part = slice(n * chunk, (n + 1) * chunk)
            qT_next_ref[part, :] = (
                lax.dot_general(wqT_ref[part, :], xb_next, _NT, preferred_element_type=F32)
                * (B_HEAD_DIM ** -0.5 * LOG2_E)).astype(BF16)
            gate = jnp.dot(xb_next, wg_ref[:, part], preferred_element_type=F32)
            sg_next_ref[:, part] = gate * _sigmoid(gate)

            o_parts = []
            for g in range(B_KV_HEADS):
                sink = sink_ref[g]
                m = jnp.maximum(score_max[g], sink)
                e = jnp.exp2(s_ref[g] - m).astype(BF16)
                v_rows = slice(g * B_VROWS, (g + 1) * B_VROWS)
                oT = jnp.dot(vband[v_rows, :], e, preferred_element_type=F32)
                den = oT[B_HEAD_DIM:B_HEAD_DIM + 1] + jnp.exp2(sink - m)
                o_parts.append(oT[:B_HEAD_DIM] / den)
            for hi in range(B_GROUP):
                cols = slice(hi * Q_BLOCK, (hi + 1) * Q_BLOCK)
                o_hi = jnp.concatenate([o_parts[0][:, cols], o_parts[1][:, cols]], axis=0).T
                og_ref[lanes, cols] = (o_hi * sg_ref[lanes, cols]).astype(BF16)

            if (n + 1) % half == 0:
                rows = slice((n + 1 - half) * Q_BLOCK, (n + 1) * Q_BLOCK)
                y = jnp.dot(og_ref[rows, :], wo_ref[...], preferred_element_type=F32)
                xo_ref[rows, :] = _deepnorm_ln(x_ref[rows, :], y, g_ref[...], b_ref[...])

    @pl.when(step % 2 == 0)
    def _even():
        body(qT0_ref, sg0_ref, qT1_ref, sg1_ref)

    @pl.when(step % 2 == 1)
    def _odd():
        body(qT1_ref, sg1_ref, qT0_ref, sg0_ref)


def _swa_layer(x, wqT, wg, wo, ks, vsT, bias, sink, g, b, tm):
    B, S, _ = x.shape
    nt = S // tm
    nblk = tm // Q_BLOCK
    once = pl.Buffered(1)
    const = lambda *shape: pl.BlockSpec(shape, lambda b_, i: (0,) * len(shape), pipeline_mode=once)
    done = pl.BlockSpec((None, tm, D_MODEL), lambda b_, i: (b_, jnp.maximum(i - 1, 0), 0))
    return pl.pallas_call(
        functools.partial(_swa_layer_kernel, nblk=nblk),
        grid=(B, nt + 1),
        in_specs=[
            pl.BlockSpec((None, tm, D_MODEL), lambda b_, i: (b_, jnp.minimum(i, nt - 1), 0)),
            done, const(*wqT.shape), const(*wg.shape), const(*wo.shape),
            pl.BlockSpec((None, S, B_KV_WIDTH), lambda b_, i: (b_, 0, 0), pipeline_mode=once),
            pl.BlockSpec((None, S // Q_BLOCK, B_KV_HEADS * B_VROWS, Q_BLOCK), lambda b_, i: (b_, 0, 0, 0),
                         pipeline_mode=once),
            const(*bias.shape), const(*sink.shape), const(*g.shape), const(*b.shape),
        ],
        out_specs=done,
        out_shape=jax.ShapeDtypeStruct(x.shape, F32),
        scratch_shapes=[
            pltpu.VMEM((B_WIDTH, tm), BF16), pltpu.VMEM((tm, B_WIDTH), F32),
            pltpu.VMEM((B_WIDTH, tm), BF16), pltpu.VMEM((tm, B_WIDTH), F32),
            pltpu.VMEM((tm, B_WIDTH), BF16),
            pltpu.VMEM((B_KV_HEADS, 2 * Q_BLOCK, B_GROUP * Q_BLOCK), F32),
        ],
        compiler_params=_cparams(2),
        name="swa_layer",
    )(x, x, wqT, wg, wo, ks, vsT, bias, sink, g, b)


def _rope_tables(S):
    inv = ROPE_THETA ** (-jnp.arange(ROPE_HALF, dtype=F32) / ROPE_HALF)
    ang = jnp.arange(S, dtype=F32)[:, None] * inv[None, :]
    cos, sin = jnp.cos(ang), jnp.sin(ang)
    zeros = jnp.zeros((S, LANES - A_ROPE), F32)
    z32 = jnp.zeros((S, ROPE_HALF), F32)
    cosk = jnp.concatenate([cos, cos, zeros], axis=1)
    sinka = jnp.concatenate([-sin, z32, zeros], axis=1)
    sinkb = jnp.concatenate([z32, sin, zeros], axis=1)
    return cos.T, sin.T, cosk, sinka, sinkb


def _mla_weights(w_in, w_uq, w_ukv):
    o_kr = A_Q_LORA + A_KV_LORA
    w_in_pad = jnp.concatenate(
        [w_in[:, :o_kr + A_ROPE], jnp.zeros((D_MODEL, LANES - A_ROPE), w_in.dtype),
         w_in[:, o_kr + A_ROPE:]], axis=1).astype(BF16)
    w_ukv_h = w_ukv.reshape(A_KV_LORA, A_HEADS, A_NOPE + A_VDIM)
    w_kn = w_ukv_h[:, :, :A_NOPE].reshape(A_KV_LORA, A_HEADS * A_NOPE)
    w_v = w_ukv_h[:, :, A_NOPE:].reshape(A_KV_LORA, A_WIDTH)
    return w_in_pad, w_uq.T.astype(BF16), w_kn.astype(BF16), w_v.T.astype(BF16)


def _swa_feature_order(w, axis):
    shape = w.shape
    split = shape[:axis] + (B_KV_HEADS, B_GROUP, B_HEAD_DIM) + shape[axis + 1:]
    return jnp.swapaxes(w.reshape(split), axis, axis + 1).reshape(shape)


def kernel(x, w_in_a, q_norm_a, kv_norm_a, w_uq_a, w_ukv_a, w_o_a, w_kv_shared, w_in_b, sinks_b,
           w_o_b, rel_bias_table, ln_gain, ln_bias):
    B, S, _ = x.shape
    tm = min(512, S)
    tq = min(512, S)
    rope = _rope_tables(S)

    prev = None
    for i in range(N_A_LAYERS):
        w_in, w_uqT, w_kn, w_vT = _mla_weights(w_in_a[i], w_uq_a[i], w_ukv_a[i])
        outs = _mla_proj(x, w_in, q_norm_a[i][None, :], kv_norm_a[i][None, :], w_uqT, w_kn, w_vT,
                         rope, tm, tq, tq, prev)
        if prev is not None:
            x, *outs = outs
        qT, kn, kr, vT, sg = outs
        o = _mla_attn(qT, kn, kr, vT)
        prev = (o, sg, w_o_a[i].astype(BF16), ln_gain[i][None, :], ln_bias[i][None, :])
    o, sg, wo, g, b = prev
    x, ks, vsT = _out_ln_kv(o, sg, x, wo, g, b, w_kv_shared[:, :B_KV_WIDTH].astype(BF16),
                            w_kv_shared[:, B_KV_WIDTH:].T.astype(BF16), tm)

    bias = _band_bias(rel_bias_table)
    for j in range(DEPTH - N_A_LAYERS):
        layer = N_A_LAYERS + j
        wqT = w_in_b[j][:, :B_WIDTH].T.astype(BF16)
        wg = _swa_feature_order(w_in_b[j][:, B_WIDTH:], 1).astype(BF16)
        wo = _swa_feature_order(w_o_b[j], 0).astype(BF16)
        sink = jnp.repeat(sinks_b[j].astype(F32).reshape(B_KV_HEADS, 1, B_GROUP) * LOG2_E, Q_BLOCK, axis=2)
        x = _swa_layer(x, wqT, wg, wo, ks, vsT, bias, sink,
                       ln_gain[layer][None, :], ln_bias[layer][None, :], min(512, S))
    return x
```

```python
import functools
import math

import jax
import jax.numpy as jnp
import numpy as np
from jax import lax
from jax.experimental import pallas as pl
from jax.experimental.pallas import tpu as pltpu

F32 = jnp.float32
BF16 = jnp.bfloat16

D_MODEL = 1024
DEPTH = 4
CHUNK = 64
Q_BLOCK = 128
N_A_LAYERS = DEPTH // 2
A_HEADS = 8
A_NOPE = 128
A_ROPE = 64
A_VDIM = 128
A_QK = A_NOPE + A_ROPE
A_Q_LORA = 384
A_KV_LORA = 256
A_WIDTH = A_HEADS * A_VDIM
A_QPAD = 256
ROPE_THETA = 10000.0
ROPE_HALF = A_ROPE // 2
B_Q_HEADS = 16
B_KV_HEADS = 2
B_GROUP = B_Q_HEADS // B_KV_HEADS
B_HEAD_DIM = 64
B_WIDTH = B_Q_HEADS * B_HEAD_DIM
B_KV_WIDTH = B_KV_HEADS * B_HEAD_DIM
B_VROWS = B_HEAD_DIM + 16
WINDOW = 128
WIN_CHUNKS = WINDOW // CHUNK
NUM_BUCKETS = 32
MAX_DISTANCE = 128
DEEPNORM_ALPHA = (2 * DEPTH) ** 0.25
NORM_EPS = 1e-5
MASK_VALUE = -1e30
LOG2_E = math.log2(math.e)

LANES = 128
VMEM_LIMIT_BYTES = 56 * 1024 * 1024

_NT = (((1,), (1,)), ((), ()))
UNROLL = 34
BOUND_MARGIN = 1.0 + 2.0 ** -6
DENOM_FLOOR = 2.0 ** -80


def _cparams(n_axes, flags=None):
    return pltpu.CompilerParams(
        dimension_semantics=("arbitrary",) * n_axes, vmem_limit_bytes=VMEM_LIMIT_BYTES, flags=flags)


def _sigmoid(x):
    return 1.0 / (1.0 + jnp.exp(-x))


def _rms_scale(c, g):
    return c * lax.rsqrt(jnp.mean(c * c, axis=-1, keepdims=True) + NORM_EPS) * g


def _deepnorm_ln(x, y, g, b):
    z = DEEPNORM_ALPHA * x + y
    mu = jnp.mean(z, axis=-1, keepdims=True)
    zc = z - mu
    var = jnp.mean(zc * zc, axis=-1, keepdims=True)
    return zc * lax.rsqrt(var + NORM_EPS) * g + b


def _mla_proj_kernel(*refs, after_layer):
    if after_layer:
        o_ref, sg_prev_ref, x_prev_ref, wo_ref, g_ref, b_ref, *refs = refs
        *refs, xo_ref, qT_ref, kn_ref, kr_ref, vT_ref, sg_ref, kmax_ref = refs
    else:
        x_ref, *refs = refs
        *refs, qT_ref, kn_ref, kr_ref, vT_ref, sg_ref, kmax_ref = refs
    (w_in_ref, gq_ref, gkv_ref, w_uqT_ref, w_kn_ref, w_vT_ref,
     cosT_ref, sinT_ref, cosk_ref, sinka_ref, sinkb_ref, ktail_ref) = refs

    @pl.when(pl.program_id(1) == 0)
    def _start_of_sequence():
        kmax_ref[...] = jnp.zeros(kmax_ref.shape, F32)

    if after_layer:
        _gated_out_ln(o_ref, sg_prev_ref, x_prev_ref, wo_ref, g_ref, b_ref, xo_ref)
        x_ref = xo_ref
    xb = x_ref[...].astype(BF16)
    h = jnp.dot(xb, w_in_ref[...], preferred_element_type=F32)
    o_kv = A_Q_LORA
    o_kr = A_Q_LORA + A_KV_LORA
    o_g = o_kr + LANES
    cq = _rms_scale(h[:, :o_kv], gq_ref[...]).astype(BF16)
    ckv = _rms_scale(h[:, o_kv:o_kr], gkv_ref[...]).astype(BF16)

    scale = A_QK ** -0.5 * math.log2(math.e)
    qT = lax.dot_general(w_uqT_ref[...], cq, _NT, preferred_element_type=F32) * scale

    krs = h[:, o_kr:o_g]
    kr = (krs * cosk_ref[...]
          + pltpu.roll(krs, LANES - ROPE_HALF, axis=1) * sinka_ref[...]
          + pltpu.roll(krs, ROPE_HALF, axis=1) * sinkb_ref[...])
    kr_ref[...] = (kr + ktail_ref[...]).astype(BF16)
    kn = jnp.dot(ckv, w_kn_ref[...], preferred_element_type=F32)

    kr2 = jnp.sum(kr * kr, axis=1, keepdims=True)
    lane = lax.broadcasted_iota(jnp.int32, kmax_ref.shape, 1)
    kmax2 = kmax_ref[...]
    for hh in range(A_HEADS):
        knh = kn[:, hh * A_NOPE:(hh + 1) * A_NOPE]
        k2 = jnp.max(jnp.sum(knh * knh, axis=1, keepdims=True) + kr2, axis=0, keepdims=True)
        kmax2 = jnp.where(lane == hh, jnp.maximum(kmax2, k2), kmax2)
    kmax_ref[...] = kmax2

    cos = cosT_ref[...]
    sin = sinT_ref[...]
    n_r1 = A_NOPE
    n_r2 = n_r1 + ROPE_HALF
    n_r3 = n_r2 + ROPE_HALF
    tail_row = lax.broadcasted_iota(jnp.int32, (A_QPAD - A_QK, qT.shape[1]), 0)
    bound_row = vT_ref.shape[-1] // CHUNK
    for hh in range(A_HEADS):
        r0 = hh * A_QK
        t1 = qT[r0 + n_r1:r0 + n_r2]
        t2 = qT[r0 + n_r2:r0 + n_r3]
        qT_ref[hh, :n_r1, :] = qT[r0:r0 + n_r1].astype(BF16)
        qT_ref[hh, n_r1:n_r2, :] = (t1 * cos - t2 * sin).astype(BF16)
        qT_ref[hh, n_r2:n_r3, :] = (t2 * cos + t1 * sin).astype(BF16)
        qh = qT[r0:r0 + A_QK]
        q2 = jnp.sum(qh * qh, axis=0, keepdims=True)
        bound = jnp.sqrt(q2 * kmax2[:, hh:hh + 1]) * BOUND_MARGIN
        qT_ref[hh, n_r3:, :] = jnp.where(tail_row == bound_row, -bound, 0.0).astype(BF16)

    kn = kn.astype(BF16)
    vT = lax.dot_general(w_vT_ref[...], ckv, _NT, preferred_element_type=F32).astype(BF16)
    tk = vT_ref.shape[-1]
    for hh in range(A_HEADS):
        kn_ref[hh] = kn[:, hh * A_NOPE:(hh + 1) * A_NOPE]
        for n in range(vT_ref.shape[1]):
            vT_ref[hh, n] = vT[hh * A_VDIM:(hh + 1) * A_VDIM, n * tk:(n + 1) * tk]

    gate = h[:, o_g:]
    sg_ref[...] = (gate * _sigmoid(gate)).astype(BF16)


def _layer_block(stacked, layer):
    zeros = (0,) * (stacked.ndim - 1)
    return pl.BlockSpec((None,) + stacked.shape[1:], lambda *_: (layer,) + zeros,
                        pipeline_mode=pl.Buffered(1))


def _mla_proj(x, layer, params, rope, tm, tq, tk, prev=None):
    B, S, _ = x.shape
    nt = S // tm
    cosT, sinT, cosk, sinka, sinkb = rope
    weights = [params[k] for k in ("w_in", "gq", "gkv", "w_uqT", "w_kn", "w_vT")]
    key = np.arange(S)[:, None]
    lane = np.arange(LANES)[None, :] - A_ROPE
    ktail = jnp.asarray((lane == (key % tk) // CHUNK) | (lane == tk // CHUNK), F32)
    tok_lanes = pl.BlockSpec((tm, LANES), lambda b, i: (i, 0))
    row = pl.BlockSpec((None, tm, D_MODEL), lambda b, i: (b, i, 0))
    in_specs = [_layer_block(w, layer) for w in weights] + [
        pl.BlockSpec((ROPE_HALF, tm), lambda b, i: (0, i)),
        pl.BlockSpec((ROPE_HALF, tm), lambda b, i: (0, i)),
        tok_lanes, tok_lanes, tok_lanes, tok_lanes,
    ]
    operands = (*weights, cosT, sinT, cosk, sinka, sinkb, ktail)
    out_specs = [
        pl.BlockSpec((None, A_HEADS, None, A_QPAD, tm),
                     lambda b, i: (b, 0, i // (tq // tm), 0, i % (tq // tm))),
        pl.BlockSpec((None, A_HEADS, tm, A_NOPE), lambda b, i: (b, 0, i, 0)),
        pl.BlockSpec((None, tm, LANES), lambda b, i: (b, i, 0)),
        pl.BlockSpec((None, A_HEADS, tm // tk, A_VDIM, tk), lambda b, i: (b, 0, i, 0, 0)),
        pl.BlockSpec((None, tm, A_WIDTH), lambda b, i: (b, i, 0)),
    ]
    out_shape = [
        jax.ShapeDtypeStruct((B, A_HEADS, S // tq, A_QPAD, tq), BF16),
        jax.ShapeDtypeStruct((B, A_HEADS, S, A_NOPE), BF16),
        jax.ShapeDtypeStruct((B, S, LANES), BF16),
        jax.ShapeDtypeStruct((B, A_HEADS, S // tk, A_VDIM, tk), BF16),
        jax.ShapeDtypeStruct((B, S, A_WIDTH), BF16),
    ]
    if prev is None:
        in_specs = [row] + in_specs
        operands = (x,) + operands
    else:
        o, sg = prev
        tail = [params[k] for k in ("w_o", "ln_g", "ln_b")]
        heads = pl.BlockSpec((None, A_HEADS, None, A_VDIM, tm), lambda b_, i: (b_, 0, i, 0, 0))
        in_specs = [heads, row, row] + [_layer_block(w, layer - 1) for w in tail] + in_specs
        operands = (o, sg, x, *tail) + operands
        out_specs = [row] + out_specs
        out_shape = [jax.ShapeDtypeStruct(x.shape, F32)] + out_shape
    return pl.pallas_call(
        functools.partial(_mla_proj_kernel, after_layer=prev is not None),
        grid=(B, nt),
        in_specs=in_specs,
        out_specs=out_specs,
        out_shape=out_shape,
        scratch_shapes=[pltpu.VMEM((1, LANES), F32)],
        compiler_params=_cparams(2),
        name="mla_proj" if prev is None else "out_ln_mla_proj",
    )(*operands)


def _mla_attn_kernel(qT_ref, kn_ref, kr_ref, vT_ref, mrow_ref, o_ref, *scratch, nq, tq):
    *p_refs, l_ref, acc_ref = scratch
    n_buf = len(p_refs)

    def shifted_scores(qi, t):
        k0 = pl.multiple_of(t * tq, tq)
        kcat = jnp.concatenate([kn_ref[pl.ds(k0, tq), :], kr_ref[pl.ds(k0, tq), :]], axis=1)
        tail = qT_ref[qi, A_QK:, :] + mrow_ref[jnp.where(t == qi, 1, 0)]
        q_full = jnp.concatenate([qT_ref[qi, :A_QK, :], tail], axis=0)
        return jnp.dot(kcat, q_full, preferred_element_type=F32)

    def weighted_values(qi, t, buf):
        acc_ref[qi] += jnp.dot(vT_ref[t], p_refs[buf][...], preferred_element_type=F32)

    acc_ref[...] = jnp.zeros(acc_ref.shape, F32)
    l_ref[...] = jnp.zeros(l_ref.shape, F32)
    for p_ref in p_refs[-2:]:
        p_ref[...] = jnp.zeros(p_ref.shape, BF16)

    def step(carry, buf):
        qi, t, qi_1, t_1, qi_2, t_2 = carry
        weighted_values(qi_2, t_2, (buf - 2) % n_buf)
        p = jnp.exp2(shifted_scores(qi, t))
        l_ref[qi] += jnp.sum(p, axis=0, keepdims=True)
        p_refs[buf][...] = p.astype(BF16)
        last = t == qi
        return jnp.where(last, qi + 1, qi), jnp.where(last, 0, t + 1), qi, t, qi_1, t_1

    carry = (jnp.int32(0),) * 6
    n_steps = nq * (nq + 1) // 2
    n_lead = n_steps % UNROLL
    for u in range(n_lead):
        carry = step(carry, u % n_buf)

    def steps(it, carry):
        for u in range(UNROLL):
            carry = step(carry, (n_lead + u) % n_buf)
        return carry

    _, _, qi_1, t_1, qi_2, t_2 = lax.fori_loop(0, n_steps // UNROLL, steps, carry)
    weighted_values(qi_2, t_2, (n_steps - 2) % n_buf)
    weighted_values(qi_1, t_1, (n_steps - 1) % n_buf)

    def write_out(qi, acc, l):
        o_ref[qi] = (acc / l).astype(o_ref.dtype)

    def finish(qi, denom_min):
        l = l_ref[qi]
        write_out(qi, acc_ref[qi], l)
        return jnp.minimum(denom_min, l)

    denom_min = lax.fori_loop(0, nq, finish, jnp.full((1, tq), jnp.inf, F32), unroll=4)

    @pl.when(jnp.logical_not(jnp.min(denom_min) >= DENOM_FLOOR))
    def _redo_with_running_max():
        def query_tile(qi, carry):
            def key_tile(t, mla):
                m, l, acc = mla
                s = shifted_scores(qi, t)
                m_new = jnp.maximum(m, jnp.max(s, axis=0, keepdims=True))
                alpha = jnp.exp2(m - m_new)
                p = jnp.exp2(s - m_new)
                l = l * alpha + jnp.sum(p, axis=0, keepdims=True)
                acc = acc * alpha + jnp.dot(vT_ref[t], p.astype(BF16), preferred_element_type=F32)
                return m_new, l, acc

            init = (jnp.full((1, tq), MASK_VALUE, F32), jnp.zeros((1, tq), F32),
                    jnp.zeros((A_VDIM, tq), F32))
            _, l, acc = lax.fori_loop(0, qi + 1, key_tile, init)
            write_out(qi, acc, l)
            return carry

        lax.fori_loop(0, nq, query_tile, 0)


def _mla_attn(qT, kn, kr, vT):
    B, H, nq, _, tq = qT.shape
    S = kn.shape[2]
    assert vT.shape[2:] == (nq, A_VDIM, tq) and tq // CHUNK < A_QPAD - A_QK
    a = np.arange(A_QPAD - A_QK)[:, None]
    qchunk = (np.arange(tq) // CHUNK)[None, :]
    diagonal = np.where((qchunk < a) & (a < tq // CHUNK), MASK_VALUE, 0.0).astype(np.float32)
    mrow = jnp.asarray(np.stack([np.zeros_like(diagonal), diagonal]), BF16)
    return pl.pallas_call(
        functools.partial(_mla_attn_kernel, nq=nq, tq=tq),
        grid=(B, H),
        in_specs=[
            pl.BlockSpec((None, None, nq, A_QPAD, tq), lambda b, h: (b, h, 0, 0, 0)),
            pl.BlockSpec((None, None, S, A_NOPE), lambda b, h: (b, h, 0, 0)),
            pl.BlockSpec((None, S, LANES), lambda b, h: (b, 0, 0)),
            pl.BlockSpec((None, None, nq, A_VDIM, tq), lambda b, h: (b, h, 0, 0, 0)),
            pl.BlockSpec(mrow.shape, lambda b, h: (0, 0, 0)),
        ],
        out_specs=pl.BlockSpec((None, None, nq, A_VDIM, tq), lambda b, h: (b, h, 0, 0, 0)),
        out_shape=jax.ShapeDtypeStruct((B, H, nq, A_VDIM, tq), BF16),
        scratch_shapes=[
            *([pltpu.VMEM((tq, tq), BF16)] * UNROLL),
            pltpu.VMEM((nq, 1, tq), F32),
            pltpu.VMEM((nq, A_VDIM, tq), F32),
        ],
        compiler_params=_cparams(2),
        name="mla_attn",
    )(qT, kn, kr, vT, mrow)


def _gated_out_ln(o_ref, sg_ref, x_ref, wo_ref, g_ref, b_ref, xo_ref):
    half = xo_ref.shape[0] // 2
    for rows in (slice(0, half), slice(half, 2 * half)):
        o = jnp.concatenate(
            [o_ref[hh, :, rows].astype(F32).T for hh in range(o_ref.shape[0])], axis=1)
        y = jnp.dot((o * sg_ref[rows, :]).astype(BF16), wo_ref[...], preferred_element_type=F32)
        xo_ref[rows, :] = _deepnorm_ln(x_ref[rows, :], y, g_ref[...], b_ref[...])


def _out_ln_kv_kernel(o_ref, sg_ref, x_ref, wo_ref, g_ref, b_ref, wk_ref, wvT_ref,
                      xo_ref, ks_ref, vsT_ref):
    _gated_out_ln(o_ref, sg_ref, x_ref, wo_ref, g_ref, b_ref, xo_ref)
    xb = xo_ref[...].astype(BF16)
    ks_ref[...] = jnp.dot(xb, wk_ref[...], preferred_element_type=F32).astype(BF16)
    vT = lax.dot_general(wvT_ref[...], xb, _NT, preferred_element_type=F32).astype(BF16)
    ones = (lax.broadcasted_iota(jnp.int32, (B_VROWS - B_HEAD_DIM, vT.shape[1]), 0) == 0).astype(BF16)
    vT = jnp.concatenate(
        [part for g in range(B_KV_HEADS) for part in (vT[g * B_HEAD_DIM:(g + 1) * B_HEAD_DIM], ones)],
        axis=0)
    for n in range(vsT_ref.shape[0]):
        vsT_ref[n] = vT[:, n * Q_BLOCK:(n + 1) * Q_BLOCK]


def _out_ln_kv(o, sg, x, layer, params, wk, wvT, tm):
    B, S, _ = x.shape
    row = pl.BlockSpec((None, tm, D_MODEL), lambda b_, i: (b_, i, 0))
    const = lambda *shape: pl.BlockSpec(shape, lambda b_, i: (0,) * len(shape))
    heads = pl.BlockSpec((None, A_HEADS, None, A_VDIM, tm), lambda b_, i: (b_, 0, i, 0, 0))
    tail = [params[k] for k in ("w_o", "ln_g", "ln_b")]
    nb = tm // Q_BLOCK
    return pl.pallas_call(
        _out_ln_kv_kernel, grid=(B, S // tm),
        in_specs=[heads, row, row] + [_layer_block(w, layer) for w in tail]
                 + [const(*wk.shape), const(*wvT.shape)],
        out_specs=[
            row,
            pl.BlockSpec((None, tm, B_KV_WIDTH), lambda b_, i: (b_, i, 0)),
            pl.BlockSpec((None, nb, B_KV_HEADS * B_VROWS, Q_BLOCK), lambda b_, i: (b_, i, 0, 0)),
        ],
        out_shape=[
            jax.ShapeDtypeStruct(x.shape, F32),
            jax.ShapeDtypeStruct((B, S, B_KV_WIDTH), BF16),
            jax.ShapeDtypeStruct((B, S // Q_BLOCK, B_KV_HEADS * B_VROWS, Q_BLOCK), BF16),
        ],
        compiler_params=_cparams(2), name="out_ln_kv",
    )(o, sg, x, *tail, wk, wvT)


def _band_bias_kernel(tab_ref, bkt_ref, ok_ref, out_ref):
    bkt = bkt_ref[...]
    ok = ok_ref[...] != 0
    has_prev = lax.broadcasted_iota(jnp.int32, bkt.shape, 0) >= Q_BLOCK
    for h in range(B_Q_HEADS):
        acc = jnp.zeros(bkt.shape, F32)
        for bucket in range(NUM_BUCKETS):
            acc = jnp.where(bkt == bucket, tab_ref[bucket, h], acc)
        generic = jnp.where(ok, acc * LOG2_E, MASK_VALUE)
        g, hi = divmod(h, B_GROUP)
        out_ref[0, g, :, hi * Q_BLOCK:(hi + 1) * Q_BLOCK] = jnp.where(has_prev, generic, MASK_VALUE)
        out_ref[1, g, :, hi * Q_BLOCK:(hi + 1) * Q_BLOCK] = generic


def _t5_bucket_map(rel):
    half = NUM_BUCKETS // 2
    ret = jnp.where(rel > 0, half, 0)
    n = jnp.abs(rel)
    max_exact = half // 2
    large = max_exact + (jnp.log(jnp.maximum(n, 1).astype(F32) / max_exact)
                         / math.log(MAX_DISTANCE / max_exact) * (half - max_exact)).astype(jnp.int32)
    large = jnp.minimum(large, half - 1)
    return ret + jnp.where(n < max_exact, n, large)


def _band_bias(rel_bias_table):
    j = jnp.arange(2 * Q_BLOCK)[:, None]
    r = jnp.arange(Q_BLOCK)[None, :]
    rel = j - Q_BLOCK - r
    d = (r // CHUNK) - ((j - Q_BLOCK) // CHUNK)
    ok = ((d >= 0) & (d <= WIN_CHUNKS)).astype(jnp.int32)
    bkt = _t5_bucket_map(rel).astype(jnp.int32)
    return pl.pallas_call(
        _band_bias_kernel,
        in_specs=[
            pl.BlockSpec(memory_space=pltpu.SMEM),
            pl.BlockSpec(memory_space=pltpu.VMEM),
            pl.BlockSpec(memory_space=pltpu.VMEM),
        ],
        out_specs=pl.BlockSpec(memory_space=pltpu.VMEM),
        out_shape=jax.ShapeDtypeStruct((2, B_KV_HEADS, 2 * Q_BLOCK, B_GROUP * Q_BLOCK), F32),
        name="band_bias",
    )(rel_bias_table, bkt, ok)


def _swa_layer_kernel(xn_ref, x_ref, wqT_ref, wg_ref, wo_ref, ks_ref, vsT_ref, bias_ref, sink_ref,
                      g_ref, b_ref, xo_ref, qT0_ref, sg0_ref, qT1_ref, sg1_ref, og_ref, s_ref, *, nblk):
    step = pl.program_id(1)

    @pl.when((pl.program_id(0) == 0) & (step == 0))
    def _first_step():
        qT0_ref[...] = jnp.zeros(qT0_ref.shape, qT0_ref.dtype)
        sg0_ref[...] = jnp.zeros(sg0_ref.shape, sg0_ref.dtype)

    def body(qT_ref, sg_ref, qT_next_ref, sg_next_ref):
        tile = jnp.maximum(step - 1, 0)
        xb_next = xn_ref[...].astype(BF16)
        zeros = jnp.zeros((B_HEAD_DIM, B_GROUP * Q_BLOCK), BF16)
        chunk = B_WIDTH // nblk
        half = max(nblk // 2, 1)

        for n in range(nblk):
            blk = tile * nblk + n
            cur = pl.multiple_of(blk * Q_BLOCK, Q_BLOCK)
            prev_blk = jnp.maximum(blk - 1, 0)
            prev = pl.multiple_of(prev_blk * Q_BLOCK, Q_BLOCK)
            kband = jnp.concatenate(
                [ks_ref[pl.ds(prev, Q_BLOCK), :], ks_ref[pl.ds(cur, Q_BLOCK), :]], axis=0)
            vband = jnp.concatenate([vsT_ref[prev_blk], vsT_ref[blk]], axis=1)
            table = jnp.minimum(blk, 1)
            lanes = slice(n * Q_BLOCK, (n + 1) * Q_BLOCK)
            score_max = []
            for g in range(B_KV_HEADS):
                qg = jnp.concatenate(
                    [qT_ref[(g * B_GROUP + hi) * B_HEAD_DIM:(g * B_GROUP + hi + 1) * B_HEAD_DIM, lanes]
                     for hi in range(B_GROUP)], axis=1)
                rhs = jnp.concatenate([qg, zeros] if g == 0 else [zeros, qg], axis=0)
                sT = jnp.dot(kband, rhs, preferred_element_type=F32) + bias_ref[table, g]
                s_ref[g] = sT
                score_max.append(jnp.max(sT, axis=0, keepdims=True))

            part = slice(n * chunk, (n + 1) * chunk)
            qT_next_ref[part, :] = (
                lax.dot_general(wqT_ref[part, :], xb_next, _NT, preferred_element_type=F32)
                * (B_HEAD_DIM ** -0.5 * LOG2_E)).astype(BF16)
            gate = jnp.dot(xb_next, wg_ref[:, part], preferred_element_type=F32)
            sg_next_ref[:, part] = gate * _sigmoid(gate)

            o_parts = []
            for g in range(B_KV_HEADS):
                sink = sink_ref[g]
                m = jnp.maximum(score_max[g], sink)
                e = jnp.exp2(s_ref[g] - m).astype(BF16)
                v_rows = slice(g * B_VROWS, (g + 1) * B_VROWS)
                oT = jnp.dot(vband[v_rows, :], e, preferred_element_type=F32)
                den = oT[B_HEAD_DIM:B_HEAD_DIM + 1] + jnp.exp2(sink - m)
                o_parts.append(oT[:B_HEAD_DIM] / den)
            for hi in range(B_GROUP):
                cols = slice(hi * Q_BLOCK, (hi + 1) * Q_BLOCK)
                o_hi = jnp.concatenate([o_parts[0][:, cols], o_parts[1][:, cols]], axis=0).T
                og_ref[lanes, cols] = (o_hi * sg_ref[lanes, cols]).astype(BF16)

            if (n + 1) % half == 0:
                rows = slice((n + 1 - half) * Q_BLOCK, (n + 1) * Q_BLOCK)
                y = jnp.dot(og_ref[rows, :], wo_ref[...], preferred_element_type=F32)
                xo_ref[rows, :] = _deepnorm_ln(x_ref[rows, :], y, g_ref[...], b_ref[...])

    @pl.when(step % 2 == 0)
    def _even():
        body(qT0_ref, sg0_ref, qT1_ref, sg1_ref)

    @pl.when(step % 2 == 1)
    def _odd():
        body(qT1_ref, sg1_ref, qT0_ref, sg0_ref)


def _swa_layer(x, j, params, ks, vsT, bias, tm):
    B, S, _ = x.shape
    nt = S // tm
    nblk = tm // Q_BLOCK
    once = pl.Buffered(1)
    done = pl.BlockSpec((None, tm, D_MODEL), lambda b_, i: (b_, jnp.maximum(i - 1, 0), 0))
    weights = [params[k] for k in ("wqT", "wg", "w_o")]
    small = [params[k] for k in ("sink", "ln_g", "ln_b")]
    return pl.pallas_call(
        functools.partial(_swa_layer_kernel, nblk=nblk),
        grid=(B, nt + 1),
        in_specs=[
            pl.BlockSpec((None, tm, D_MODEL), lambda b_, i: (b_, jnp.minimum(i, nt - 1), 0)),
            done, *[_layer_block(w, j) for w in weights],
            pl.BlockSpec((None, S, B_KV_WIDTH), lambda b_, i: (b_, 0, 0), pipeline_mode=once),
            pl.BlockSpec((None, S // Q_BLOCK, B_KV_HEADS * B_VROWS, Q_BLOCK), lambda b_, i: (b_, 0, 0, 0),
                         pipeline_mode=once),
            pl.BlockSpec(bias.shape, lambda b_, i: (0,) * bias.ndim, pipeline_mode=once),
            *[_layer_block(w, j) for w in small],
        ],
        out_specs=done,
        out_shape=jax.ShapeDtypeStruct(x.shape, F32),
        scratch_shapes=[
            pltpu.VMEM((B_WIDTH, tm), BF16), pltpu.VMEM((tm, B_WIDTH), F32),
            pltpu.VMEM((B_WIDTH, tm), BF16), pltpu.VMEM((tm, B_WIDTH), F32),
            pltpu.VMEM((tm, B_WIDTH), BF16),
            pltpu.VMEM((B_KV_HEADS, 2 * Q_BLOCK, B_GROUP * Q_BLOCK), F32),
        ],
        compiler_params=_cparams(2),
        name="swa_layer",
    )(x, x, *weights, ks, vsT, bias, *small)


def _rope_tables(S):
    inv = ROPE_THETA ** (-jnp.arange(ROPE_HALF, dtype=F32) / ROPE_HALF)
    ang = jnp.arange(S, dtype=F32)[:, None] * inv[None, :]
    cos, sin = jnp.cos(ang), jnp.sin(ang)
    zeros = jnp.zeros((S, LANES - A_ROPE), F32)
    z32 = jnp.zeros((S, ROPE_HALF), F32)
    cosk = jnp.concatenate([cos, cos, zeros], axis=1)
    sinka = jnp.concatenate([-sin, z32, zeros], axis=1)
    sinkb = jnp.concatenate([z32, sin, zeros], axis=1)
    return cos.T, sin.T, cosk, sinka, sinkb


def _mla_params(w_in, q_norm, kv_norm, w_uq, w_ukv, w_o, ln_gain, ln_bias):
    n = w_in.shape[0]
    o_kr = A_Q_LORA + A_KV_LORA
    w_in_pad = jnp.concatenate(
        [w_in[..., :o_kr + A_ROPE], jnp.zeros((n, D_MODEL, LANES - A_ROPE), w_in.dtype),
         w_in[..., o_kr + A_ROPE:]], axis=-1)
    w_ukv_h = w_ukv.reshape(n, A_KV_LORA, A_HEADS, A_NOPE + A_VDIM)
    w_kn = w_ukv_h[..., :A_NOPE].reshape(n, A_KV_LORA, A_HEADS * A_NOPE)
    w_v = w_ukv_h[..., A_NOPE:].reshape(n, A_KV_LORA, A_WIDTH)
    return {
        "w_in": w_in_pad.astype(BF16), "gq": q_norm[:, None, :], "gkv": kv_norm[:, None, :],
        "w_uqT": jnp.swapaxes(w_uq, 1, 2).astype(BF16), "w_kn": w_kn.astype(BF16),
        "w_vT": jnp.swapaxes(w_v, 1, 2).astype(BF16), "w_o": w_o.astype(BF16),
        "ln_g": ln_gain[:n, None, :], "ln_b": ln_bias[:n, None, :],
    }


def _swa_feature_order(w, axis):
    shape = w.shape
    split = shape[:axis] + (B_KV_HEADS, B_GROUP, B_HEAD_DIM) + shape[axis + 1:]
    return jnp.swapaxes(w.reshape(split), axis, axis + 1).reshape(shape)


def _swa_params(w_in, sinks, w_o, ln_gain, ln_bias):
    n = w_in.shape[0]
    sink = sinks.astype(F32).reshape(n, B_KV_HEADS, 1, B_GROUP) * LOG2_E
    return {
        "wqT": jnp.swapaxes(w_in[..., :B_WIDTH], 1, 2).astype(BF16),
        "wg": _swa_feature_order(w_in[..., B_WIDTH:], 2).astype(BF16),
        "w_o": _swa_feature_order(w_o, 1).astype(BF16),
        "sink": jnp.repeat(sink, Q_BLOCK, axis=3),
        "ln_g": ln_gain[-n:, None, :], "ln_b": ln_bias[-n:, None, :],
    }


def kernel(x, w_in_a, q_norm_a, kv_norm_a, w_uq_a, w_ukv_a, w_o_a, w_kv_shared, w_in_b, sinks_b,
           w_o_b, rel_bias_table, ln_gain, ln_bias):
    B, S, _ = x.shape
    tm = min(512, S)
    tq = min(512, S)
    rope = _rope_tables(S)

    mla = _mla_params(w_in_a, q_norm_a, kv_norm_a, w_uq_a, w_ukv_a, w_o_a, ln_gain, ln_bias)
    prev = None
    for i in range(N_A_LAYERS):
        outs = _mla_proj(x, i, mla, rope, tm, tq, tq, prev)
        if prev is not None:
            x, *outs = outs
        qT, kn, kr, vT, sg = outs
        prev = (_mla_attn(qT, kn, kr, vT), sg)
    x, ks, vsT = _out_ln_kv(*prev, x, N_A_LAYERS - 1, mla, w_kv_shared[:, :B_KV_WIDTH].astype(BF16),
                            w_kv_shared[:, B_KV_WIDTH:].T.astype(BF16), tm)

    bias = _band_bias(rel_bias_table)
    swa = _swa_params(w_in_b, sinks_b, w_o_b, ln_gain, ln_bias)
    for j in range(DEPTH - N_A_LAYERS):
        x = _swa_layer(x, j, swa, ks, vsT, bias, min(512, S))
    return x
```

```python
import functools
import math

import jax
import jax.numpy as jnp
import numpy as np
from jax import lax
from jax.experimental import pallas as pl
from jax.experimental.pallas import tpu as pltpu

F32 = jnp.float32
BF16 = jnp.bfloat16

D_MODEL = 1024
DEPTH = 4
CHUNK = 64
Q_BLOCK = 128
N_A_LAYERS = DEPTH // 2
A_HEADS = 8
A_NOPE = 128
A_ROPE = 64
A_VDIM = 128
A_QK = A_NOPE + A_ROPE
A_Q_LORA = 384
A_KV_LORA = 256
A_WIDTH = A_HEADS * A_VDIM
A_QPAD = 256
ROPE_THETA = 10000.0
ROPE_HALF = A_ROPE // 2
B_Q_HEADS = 16
B_KV_HEADS = 2
B_GROUP = B_Q_HEADS // B_KV_HEADS
B_HEAD_DIM = 64
B_WIDTH = B_Q_HEADS * B_HEAD_DIM
B_KV_WIDTH = B_KV_HEADS * B_HEAD_DIM
B_VROWS = B_HEAD_DIM + 16
WINDOW = 128
WIN_CHUNKS = WINDOW // CHUNK
NUM_BUCKETS = 32
MAX_DISTANCE = 128
DEEPNORM_ALPHA = (2 * DEPTH) ** 0.25
NORM_EPS = 1e-5
MASK_VALUE = -1e30
LOG2_E = math.log2(math.e)

LANES = 128
VMEM_LIMIT_BYTES = 56 * 1024 * 1024

_NT = (((1,), (1,)), ((), ()))
UNROLL = 68
WEIGHT_BUFFERS = 4
BOUND_MARGIN = 1.0 + 2.0 ** -6
DENOM_FLOOR = 2.0 ** -80


def _cparams(n_axes):
    return pltpu.CompilerParams(
        dimension_semantics=("arbitrary",) * n_axes, vmem_limit_bytes=VMEM_LIMIT_BYTES)


def _sigmoid(x):
    return 1.0 / (1.0 + jnp.exp(-x))


def _rms_scale(c, g):
    return c * lax.rsqrt(jnp.mean(c * c, axis=-1, keepdims=True) + NORM_EPS) * g


def _deepnorm_ln(x, y, g, b):
    z = DEEPNORM_ALPHA * x + y
    mu = jnp.mean(z, axis=-1, keepdims=True)
    zc = z - mu
    var = jnp.mean(zc * zc, axis=-1, keepdims=True)
    return zc * lax.rsqrt(var + NORM_EPS) * g + b


def _mla_proj_kernel(*refs, after_layer):
    if after_layer:
        o_ref, sg_prev_ref, x_prev_ref, wo_ref, g_ref, b_ref, *refs = refs
        *refs, xo_ref, qT_ref, kn_ref, kr_ref, vT_ref, sg_ref, kmax_ref = refs
    else:
        x_ref, *refs = refs
        *refs, qT_ref, kn_ref, kr_ref, vT_ref, sg_ref, kmax_ref = refs
    (w_in_ref, gq_ref, gkv_ref, w_uqT_ref, w_kn_ref, w_vT_ref,
     cosT_ref, sinT_ref, cosk_ref, sinka_ref, sinkb_ref, ktail_ref) = refs

    @pl.when(pl.program_id(1) == 0)
    def _start_of_sequence():
        kmax_ref[...] = jnp.zeros(kmax_ref.shape, F32)

    if after_layer:
        _gated_out_ln(o_ref, sg_prev_ref, x_prev_ref, wo_ref, g_ref, b_ref, xo_ref)
        x_ref = xo_ref
    xb = x_ref[...].astype(BF16)
    h = jnp.dot(xb, w_in_ref[...], preferred_element_type=F32)
    o_kv = A_Q_LORA
    o_kr = A_Q_LORA + A_KV_LORA
    o_g = o_kr + LANES
    cq = _rms_scale(h[:, :o_kv], gq_ref[...]).astype(BF16)
    ckv = _rms_scale(h[:, o_kv:o_kr], gkv_ref[...]).astype(BF16)

    scale = A_QK ** -0.5 * math.log2(math.e)
    qT = lax.dot_general(w_uqT_ref[...], cq, _NT, preferred_element_type=F32) * scale

    krs = h[:, o_kr:o_g]
    kr = (krs * cosk_ref[...]
          + pltpu.roll(krs, LANES - ROPE_HALF, axis=1) * sinka_ref[...]
          + pltpu.roll(krs, ROPE_HALF, axis=1) * sinkb_ref[...])
    kr_ref[...] = (kr + ktail_ref[...]).astype(BF16)
    kn = jnp.dot(ckv, w_kn_ref[...], preferred_element_type=F32)

    kr2 = jnp.sum(kr * kr, axis=1, keepdims=True)
    lane = lax.broadcasted_iota(jnp.int32, kmax_ref.shape, 1)
    kmax2 = kmax_ref[...]
    for hh in range(A_HEADS):
        knh = kn[:, hh * A_NOPE:(hh + 1) * A_NOPE]
        k2 = jnp.max(jnp.sum(knh * knh, axis=1, keepdims=True) + kr2, axis=0, keepdims=True)
        kmax2 = jnp.where(lane == hh, jnp.maximum(kmax2, k2), kmax2)
    kmax_ref[...] = kmax2

    cos = cosT_ref[...]
    sin = sinT_ref[...]
    n_r1 = A_NOPE
    n_r2 = n_r1 + ROPE_HALF
    n_r3 = n_r2 + ROPE_HALF
    tail_row = lax.broadcasted_iota(jnp.int32, (A_QPAD - A_QK, qT.shape[1]), 0)
    bound_row = vT_ref.shape[-1] // CHUNK
    for hh in range(A_HEADS):
        r0 = hh * A_QK
        t1 = qT[r0 + n_r1:r0 + n_r2]
        t2 = qT[r0 + n_r2:r0 + n_r3]
        qT_ref[hh, :n_r1, :] = qT[r0:r0 + n_r1].astype(BF16)
        qT_ref[hh, n_r1:n_r2, :] = (t1 * cos - t2 * sin).astype(BF16)
        qT_ref[hh, n_r2:n_r3, :] = (t2 * cos + t1 * sin).astype(BF16)
        qh = qT[r0:r0 + A_QK]
        q2 = jnp.sum(qh * qh, axis=0, keepdims=True)
        bound = jnp.sqrt(q2 * kmax2[:, hh:hh + 1]) * BOUND_MARGIN
        qT_ref[hh, n_r3:, :] = jnp.where(tail_row == bound_row, -bound, 0.0).astype(BF16)

    kn = kn.astype(BF16)
    vT = lax.dot_general(w_vT_ref[...], ckv, _NT, preferred_element_type=F32).astype(BF16)
    tk = vT_ref.shape[-1]
    for hh in range(A_HEADS):
        kn_ref[hh] = kn[:, hh * A_NOPE:(hh + 1) * A_NOPE]
        for n in range(vT_ref.shape[1]):
            vT_ref[hh, n] = vT[hh * A_VDIM:(hh + 1) * A_VDIM, n * tk:(n + 1) * tk]

    gate = h[:, o_g:]
    sg_ref[...] = (gate * _sigmoid(gate)).astype(BF16)


def _layer_block(stacked, layer):
    zeros = (0,) * (stacked.ndim - 1)
    return pl.BlockSpec((None,) + stacked.shape[1:], lambda *_: (layer,) + zeros,
                        pipeline_mode=pl.Buffered(1))


def _mla_proj(x, layer, params, rope, tm, tq, tk, prev=None):
    B, S, _ = x.shape
    nt = S // tm
    cosT, sinT, cosk, sinka, sinkb = rope
    weights = [params[k] for k in ("w_in", "gq", "gkv", "w_uqT", "w_kn", "w_vT")]
    key = np.arange(S)[:, None]
    lane = np.arange(LANES)[None, :] - A_ROPE
    ktail = jnp.asarray((lane == (key % tk) // CHUNK) | (lane == tk // CHUNK), F32)
    tok_lanes = pl.BlockSpec((tm, LANES), lambda b, i: (i, 0))
    row = pl.BlockSpec((None, tm, D_MODEL), lambda b, i: (b, i, 0))
    in_specs = [_layer_block(w, layer) for w in weights] + [
        pl.BlockSpec((ROPE_HALF, tm), lambda b, i: (0, i)),
        pl.BlockSpec((ROPE_HALF, tm), lambda b, i: (0, i)),
        tok_lanes, tok_lanes, tok_lanes, tok_lanes,
    ]
    operands = (*weights, cosT, sinT, cosk, sinka, sinkb, ktail)
    out_specs = [
        pl.BlockSpec((None, A_HEADS, None, A_QPAD, tm),
                     lambda b, i: (b, 0, i // (tq // tm), 0, i % (tq // tm))),
        pl.BlockSpec((None, A_HEADS, tm, A_NOPE), lambda b, i: (b, 0, i, 0)),
        pl.BlockSpec((None, tm, LANES), lambda b, i: (b, i, 0)),
        pl.BlockSpec((None, A_HEADS, tm // tk, A_VDIM, tk), lambda b, i: (b, 0, i, 0, 0)),
        pl.BlockSpec((None, tm, A_WIDTH), lambda b, i: (b, i, 0)),
    ]
    out_shape = [
        jax.ShapeDtypeStruct((B, A_HEADS, S // tq, A_QPAD, tq), BF16),
        jax.ShapeDtypeStruct((B, A_HEADS, S, A_NOPE), BF16),
        jax.ShapeDtypeStruct((B, S, LANES), BF16),
        jax.ShapeDtypeStruct((B, A_HEADS, S // tk, A_VDIM, tk), BF16),
        jax.ShapeDtypeStruct((B, S, A_WIDTH), BF16),
    ]
    if prev is None:
        in_specs = [row] + in_specs
        operands = (x,) + operands
    else:
        o, sg = prev
        tail = [params[k] for k in ("w_o", "ln_g", "ln_b")]
        heads = pl.BlockSpec((None, A_HEADS, None, A_VDIM, tm), lambda b_, i: (b_, 0, i, 0, 0))
        in_specs = [heads, row, row] + [_layer_block(w, layer - 1) for w in tail] + in_specs
        operands = (o, sg, x, *tail) + operands
        out_specs = [row] + out_specs
        out_shape = [jax.ShapeDtypeStruct(x.shape, F32)] + out_shape
    return pl.pallas_call(
        functools.partial(_mla_proj_kernel, after_layer=prev is not None),
        grid=(B, nt),
        in_specs=in_specs,
        out_specs=out_specs,
        out_shape=out_shape,
        scratch_shapes=[pltpu.VMEM((1, LANES), F32)],
        compiler_params=_cparams(2),
        name="mla_proj" if prev is None else "out_ln_mla_proj",
    )(*operands)


def _mla_attn_kernel(qT_ref, kn_ref, kr_ref, vT_ref, mrow_ref, o_ref, *scratch, nq, tq):
    *p_refs, l_ref, acc_ref = scratch
    n_buf = len(p_refs)

    def shifted_scores(qi, t):
        k0 = pl.multiple_of(t * tq, tq)
        kcat = jnp.concatenate([kn_ref[pl.ds(k0, tq), :], kr_ref[pl.ds(k0, tq), :]], axis=1)
        tail = qT_ref[qi, A_QK:, :] + mrow_ref[jnp.where(t == qi, 1, 0)]
        q_full = jnp.concatenate([qT_ref[qi, :A_QK, :], tail], axis=0)
        return jnp.dot(kcat, q_full, preferred_element_type=F32)

    def weighted_values(qi, t, buf):
        acc_ref[qi] += jnp.dot(vT_ref[t], p_refs[buf][...], preferred_element_type=F32)

    acc_ref[...] = jnp.zeros(acc_ref.shape, F32)
    l_ref[...] = jnp.zeros(l_ref.shape, F32)
    for p_ref in p_refs[-2:]:
        p_ref[...] = jnp.zeros(p_ref.shape, BF16)

    def step(carry, buf):
        qi, t, qi_1, t_1, qi_2, t_2 = carry
        weighted_values(qi_2, t_2, (buf - 2) % n_buf)
        p = jnp.exp2(shifted_scores(qi, t))
        l_ref[qi] += jnp.sum(p, axis=0, keepdims=True)
        p_refs[buf][...] = p.astype(BF16)
        last = t == qi
        return jnp.where(last, qi + 1, qi), jnp.where(last, 0, t + 1), qi, t, qi_1, t_1

    carry = (jnp.int32(0),) * 6
    n_steps = nq * (nq + 1) // 2
    n_lead = n_steps % UNROLL
    for u in range(n_lead):
        carry = step(carry, u % n_buf)

    def steps(it, carry):
        for u in range(UNROLL):
            carry = step(carry, (n_lead + u) % n_buf)
        return carry

    _, _, qi_1, t_1, qi_2, t_2 = lax.fori_loop(0, n_steps // UNROLL, steps, carry)
    weighted_values(qi_2, t_2, (n_steps - 2) % n_buf)
    weighted_values(qi_1, t_1, (n_steps - 1) % n_buf)

    def write_out(qi, acc, l):
        o_ref[qi] = (acc / l).astype(o_ref.dtype)

    def finish(qi, denom_min):
        l = l_ref[qi]
        write_out(qi, acc_ref[qi], l)
        return jnp.minimum(denom_min, l)

    denom_min = lax.fori_loop(0, nq, finish, jnp.full((1, tq), jnp.inf, F32), unroll=4)

    @pl.when(jnp.logical_not(jnp.min(denom_min) >= DENOM_FLOOR))
    def _redo_with_running_max():
        def query_tile(qi, carry):
            def key_tile(t, mla):
                m, l, acc = mla
                s = shifted_scores(qi, t)
                m_new = jnp.maximum(m, jnp.max(s, axis=0, keepdims=True))
                alpha = jnp.exp2(m - m_new)
                p = jnp.exp2(s - m_new)
                l = l * alpha + jnp.sum(p, axis=0, keepdims=True)
                acc = acc * alpha + jnp.dot(vT_ref[t], p.astype(BF16), preferred_element_type=F32)
                return m_new, l, acc

            init = (jnp.full((1, tq), MASK_VALUE, F32), jnp.zeros((1, tq), F32),
                    jnp.zeros((A_VDIM, tq), F32))
            _, l, acc = lax.fori_loop(0, qi + 1, key_tile, init)
            write_out(qi, acc, l)
            return carry

        lax.fori_loop(0, nq, query_tile, 0)


def _mla_attn(qT, kn, kr, vT):
    B, H, nq, _, tq = qT.shape
    S = kn.shape[2]
    assert vT.shape[2:] == (nq, A_VDIM, tq) and tq // CHUNK < A_QPAD - A_QK
    assert UNROLL % WEIGHT_BUFFERS == 0 and WEIGHT_BUFFERS >= 3
    a = np.arange(A_QPAD - A_QK)[:, None]
    qchunk = (np.arange(tq) // CHUNK)[None, :]
    diagonal = np.where((qchunk < a) & (a < tq // CHUNK), MASK_VALUE, 0.0).astype(np.float32)
    mrow = jnp.asarray(np.stack([np.zeros_like(diagonal), diagonal]), BF16)
    return pl.pallas_call(
        functools.partial(_mla_attn_kernel, nq=nq, tq=tq),
        grid=(B, H),
        in_specs=[
            pl.BlockSpec((None, None, nq, A_QPAD, tq), lambda b, h: (b, h, 0, 0, 0)),
            pl.BlockSpec((None, None, S, A_NOPE), lambda b, h: (b, h, 0, 0)),
            pl.BlockSpec((None, S, LANES), lambda b, h: (b, 0, 0)),
            pl.BlockSpec((None, None, nq, A_VDIM, tq), lambda b, h: (b, h, 0, 0, 0)),
            pl.BlockSpec(mrow.shape, lambda b, h: (0, 0, 0)),
        ],
        out_specs=pl.BlockSpec((None, None, nq, A_VDIM, tq), lambda b, h: (b, h, 0, 0, 0)),
        out_shape=jax.ShapeDtypeStruct((B, H, nq, A_VDIM, tq), BF16),
        scratch_shapes=[
            *([pltpu.VMEM((tq, tq), BF16)] * WEIGHT_BUFFERS),
            pltpu.VMEM((nq, 1, tq), F32),
            pltpu.VMEM((nq, A_VDIM, tq), F32),
        ],
        compiler_params=_cparams(2),
        name="mla_attn",
    )(qT, kn, kr, vT, mrow)


def _gated_out_ln(o_ref, sg_ref, x_ref, wo_ref, g_ref, b_ref, xo_ref):
    half = xo_ref.shape[0] // 2
    for rows in (slice(0, half), slice(half, 2 * half)):
        o = jnp.concatenate(
            [o_ref[hh, :, rows].astype(F32).T for hh in range(o_ref.shape[0])], axis=1)
        y = jnp.dot((o * sg_ref[rows, :]).astype(BF16), wo_ref[...], preferred_element_type=F32)
        xo_ref[rows, :] = _deepnorm_ln(x_ref[rows, :], y, g_ref[...], b_ref[...])


def _out_ln_kv_kernel(o_ref, sg_ref, x_ref, wo_ref, g_ref, b_ref, wk_ref, wvT_ref,
                      xo_ref, ks_ref, vsT_ref):
    _gated_out_ln(o_ref, sg_ref, x_ref, wo_ref, g_ref, b_ref, xo_ref)
    xb = xo_ref[...].astype(BF16)
    ks_ref[...] = jnp.dot(xb, wk_ref[...], preferred_element_type=F32).astype(BF16)
    vT = lax.dot_general(wvT_ref[...], xb, _NT, preferred_element_type=F32).astype(BF16)
    ones = (lax.broadcasted_iota(jnp.int32, (B_VROWS - B_HEAD_DIM, vT.shape[1]), 0) == 0).astype(BF16)
    vT = jnp.concatenate(
        [part for g in range(B_KV_HEADS) for part in (vT[g * B_HEAD_DIM:(g + 1) * B_HEAD_DIM], ones)],
        axis=0)
    for n in range(vsT_ref.shape[0]):
        vsT_ref[n] = vT[:, n * Q_BLOCK:(n + 1) * Q_BLOCK]


def _out_ln_kv(o, sg, x, layer, params, wk, wvT, tm):
    B, S, _ = x.shape
    row = pl.BlockSpec((None, tm, D_MODEL), lambda b_, i: (b_, i, 0))
    const = lambda *shape: pl.BlockSpec(shape, lambda b_, i: (0,) * len(shape))
    heads = pl.BlockSpec((None, A_HEADS, None, A_VDIM, tm), lambda b_, i: (b_, 0, i, 0, 0))
    tail = [params[k] for k in ("w_o", "ln_g", "ln_b")]
    nb = tm // Q_BLOCK
    return pl.pallas_call(
        _out_ln_kv_kernel, grid=(B, S // tm),
        in_specs=[heads, row, row] + [_layer_block(w, layer) for w in tail]
                 + [const(*wk.shape), const(*wvT.shape)],
        out_specs=[
            row,
            pl.BlockSpec((None, tm, B_KV_WIDTH), lambda b_, i: (b_, i, 0)),
            pl.BlockSpec((None, nb, B_KV_HEADS * B_VROWS, Q_BLOCK), lambda b_, i: (b_, i, 0, 0)),
        ],
        out_shape=[
            jax.ShapeDtypeStruct(x.shape, F32),
            jax.ShapeDtypeStruct((B, S, B_KV_WIDTH), BF16),
            jax.ShapeDtypeStruct((B, S // Q_BLOCK, B_KV_HEADS * B_VROWS, Q_BLOCK), BF16),
        ],
        compiler_params=_cparams(2), name="out_ln_kv",
    )(o, sg, x, *tail, wk, wvT)


def _band_bias_kernel(tab_ref, bkt_ref, ok_ref, out_ref):
    bkt = bkt_ref[...]
    ok = ok_ref[...] != 0
    has_prev = lax.broadcasted_iota(jnp.int32, bkt.shape, 0) >= Q_BLOCK
    for h in range(B_Q_HEADS):
        acc = jnp.zeros(bkt.shape, F32)
        for bucket in range(NUM_BUCKETS):
            acc = jnp.where(bkt == bucket, tab_ref[bucket, h], acc)
        generic = jnp.where(ok, acc * LOG2_E, MASK_VALUE)
        g, hi = divmod(h, B_GROUP)
        out_ref[0, g, :, hi * Q_BLOCK:(hi + 1) * Q_BLOCK] = jnp.where(has_prev, generic, MASK_VALUE)
        out_ref[1, g, :, hi * Q_BLOCK:(hi + 1) * Q_BLOCK] = generic


def _t5_bucket_map(rel):
    half = NUM_BUCKETS // 2
    ret = jnp.where(rel > 0, half, 0)
    n = jnp.abs(rel)
    max_exact = half // 2
    large = max_exact + (jnp.log(jnp.maximum(n, 1).astype(F32) / max_exact)
                         / math.log(MAX_DISTANCE / max_exact) * (half - max_exact)).astype(jnp.int32)
    large = jnp.minimum(large, half - 1)
    return ret + jnp.where(n < max_exact, n, large)


def _band_bias(rel_bias_table):
    j = jnp.arange(2 * Q_BLOCK)[:, None]
    r = jnp.arange(Q_BLOCK)[None, :]
    rel = j - Q_BLOCK - r
    d = (r // CHUNK) - ((j - Q_BLOCK) // CHUNK)
    ok = ((d >= 0) & (d <= WIN_CHUNKS)).astype(jnp.int32)
    bkt = _t5_bucket_map(rel).astype(jnp.int32)
    return pl.pallas_call(
        _band_bias_kernel,
        in_specs=[
            pl.BlockSpec(memory_space=pltpu.SMEM),
            pl.BlockSpec(memory_space=pltpu.VMEM),
            pl.BlockSpec(memory_space=pltpu.VMEM),
        ],
        out_specs=pl.BlockSpec(memory_space=pltpu.VMEM),
        out_shape=jax.ShapeDtypeStruct((2, B_KV_HEADS, 2 * Q_BLOCK, B_GROUP * Q_BLOCK), F32),
        name="band_bias",
    )(rel_bias_table, bkt, ok)


def _swa_layer_kernel(xn_ref, x_ref, wqT_ref, wg_ref, wo_ref, ks_ref, vsT_ref, bias_ref, sink_ref,
                      g_ref, b_ref, xo_ref, qT0_ref, sg0_ref, qT1_ref, sg1_ref, og_ref, s_ref, *, nblk):
    step = pl.program_id(1)

    @pl.when((pl.program_id(0) == 0) & (step == 0))
    def _first_step():
        qT0_ref[...] = jnp.zeros(qT0_ref.shape, qT0_ref.dtype)
        sg0_ref[...] = jnp.zeros(sg0_ref.shape, sg0_ref.dtype)

    def body(qT_ref, sg_ref, qT_next_ref, sg_next_ref):
        tile = jnp.maximum(step - 1, 0)
        xb_next = xn_ref[...].astype(BF16)
        zeros = jnp.zeros((B_HEAD_DIM, B_GROUP * Q_BLOCK), BF16)
        chunk = B_WIDTH // nblk
        half = max(nblk // 2, 1)

        for n in range(nblk):
            blk = tile * nblk + n
            cur = pl.multiple_of(blk * Q_BLOCK, Q_BLOCK)
            prev_blk = jnp.maximum(blk - 1, 0)
            prev = pl.multiple_of(prev_blk * Q_BLOCK, Q_BLOCK)
            kband = jnp.concatenate(
                [ks_ref[pl.ds(prev, Q_BLOCK), :], ks_ref[pl.ds(cur, Q_BLOCK), :]], axis=0)
            vband = jnp.concatenate([vsT_ref[prev_blk], vsT_ref[blk]], axis=1)
            table = jnp.minimum(blk, 1)
            lanes = slice(n * Q_BLOCK, (n + 1) * Q_BLOCK)
            score_max = []
            for g in range(B_KV_HEADS):
                qg = jnp.concatenate(
                    [qT_ref[(g * B_GROUP + hi) * B_HEAD_DIM:(g * B_GROUP + hi + 1) * B_HEAD_DIM, lanes]
                     for hi in range(B_GROUP)], axis=1)
                rhs = jnp.concatenate([qg, zeros] if g == 0 else [zeros, qg], axis=0)
                sT = jnp.dot(kband, rhs, preferred_element_type=F32) + bias_ref[table, g]
                s_ref[g] = sT
                score_max.append(jnp.max(sT, axis=0, keepdims=True))

            part = slice(n * chunk, (n + 1) * chunk)
            qT_next_ref[part, :] = (
                lax.dot_general(wqT_ref[part, :], xb_next, _NT, preferred_element_type=F32)
                * (B_HEAD_DIM ** -0.5 * LOG2_E)).astype(BF16)
            gate = jnp.dot(xb_next, wg_ref[:, part], preferred_element_type=F32)
            sg_next_ref[:, part] = gate * _sigmoid(gate)

            o_parts = []
            for g in range(B_KV_HEADS):
                sink = sink_ref[g]
                m = jnp.maximum(score_max[g], sink)
                e = jnp.exp2(s_ref[g] - m).astype(BF16)
                v_rows = slice(g * B_VROWS, (g + 1) * B_VROWS)
                oT = jnp.dot(vband[v_rows, :], e, preferred_element_type=F32)
                den = oT[B_HEAD_DIM:B_HEAD_DIM + 1] + jnp.exp2(sink - m)
                o_parts.append(oT[:B_HEAD_DIM] / den)
            for hi in range(B_GROUP):
                cols = slice(hi * Q_BLOCK, (hi + 1) * Q_BLOCK)
                o_hi = jnp.concatenate([o_parts[0][:, cols], o_parts[1][:, cols]], axis=0).T
                og_ref[lanes, cols] = (o_hi * sg_ref[lanes, cols]).astype(BF16)

            if (n + 1) % half == 0:
                rows = slice((n + 1 - half) * Q_BLOCK, (n + 1) * Q_BLOCK)
                y = jnp.dot(og_ref[rows, :], wo_ref[...], preferred_element_type=F32)
                xo_ref[rows, :] = _deepnorm_ln(x_ref[rows, :], y, g_ref[...], b_ref[...])

    @pl.when(step % 2 == 0)
    def _even():
        body(qT0_ref, sg0_ref, qT1_ref, sg1_ref)

    @pl.when(step % 2 == 1)
    def _odd():
        body(qT1_ref, sg1_ref, qT0_ref, sg0_ref)


def _swa_layer(x, j, params, ks, vsT, bias, tm):
    B, S, _ = x.shape
    nt = S // tm
    nblk = tm // Q_BLOCK
    once = pl.Buffered(1)
    done = pl.BlockSpec((None, tm, D_MODEL), lambda b_, i: (b_, jnp.maximum(i - 1, 0), 0))
    weights = [params[k] for k in ("wqT", "wg", "w_o")]
    small = [params[k] for k in ("sink", "ln_g", "ln_b")]
    return pl.pallas_call(
        functools.partial(_swa_layer_kernel, nblk=nblk),
        grid=(B, nt + 1),
        in_specs=[
            pl.BlockSpec((None, tm, D_MODEL), lambda b_, i: (b_, jnp.minimum(i, nt - 1), 0)),
            done, *[_layer_block(w, j) for w in weights],
            pl.BlockSpec((None, S, B_KV_WIDTH), lambda b_, i: (b_, 0, 0), pipeline_mode=once),
            pl.BlockSpec((None, S // Q_BLOCK, B_KV_HEADS * B_VROWS, Q_BLOCK), lambda b_, i: (b_, 0, 0, 0),
                         pipeline_mode=once),
            pl.BlockSpec(bias.shape, lambda b_, i: (0,) * bias.ndim, pipeline_mode=once),
            *[_layer_block(w, j) for w in small],
        ],
        out_specs=done,
        out_shape=jax.ShapeDtypeStruct(x.shape, F32),
        scratch_shapes=[
            pltpu.VMEM((B_WIDTH, tm), BF16), pltpu.VMEM((tm, B_WIDTH), F32),
            pltpu.VMEM((B_WIDTH, tm), BF16), pltpu.VMEM((tm, B_WIDTH), F32),
            pltpu.VMEM((tm, B_WIDTH), BF16),
            pltpu.VMEM((B_KV_HEADS, 2 * Q_BLOCK, B_GROUP * Q_BLOCK), F32),
        ],
        compiler_params=_cparams(2),
        name="swa_layer",
    )(x, x, *weights, ks, vsT, bias, *small)


def _rope_tables(S):
    inv = ROPE_THETA ** (-jnp.arange(ROPE_HALF, dtype=F32) / ROPE_HALF)
    ang = jnp.arange(S, dtype=F32)[:, None] * inv[None, :]
    cos, sin = jnp.cos(ang), jnp.sin(ang)
    zeros = jnp.zeros((S, LANES - A_ROPE), F32)
    z32 = jnp.zeros((S, ROPE_HALF), F32)
    cosk = jnp.concatenate([cos, cos, zeros], axis=1)
    sinka = jnp.concatenate([-sin, z32, zeros], axis=1)
    sinkb = jnp.concatenate([z32, sin, zeros], axis=1)
    return cos.T, sin.T, cosk, sinka, sinkb


def _mla_params(w_in, q_norm, kv_norm, w_uq, w_ukv, w_o, ln_gain, ln_bias):
    n = w_in.shape[0]
    o_kr = A_Q_LORA + A_KV_LORA
    w_in_pad = jnp.concatenate(
        [w_in[..., :o_kr + A_ROPE], jnp.zeros((n, D_MODEL, LANES - A_ROPE), w_in.dtype),
         w_in[..., o_kr + A_ROPE:]], axis=-1)
    w_ukv_h = w_ukv.reshape(n, A_KV_LORA, A_HEADS, A_NOPE + A_VDIM)
    w_kn = w_ukv_h[..., :A_NOPE].reshape(n, A_KV_LORA, A_HEADS * A_NOPE)
    w_v = w_ukv_h[..., A_NOPE:].reshape(n, A_KV_LORA, A_WIDTH)
    return {
        "w_in": w_in_pad.astype(BF16), "gq": q_norm[:, None, :], "gkv": kv_norm[:, None, :],
        "w_uqT": jnp.swapaxes(w_uq, 1, 2).astype(BF16), "w_kn": w_kn.astype(BF16),
        "w_vT": jnp.swapaxes(w_v, 1, 2).astype(BF16), "w_o": w_o.astype(BF16),
        "ln_g": ln_gain[:n, None, :], "ln_b": ln_bias[:n, None, :],
    }


def _swa_feature_order(w, axis):
    shape = w.shape
    split = shape[:axis] + (B_KV_HEADS, B_GROUP, B_HEAD_DIM) + shape[axis + 1:]
    return jnp.swapaxes(w.reshape(split), axis, axis + 1).reshape(shape)


def _swa_params(w_in, sinks, w_o, ln_gain, ln_bias):
    n = w_in.shape[0]
    sink = sinks.astype(F32).reshape(n, B_KV_HEADS, 1, B_GROUP) * LOG2_E
    return {
        "wqT": jnp.swapaxes(w_in[..., :B_WIDTH], 1, 2).astype(BF16),
        "wg": _swa_feature_order(w_in[..., B_WIDTH:], 2).astype(BF16),
        "w_o": _swa_feature_order(w_o, 1).astype(BF16),
        "sink": jnp.repeat(sink, Q_BLOCK, axis=3),
        "ln_g": ln_gain[-n:, None, :], "ln_b": ln_bias[-n:, None, :],
    }


def kernel(x, w_in_a, q_norm_a, kv_norm_a, w_uq_a, w_ukv_a, w_o_a, w_kv_shared, w_in_b, sinks_b,
           w_o_b, rel_bias_table, ln_gain, ln_bias):
    B, S, _ = x.shape
    tm = min(512, S)
    tq = min(512, S)
    rope = _rope_tables(S)

    mla = _mla_params(w_in_a, q_norm_a, kv_norm_a, w_uq_a, w_ukv_a, w_o_a, ln_gain, ln_bias)
    prev = None
    for i in range(N_A_LAYERS):
        outs = _mla_proj(x, i, mla, rope, tm, tq, tq, prev)
        if prev is not None:
            x, *outs = outs
        qT, kn, kr, vT, sg = outs
        prev = (_mla_attn(qT, kn, kr, vT), sg)
    x, ks, vsT = _out_ln_kv(*prev, x, N_A_LAYERS - 1, mla, w_kv_shared[:, :B_KV_WIDTH].astype(BF16),
                            w_kv_shared[:, B_KV_WIDTH:].T.astype(BF16), tm)

    bias = _band_bias(rel_bias_table)
    swa = _swa_params(w_in_b, sinks_b, w_o_b, ln_gain, ln_bias)
    for j in range(DEPTH - N_A_LAYERS):
        x = _swa_layer(x, j, swa, ks, vsT, bias, min(512, S))
    return x
```

```python
import functools
import math

import jax
import jax.numpy as jnp
import numpy as np
from jax import lax
from jax.experimental import pallas as pl
from jax.experimental.pallas import tpu as pltpu

F32 = jnp.float32
BF16 = jnp.bfloat16

D_MODEL = 1024
DEPTH = 4
CHUNK = 64
Q_BLOCK = 128
N_A_LAYERS = DEPTH // 2
A_HEADS = 8
A_NOPE = 128
A_ROPE = 64
A_VDIM = 128
A_QK = A_NOPE + A_ROPE
A_Q_LORA = 384
A_KV_LORA = 256
A_WIDTH = A_HEADS * A_VDIM
A_QPAD = 256
ROPE_THETA = 10000.0
ROPE_HALF = A_ROPE // 2
B_Q_HEADS = 16
B_KV_HEADS = 2
B_GROUP = B_Q_HEADS // B_KV_HEADS
B_HEAD_DIM = 64
B_WIDTH = B_Q_HEADS * B_HEAD_DIM
B_KV_WIDTH = B_KV_HEADS * B_HEAD_DIM
BF16_TILE_ROWS = 16
B_VROWS = B_HEAD_DIM + BF16_TILE_ROWS
WINDOW = 128
WIN_CHUNKS = WINDOW // CHUNK
NUM_BUCKETS = 32
MAX_DISTANCE = 128
DEEPNORM_ALPHA = (2 * DEPTH) ** 0.25
NORM_EPS = 1e-5
MASK_VALUE = -1e30
LOG2_E = math.log2(math.e)

LANES = 128
VMEM_LIMIT_BYTES = 56 * 1024 * 1024

_NT = (((1,), (1,)), ((), ()))
UNROLL = 68
WEIGHT_BUFFERS = 4
BOUND_MARGIN = 1.0 + 2.0 ** -6
DENOM_FLOOR = 2.0 ** -80


def _cparams(n_axes):
    return pltpu.CompilerParams(
        dimension_semantics=("arbitrary",) * n_axes, vmem_limit_bytes=VMEM_LIMIT_BYTES)


def _sigmoid(x):
    return 1.0 / (1.0 + jnp.exp(-x))


def _rms_scale(c, g):
    return c * lax.rsqrt(jnp.mean(c * c, axis=-1, keepdims=True) + NORM_EPS) * g


def _deepnorm_ln(x, y, g, b):
    z = DEEPNORM_ALPHA * x + y
    mu = jnp.mean(z, axis=-1, keepdims=True)
    zc = z - mu
    var = jnp.mean(zc * zc, axis=-1, keepdims=True)
    return zc * lax.rsqrt(var + NORM_EPS) * g + b


def _mla_proj_kernel(*refs, after_layer):
    if after_layer:
        o_ref, sg_prev_ref, x_prev_ref, wo_ref, g_ref, b_ref, *refs = refs
        *refs, xo_ref, qT_ref, kn_ref, kr_ref, vT_ref, sg_ref, kmax_ref = refs
    else:
        x_ref, *refs = refs
        *refs, qT_ref, kn_ref, kr_ref, vT_ref, sg_ref, kmax_ref = refs
    (w_in_ref, gq_ref, gkv_ref, w_uqT_ref, w_kn_ref, w_vT_ref,
     cosT_ref, sinT_ref, cosk_ref, sinka_ref, sinkb_ref, ktail_ref) = refs

    @pl.when(pl.program_id(1) == 0)
    def _start_of_sequence():
        kmax_ref[...] = jnp.zeros(kmax_ref.shape, F32)

    if after_layer:
        _gated_out_ln(o_ref, sg_prev_ref, x_prev_ref, wo_ref, g_ref, b_ref, xo_ref)
        x_ref = xo_ref
    xb = x_ref[...].astype(BF16)
    h = jnp.dot(xb, w_in_ref[...], preferred_element_type=F32)
    o_kv = A_Q_LORA
    o_kr = A_Q_LORA + A_KV_LORA
    o_g = o_kr + LANES
    cq = _rms_scale(h[:, :o_kv], gq_ref[...]).astype(BF16)
    ckv = _rms_scale(h[:, o_kv:o_kr], gkv_ref[...]).astype(BF16)

    scale = A_QK ** -0.5 * math.log2(math.e)
    qT = lax.dot_general(w_uqT_ref[...], cq, _NT, preferred_element_type=F32) * scale

    krs = h[:, o_kr:o_g]
    kr = (krs * cosk_ref[...]
          + pltpu.roll(krs, LANES - ROPE_HALF, axis=1) * sinka_ref[...]
          + pltpu.roll(krs, ROPE_HALF, axis=1) * sinkb_ref[...])
    kr_ref[...] = (kr + ktail_ref[...]).astype(BF16)
    kn = jnp.dot(ckv, w_kn_ref[...], preferred_element_type=F32)

    kr2 = jnp.sum(kr * kr, axis=1, keepdims=True)
    lane = lax.broadcasted_iota(jnp.int32, kmax_ref.shape, 1)
    kmax2 = kmax_ref[...]
    for hh in range(A_HEADS):
        knh = kn[:, hh * A_NOPE:(hh + 1) * A_NOPE]
        k2 = jnp.max(jnp.sum(knh * knh, axis=1, keepdims=True) + kr2, axis=0, keepdims=True)
        kmax2 = jnp.where(lane == hh, jnp.maximum(kmax2, k2), kmax2)
    kmax_ref[...] = kmax2

    cos = cosT_ref[...]
    sin = sinT_ref[...]
    n_r1 = A_NOPE
    n_r2 = n_r1 + ROPE_HALF
    n_r3 = n_r2 + ROPE_HALF
    tail_row = lax.broadcasted_iota(jnp.int32, (A_QPAD - A_QK, qT.shape[1]), 0)
    bound_row = vT_ref.shape[-1] // CHUNK
    for hh in range(A_HEADS):
        r0 = hh * A_QK
        t1 = qT[r0 + n_r1:r0 + n_r2]
        t2 = qT[r0 + n_r2:r0 + n_r3]
        qT_ref[hh, :n_r1, :] = qT[r0:r0 + n_r1].astype(BF16)
        qT_ref[hh, n_r1:n_r2, :] = (t1 * cos - t2 * sin).astype(BF16)
        qT_ref[hh, n_r2:n_r3, :] = (t2 * cos + t1 * sin).astype(BF16)
        qh = qT[r0:r0 + A_QK]
        q2 = jnp.sum(qh * qh, axis=0, keepdims=True)
        bound = jnp.sqrt(q2 * kmax2[:, hh:hh + 1]) * BOUND_MARGIN
        qT_ref[hh, n_r3:, :] = jnp.where(tail_row == bound_row, -bound, 0.0).astype(BF16)

    kn = kn.astype(BF16)
    vT = lax.dot_general(w_vT_ref[...], ckv, _NT, preferred_element_type=F32).astype(BF16)
    tk = vT_ref.shape[-1]
    for hh in range(A_HEADS):
        kn_ref[hh] = kn[:, hh * A_NOPE:(hh + 1) * A_NOPE]
        for n in range(vT_ref.shape[1]):
            vT_ref[hh, n] = vT[hh * A_VDIM:(hh + 1) * A_VDIM, n * tk:(n + 1) * tk]

    gate = h[:, o_g:]
    sg_ref[...] = (gate * _sigmoid(gate)).astype(BF16)


def _layer_block(stacked, layer):
    zeros = (0,) * (stacked.ndim - 1)
    return pl.BlockSpec((None,) + stacked.shape[1:], lambda *_: (layer,) + zeros,
                        pipeline_mode=pl.Buffered(1))


def _mla_proj(x, layer, params, rope, tm, tq, tk, prev=None):
    B, S, _ = x.shape
    nt = S // tm
    cosT, sinT, cosk, sinka, sinkb = rope
    weights = [params[k] for k in ("w_in", "gq", "gkv", "w_uqT", "w_kn", "w_vT")]
    key = np.arange(S)[:, None]
    lane = np.arange(LANES)[None, :] - A_ROPE
    ktail = jnp.asarray((lane == (key % tk) // CHUNK) | (lane == tk // CHUNK), F32)
    tok_lanes = pl.BlockSpec((tm, LANES), lambda b, i: (i, 0))
    row = pl.BlockSpec((None, tm, D_MODEL), lambda b, i: (b, i, 0))
    in_specs = [_layer_block(w, layer) for w in weights] + [
        pl.BlockSpec((ROPE_HALF, tm), lambda b, i: (0, i)),
        pl.BlockSpec((ROPE_HALF, tm), lambda b, i: (0, i)),
        tok_lanes, tok_lanes, tok_lanes, tok_lanes,
    ]
    operands = (*weights, cosT, sinT, cosk, sinka, sinkb, ktail)
    out_specs = [
        pl.BlockSpec((None, A_HEADS, None, A_QPAD, tm),
                     lambda b, i: (b, 0, i // (tq // tm), 0, i % (tq // tm))),
        pl.BlockSpec((None, A_HEADS, tm, A_NOPE), lambda b, i: (b, 0, i, 0)),
        pl.BlockSpec((None, tm, LANES), lambda b, i: (b, i, 0)),
        pl.BlockSpec((None, A_HEADS, tm // tk, A_VDIM, tk), lambda b, i: (b, 0, i, 0, 0)),
        pl.BlockSpec((None, tm, A_WIDTH), lambda b, i: (b, i, 0)),
    ]
    out_shape = [
        jax.ShapeDtypeStruct((B, A_HEADS, S // tq, A_QPAD, tq), BF16),
        jax.ShapeDtypeStruct((B, A_HEADS, S, A_NOPE), BF16),
        jax.ShapeDtypeStruct((B, S, LANES), BF16),
        jax.ShapeDtypeStruct((B, A_HEADS, S // tk, A_VDIM, tk), BF16),
        jax.ShapeDtypeStruct((B, S, A_WIDTH), BF16),
    ]
    if prev is None:
        in_specs = [row] + in_specs
        operands = (x,) + operands
    else:
        o, sg = prev
        tail = [params[k] for k in ("w_o", "ln_g", "ln_b")]
        heads = pl.BlockSpec((None, A_HEADS, None, A_VDIM, tm), lambda b_, i: (b_, 0, i, 0, 0))
        in_specs = [heads, row, row] + [_layer_block(w, layer - 1) for w in tail] + in_specs
        operands = (o, sg, x, *tail) + operands
        out_specs = [row] + out_specs
        out_shape = [jax.ShapeDtypeStruct(x.shape, F32)] + out_shape
    return pl.pallas_call(
        functools.partial(_mla_proj_kernel, after_layer=prev is not None),
        grid=(B, nt),
        in_specs=in_specs,
        out_specs=out_specs,
        out_shape=out_shape,
        scratch_shapes=[pltpu.VMEM((1, LANES), F32)],
        compiler_params=_cparams(2),
        name="mla_proj" if prev is None else "out_ln_mla_proj",
    )(*operands)


def _mla_attn_kernel(qT_ref, kn_ref, kr_ref, vT_ref, mrow_ref, o_ref, *scratch, nq, tq):
    *p_refs, l_ref, acc_ref = scratch
    n_buf = len(p_refs)

    def shifted_scores(qi, t):
        k0 = pl.multiple_of(t * tq, tq)
        kcat = jnp.concatenate([kn_ref[pl.ds(k0, tq), :], kr_ref[pl.ds(k0, tq), :]], axis=1)
        tail = qT_ref[qi, A_QK:, :] + mrow_ref[jnp.where(t == qi, 1, 0)]
        q_full = jnp.concatenate([qT_ref[qi, :A_QK, :], tail], axis=0)
        return jnp.dot(kcat, q_full, preferred_element_type=F32)

    def weighted_values(qi, t, buf):
        acc_ref[qi] += jnp.dot(vT_ref[t], p_refs[buf][...], preferred_element_type=F32)

    acc_ref[...] = jnp.zeros(acc_ref.shape, F32)
    l_ref[...] = jnp.zeros(l_ref.shape, F32)
    for p_ref in p_refs[-2:]:
        p_ref[...] = jnp.zeros(p_ref.shape, BF16)

    def step(carry, buf):
        qi, t, qi_1, t_1, qi_2, t_2 = carry
        weighted_values(qi_2, t_2, (buf - 2) % n_buf)
        p = jnp.exp2(shifted_scores(qi, t))
        l_ref[qi] += jnp.sum(p, axis=0, keepdims=True)
        p_refs[buf][...] = p.astype(BF16)
        last = t == qi
        return jnp.where(last, qi + 1, qi), jnp.where(last, 0, t + 1), qi, t, qi_1, t_1

    carry = (jnp.int32(0),) * 6
    n_steps = nq * (nq + 1) // 2
    n_lead = n_steps % UNROLL
    for u in range(n_lead):
        carry = step(carry, u % n_buf)

    def steps(it, carry):
        for u in range(UNROLL):
            carry = step(carry, (n_lead + u) % n_buf)
        return carry

    _, _, qi_1, t_1, qi_2, t_2 = lax.fori_loop(0, n_steps // UNROLL, steps, carry)
    weighted_values(qi_2, t_2, (n_steps - 2) % n_buf)
    weighted_values(qi_1, t_1, (n_steps - 1) % n_buf)

    def write_out(qi, acc, l):
        o_ref[qi] = (acc / l).astype(o_ref.dtype)

    def finish(qi, denom_min):
        l = l_ref[qi]
        write_out(qi, acc_ref[qi], l)
        return jnp.minimum(denom_min, l)

    denom_min = lax.fori_loop(0, nq, finish, jnp.full((1, tq), jnp.inf, F32), unroll=4)

    @pl.when(jnp.logical_not(jnp.min(denom_min) >= DENOM_FLOOR))
    def _redo_with_running_max():
        def query_tile(qi, carry):
            def key_tile(t, mla):
                m, l, acc = mla
                s = shifted_scores(qi, t)
                m_new = jnp.maximum(m, jnp.max(s, axis=0, keepdims=True))
                alpha = jnp.exp2(m - m_new)
                p = jnp.exp2(s - m_new)
                l = l * alpha + jnp.sum(p, axis=0, keepdims=True)
                acc = acc * alpha + jnp.dot(vT_ref[t], p.astype(BF16), preferred_element_type=F32)
                return m_new, l, acc

            init = (jnp.full((1, tq), MASK_VALUE, F32), jnp.zeros((1, tq), F32),
                    jnp.zeros((A_VDIM, tq), F32))
            _, l, acc = lax.fori_loop(0, qi + 1, key_tile, init)
            write_out(qi, acc, l)
            return carry

        lax.fori_loop(0, nq, query_tile, 0)


def _mla_attn(qT, kn, kr, vT):
    B, H, nq, _, tq = qT.shape
    S = kn.shape[2]
    assert vT.shape[2:] == (nq, A_VDIM, tq) and tq // CHUNK < A_QPAD - A_QK
    assert UNROLL % WEIGHT_BUFFERS == 0 and WEIGHT_BUFFERS >= 3
    a = np.arange(A_QPAD - A_QK)[:, None]
    qchunk = (np.arange(tq) // CHUNK)[None, :]
    diagonal = np.where((qchunk < a) & (a < tq // CHUNK), MASK_VALUE, 0.0).astype(np.float32)
    mrow = jnp.asarray(np.stack([np.zeros_like(diagonal), diagonal]), BF16)
    return pl.pallas_call(
        functools.partial(_mla_attn_kernel, nq=nq, tq=tq),
        grid=(B, H),
        in_specs=[
            pl.BlockSpec((None, None, nq, A_QPAD, tq), lambda b, h: (b, h, 0, 0, 0)),
            pl.BlockSpec((None, None, S, A_NOPE), lambda b, h: (b, h, 0, 0)),
            pl.BlockSpec((None, S, LANES), lambda b, h: (b, 0, 0)),
            pl.BlockSpec((None, None, nq, A_VDIM, tq), lambda b, h: (b, h, 0, 0, 0)),
            pl.BlockSpec(mrow.shape, lambda b, h: (0, 0, 0)),
        ],
        out_specs=pl.BlockSpec((None, None, nq, A_VDIM, tq), lambda b, h: (b, h, 0, 0, 0)),
        out_shape=jax.ShapeDtypeStruct((B, H, nq, A_VDIM, tq), BF16),
        scratch_shapes=[
            *([pltpu.VMEM((tq, tq), BF16)] * WEIGHT_BUFFERS),
            pltpu.VMEM((nq, 1, tq), F32),
            pltpu.VMEM((nq, A_VDIM, tq), F32),
        ],
        compiler_params=_cparams(2),
        name="mla_attn",
    )(qT, kn, kr, vT, mrow)


def _gated_out_ln(o_ref, sg_ref, x_ref, wo_ref, g_ref, b_ref, xo_ref):
    half = xo_ref.shape[0] // 2
    for rows in (slice(0, half), slice(half, 2 * half)):
        o = jnp.concatenate(
            [o_ref[hh, :, rows].astype(F32).T for hh in range(o_ref.shape[0])], axis=1)
        y = jnp.dot((o * sg_ref[rows, :]).astype(BF16), wo_ref[...], preferred_element_type=F32)
        xo_ref[rows, :] = _deepnorm_ln(x_ref[rows, :], y, g_ref[...], b_ref[...])


def _out_ln_kv_kernel(o_ref, sg_ref, x_ref, wo_ref, g_ref, b_ref, wk_ref, wvT_ref,
                      xo_ref, ks_ref, vsT_ref):
    _gated_out_ln(o_ref, sg_ref, x_ref, wo_ref, g_ref, b_ref, xo_ref)
    xb = xo_ref[...].astype(BF16)
    ks_ref[...] = jnp.dot(xb, wk_ref[...], preferred_element_type=F32).astype(BF16)
    vT = lax.dot_general(wvT_ref[...], xb, _NT, preferred_element_type=F32).astype(BF16)
    ones = (lax.broadcasted_iota(jnp.int32, (B_VROWS - B_HEAD_DIM, vT.shape[1]), 0) == 0).astype(BF16)
    vT = jnp.concatenate(
        [part for g in range(B_KV_HEADS) for part in (vT[g * B_HEAD_DIM:(g + 1) * B_HEAD_DIM], ones)],
        axis=0)
    for n in range(vsT_ref.shape[0]):
        vsT_ref[n] = vT[:, n * Q_BLOCK:(n + 1) * Q_BLOCK]


def _out_ln_kv(o, sg, x, layer, params, wk, wvT, tm):
    B, S, _ = x.shape
    row = pl.BlockSpec((None, tm, D_MODEL), lambda b_, i: (b_, i, 0))
    const = lambda *shape: pl.BlockSpec(shape, lambda b_, i: (0,) * len(shape))
    heads = pl.BlockSpec((None, A_HEADS, None, A_VDIM, tm), lambda b_, i: (b_, 0, i, 0, 0))
    tail = [params[k] for k in ("w_o", "ln_g", "ln_b")]
    nb = tm // Q_BLOCK
    return pl.pallas_call(
        _out_ln_kv_kernel, grid=(B, S // tm),
        in_specs=[heads, row, row] + [_layer_block(w, layer) for w in tail]
                 + [const(*wk.shape), const(*wvT.shape)],
        out_specs=[
            row,
            pl.BlockSpec((None, tm, B_KV_WIDTH), lambda b_, i: (b_, i, 0)),
            pl.BlockSpec((None, nb, B_KV_HEADS * B_VROWS, Q_BLOCK), lambda b_, i: (b_, i, 0, 0)),
        ],
        out_shape=[
            jax.ShapeDtypeStruct(x.shape, F32),
            jax.ShapeDtypeStruct((B, S, B_KV_WIDTH), BF16),
            jax.ShapeDtypeStruct((B, S // Q_BLOCK, B_KV_HEADS * B_VROWS, Q_BLOCK), BF16),
        ],
        compiler_params=_cparams(2), name="out_ln_kv",
    )(o, sg, x, *tail, wk, wvT)


def _band_bias_kernel(tab_ref, bkt_ref, ok_ref, out_ref):
    bkt = bkt_ref[...]
    ok = ok_ref[...] != 0
    has_prev = lax.broadcasted_iota(jnp.int32, bkt.shape, 0) >= Q_BLOCK
    for h in range(B_Q_HEADS):
        acc = jnp.zeros(bkt.shape, F32)
        for bucket in range(NUM_BUCKETS):
            acc = jnp.where(bkt == bucket, tab_ref[bucket, h], acc)
        generic = jnp.where(ok, acc * LOG2_E, MASK_VALUE)
        g, hi = divmod(h, B_GROUP)
        out_ref[0, g, :, hi * Q_BLOCK:(hi + 1) * Q_BLOCK] = jnp.where(has_prev, generic, MASK_VALUE)
        out_ref[1, g, :, hi * Q_BLOCK:(hi + 1) * Q_BLOCK] = generic


def _t5_bucket_map(rel):
    half = NUM_BUCKETS // 2
    ret = jnp.where(rel > 0, half, 0)
    n = jnp.abs(rel)
    max_exact = half // 2
    large = max_exact + (jnp.log(jnp.maximum(n, 1).astype(F32) / max_exact)
                         / math.log(MAX_DISTANCE / max_exact) * (half - max_exact)).astype(jnp.int32)
    large = jnp.minimum(large, half - 1)
    return ret + jnp.where(n < max_exact, n, large)


def _band_bias(rel_bias_table):
    j = jnp.arange(2 * Q_BLOCK)[:, None]
    r = jnp.arange(Q_BLOCK)[None, :]
    rel = j - Q_BLOCK - r
    d = (r // CHUNK) - ((j - Q_BLOCK) // CHUNK)
    ok = ((d >= 0) & (d <= WIN_CHUNKS)).astype(jnp.int32)
    bkt = _t5_bucket_map(rel).astype(jnp.int32)
    return pl.pallas_call(
        _band_bias_kernel,
        in_specs=[
            pl.BlockSpec(memory_space=pltpu.SMEM),
            pl.BlockSpec(memory_space=pltpu.VMEM),
            pl.BlockSpec(memory_space=pltpu.VMEM),
        ],
        out_specs=pl.BlockSpec(memory_space=pltpu.VMEM),
        out_shape=jax.ShapeDtypeStruct((2, B_KV_HEADS, 2 * Q_BLOCK, B_GROUP * Q_BLOCK), F32),
        name="band_bias",
    )(rel_bias_table, bkt, ok)


def _swa_layer_kernel(xn_ref, x_ref, wqT_ref, wg_ref, wo_ref, ks_ref, vsT_ref, bias_ref, sink_ref,
                      g_ref, b_ref, xo_ref, qT0_ref, sg0_ref, qT1_ref, sg1_ref, og_ref, s_ref, *, nblk):
    step = pl.program_id(1)

    @pl.when((pl.program_id(0) == 0) & (step == 0))
    def _first_step():
        qT0_ref[...] = jnp.zeros(qT0_ref.shape, qT0_ref.dtype)
        sg0_ref[...] = jnp.zeros(sg0_ref.shape, sg0_ref.dtype)

    def body(qT_ref, sg_ref, qT_next_ref, sg_next_ref):
        tile = jnp.maximum(step - 1, 0)
        xb_next = xn_ref[...].astype(BF16)
        zeros = jnp.zeros((B_HEAD_DIM, B_GROUP * Q_BLOCK), BF16)
        chunk = B_WIDTH // nblk
        half = max(nblk // 2, 1)

        for n in range(nblk):
            blk = tile * nblk + n
            cur = pl.multiple_of(blk * Q_BLOCK, Q_BLOCK)
            prev_blk = jnp.maximum(blk - 1, 0)
            prev = pl.multiple_of(prev_blk * Q_BLOCK, Q_BLOCK)
            kband = jnp.concatenate(
                [ks_ref[pl.ds(prev, Q_BLOCK), :], ks_ref[pl.ds(cur, Q_BLOCK), :]], axis=0)
            vband = jnp.concatenate([vsT_ref[prev_blk], vsT_ref[blk]], axis=1)
            table = jnp.minimum(blk, 1)
            lanes = slice(n * Q_BLOCK, (n + 1) * Q_BLOCK)
            score_max = []
            for g in range(B_KV_HEADS):
                qg = jnp.concatenate(
                    [qT_ref[(g * B_GROUP + hi) * B_HEAD_DIM:(g * B_GROUP + hi + 1) * B_HEAD_DIM, lanes]
                     for hi in range(B_GROUP)], axis=1)
                rhs = jnp.concatenate([qg, zeros] if g == 0 else [zeros, qg], axis=0)
                sT = jnp.dot(kband, rhs, preferred_element_type=F32) + bias_ref[table, g]
                s_ref[g] = sT
                score_max.append(jnp.max(sT, axis=0, keepdims=True))

            part = slice(n * chunk, (n + 1) * chunk)
            qT_next_ref[part, :] = (
                lax.dot_general(wqT_ref[part, :], xb_next, _NT, preferred_element_type=F32)
                * (B_HEAD_DIM ** -0.5 * LOG2_E)).astype(BF16)
            gate = jnp.dot(xb_next, wg_ref[:, part], preferred_element_type=F32)
            sg_next_ref[:, part] = gate * _sigmoid(gate)

            o_parts = []
            for g in range(B_KV_HEADS):
                sink = sink_ref[g]
                m = jnp.maximum(score_max[g], sink)
                e = jnp.exp2(s_ref[g] - m).astype(BF16)
                v_rows = slice(g * B_VROWS, (g + 1) * B_VROWS)
                oT = jnp.dot(vband[v_rows, :], e, preferred_element_type=F32)
                den = oT[B_HEAD_DIM:B_HEAD_DIM + 1] + jnp.exp2(sink - m)
                o_parts.append(oT[:B_HEAD_DIM] / den)
            for hi in range(B_GROUP):
                cols = slice(hi * Q_BLOCK, (hi + 1) * Q_BLOCK)
                o_hi = jnp.concatenate([o_parts[0][:, cols], o_parts[1][:, cols]], axis=0).T
                og_ref[lanes, cols] = (o_hi * sg_ref[lanes, cols]).astype(BF16)

            if (n + 1) % half == 0:
                rows = slice((n + 1 - half) * Q_BLOCK, (n + 1) * Q_BLOCK)
                y = jnp.dot(og_ref[rows, :], wo_ref[...], preferred_element_type=F32)
                xo_ref[rows, :] = _deepnorm_ln(x_ref[rows, :], y, g_ref[...], b_ref[...])

    @pl.when(step % 2 == 0)
    def _even():
        body(qT0_ref, sg0_ref, qT1_ref, sg1_ref)

    @pl.when(step % 2 == 1)
    def _odd():
        body(qT1_ref, sg1_ref, qT0_ref, sg0_ref)


def _swa_layer(x, j, params, ks, vsT, bias, tm):
    B, S, _ = x.shape
    nt = S // tm
    nblk = tm // Q_BLOCK
    once = pl.Buffered(1)
    done = pl.BlockSpec((None, tm, D_MODEL), lambda b_, i: (b_, jnp.maximum(i - 1, 0), 0))
    weights = [params[k] for k in ("wqT", "wg", "w_o")]
    small = [params[k] for k in ("sink", "ln_g", "ln_b")]
    return pl.pallas_call(
        functools.partial(_swa_layer_kernel, nblk=nblk),
        grid=(B, nt + 1),
        in_specs=[
            pl.BlockSpec((None, tm, D_MODEL), lambda b_, i: (b_, jnp.minimum(i, nt - 1), 0)),
            done, *[_layer_block(w, j) for w in weights],
            pl.BlockSpec((None, S, B_KV_WIDTH), lambda b_, i: (b_, 0, 0), pipeline_mode=once),
            pl.BlockSpec((None, S // Q_BLOCK, B_KV_HEADS * B_VROWS, Q_BLOCK), lambda b_, i: (b_, 0, 0, 0),
                         pipeline_mode=once),
            pl.BlockSpec(bias.shape, lambda b_, i: (0,) * bias.ndim, pipeline_mode=once),
            *[_layer_block(w, j) for w in small],
        ],
        out_specs=done,
        out_shape=jax.ShapeDtypeStruct(x.shape, F32),
        scratch_shapes=[
            pltpu.VMEM((B_WIDTH, tm), BF16), pltpu.VMEM((tm, B_WIDTH), F32),
            pltpu.VMEM((B_WIDTH, tm), BF16), pltpu.VMEM((tm, B_WIDTH), F32),
            pltpu.VMEM((tm, B_WIDTH), BF16),
            pltpu.VMEM((B_KV_HEADS, 2 * Q_BLOCK, B_GROUP * Q_BLOCK), F32),
        ],
        compiler_params=_cparams(2),
        name="swa_layer",
    )(x, x, *weights, ks, vsT, bias, *small)


def _rope_tables(S):
    inv = ROPE_THETA ** (-jnp.arange(ROPE_HALF, dtype=F32) / ROPE_HALF)
    ang = jnp.arange(S, dtype=F32)[:, None] * inv[None, :]
    cos, sin = jnp.cos(ang), jnp.sin(ang)
    zeros = jnp.zeros((S, LANES - A_ROPE), F32)
    z32 = jnp.zeros((S, ROPE_HALF), F32)
    cosk = jnp.concatenate([cos, cos, zeros], axis=1)
    sinka = jnp.concatenate([-sin, z32, zeros], axis=1)
    sinkb = jnp.concatenate([z32, sin, zeros], axis=1)
    return cos.T, sin.T, cosk, sinka, sinkb


def _mla_params(w_in, q_norm, kv_norm, w_uq, w_ukv, w_o, ln_gain, ln_bias):
    n = w_in.shape[0]
    o_kr = A_Q_LORA + A_KV_LORA
    w_in_pad = jnp.concatenate(
        [w_in[..., :o_kr + A_ROPE], jnp.zeros((n, D_MODEL, LANES - A_ROPE), w_in.dtype),
         w_in[..., o_kr + A_ROPE:]], axis=-1)
    w_ukv_h = w_ukv.reshape(n, A_KV_LORA, A_HEADS, A_NOPE + A_VDIM)
    w_kn = w_ukv_h[..., :A_NOPE].reshape(n, A_KV_LORA, A_HEADS * A_NOPE)
    w_v = w_ukv_h[..., A_NOPE:].reshape(n, A_KV_LORA, A_WIDTH)
    return {
        "w_in": w_in_pad.astype(BF16), "gq": q_norm[:, None, :], "gkv": kv_norm[:, None, :],
        "w_uqT": jnp.swapaxes(w_uq, 1, 2).astype(BF16), "w_kn": w_kn.astype(BF16),
        "w_vT": jnp.swapaxes(w_v, 1, 2).astype(BF16), "w_o": w_o.astype(BF16),
        "ln_g": ln_gain[:n, None, :], "ln_b": ln_bias[:n, None, :],
    }


def _swa_feature_order(w, axis):
    shape = w.shape
    split = shape[:axis] + (B_KV_HEADS, B_GROUP, B_HEAD_DIM) + shape[axis + 1:]
    return jnp.swapaxes(w.reshape(split), axis, axis + 1).reshape(shape)


def _swa_params(w_in, sinks, w_o, ln_gain, ln_bias):
    n = w_in.shape[0]
    sink = sinks.astype(F32).reshape(n, B_KV_HEADS, 1, B_GROUP) * LOG2_E
    return {
        "wqT": jnp.swapaxes(w_in[..., :B_WIDTH], 1, 2).astype(BF16),
        "wg": _swa_feature_order(w_in[..., B_WIDTH:], 2).astype(BF16),
        "w_o": _swa_feature_order(w_o, 1).astype(BF16),
        "sink": jnp.repeat(sink, Q_BLOCK, axis=3),
        "ln_g": ln_gain[-n:, None, :], "ln_b": ln_bias[-n:, None, :],
    }


def kernel(x, w_in_a, q_norm_a, kv_norm_a, w_uq_a, w_ukv_a, w_o_a, w_kv_shared, w_in_b, sinks_b,
           w_o_b, rel_bias_table, ln_gain, ln_bias):
    B, S, _ = x.shape
    tm = min(512, S)
    tq = min(512, S)
    rope = _rope_tables(S)

    mla = _mla_params(w_in_a, q_norm_a, kv_norm_a, w_uq_a, w_ukv_a, w_o_a, ln_gain, ln_bias)
    prev = None
    for i in range(N_A_LAYERS):
        outs = _mla_proj(x, i, mla, rope, tm, tq, tq, prev)
        if prev is not None:
            x, *outs = outs
        qT, kn, kr, vT, sg = outs
        prev = (_mla_attn(qT, kn, kr, vT), sg)
    x, ks, vsT = _out_ln_kv(*prev, x, N_A_LAYERS - 1, mla, w_kv_shared[:, :B_KV_WIDTH].astype(BF16),
                            w_kv_shared[:, B_KV_WIDTH:].T.astype(BF16), tm)

    bias = _band_bias(rel_bias_table)
    swa = _swa_params(w_in_b, sinks_b, w_o_b, ln_gain, ln_bias)
    for j in range(DEPTH - N_A_LAYERS):
        x = _swa_layer(x, j, swa, ks, vsT, bias, min(512, S))
    return x
```

```python
import functools
import math

import jax
import jax.numpy as jnp
import numpy as np
from jax import lax
from jax.experimental import pallas as pl
from jax.experimental.pallas import tpu as pltpu

F32 = jnp.float32
BF16 = jnp.bfloat16

D_MODEL = 1024
DEPTH = 4
CHUNK = 64
Q_BLOCK = 128
N_A_LAYERS = DEPTH // 2
A_HEADS = 8
A_NOPE = 128
A_ROPE = 64
A_VDIM = 128
A_QK = A_NOPE + A_ROPE
A_Q_LORA = 384
A_KV_LORA = 256
A_WIDTH = A_HEADS * A_VDIM
A_QPAD = 256
ROPE_THETA = 10000.0
ROPE_HALF = A_ROPE // 2
B_Q_HEADS = 16
B_KV_HEADS = 2
B_GROUP = B_Q_HEADS // B_KV_HEADS
B_HEAD_DIM = 64
B_WIDTH = B_Q_HEADS * B_HEAD_DIM
B_KV_WIDTH = B_KV_HEADS * B_HEAD_DIM
BF16_TILE_ROWS = 16
B_VROWS = B_HEAD_DIM + BF16_TILE_ROWS
WINDOW = 128
WIN_CHUNKS = WINDOW // CHUNK
NUM_BUCKETS = 32
MAX_DISTANCE = 128
DEEPNORM_ALPHA = (2 * DEPTH) ** 0.25
NORM_EPS = 1e-5
MASK_VALUE = -1e30
LOG2_E = math.log2(math.e)

LANES = 128
VMEM_LIMIT_BYTES = 56 * 1024 * 1024

_NT = (((1,), (1,)), ((), ()))
UNROLL = 68
WEIGHT_BUFFERS = 4
BOUND_MARGIN = 1.0 + 2.0 ** -6
DENOM_FLOOR = 2.0 ** -80


def _cparams(n_axes):
    return pltpu.CompilerParams(
        dimension_semantics=("arbitrary",) * n_axes, vmem_limit_bytes=VMEM_LIMIT_BYTES)


def _sigmoid(x):
    return 1.0 / (1.0 + jnp.exp(-x))


def _rms_scale(c, g):
    return c * lax.rsqrt(jnp.mean(c * c, axis=-1, keepdims=True) + NORM_EPS) * g


def _deepnorm_ln(x, y, g, b):
    z = DEEPNORM_ALPHA * x + y
    mu = jnp.mean(z, axis=-1, keepdims=True)
    zc = z - mu
    var = jnp.mean(zc * zc, axis=-1, keepdims=True)
    return zc * lax.rsqrt(var + NORM_EPS) * g + b


def _mla_proj_kernel(*refs, after_layer):
    if after_layer:
        o_ref, sg_prev_ref, x_prev_ref, wo_ref, g_ref, b_ref, *refs = refs
        *refs, xo_ref, qT_ref, kn_ref, kr_ref, vT_ref, sg_ref, kmax_ref = refs
    else:
        x_ref, *refs = refs
        *refs, qT_ref, kn_ref, kr_ref, vT_ref, sg_ref, kmax_ref = refs
    (w_in_ref, gq_ref, gkv_ref, w_uqT_ref, w_kn_ref, w_vT_ref,
     cosT_ref, sinT_ref, cosk_ref, sinka_ref, sinkb_ref, ktail_ref) = refs

    @pl.when(pl.program_id(1) == 0)
    def _start_of_sequence():
        kmax_ref[...] = jnp.zeros(kmax_ref.shape, F32)

    if after_layer:
        _gated_out_ln(o_ref, sg_prev_ref, x_prev_ref, wo_ref, g_ref, b_ref, xo_ref)
        x_ref = xo_ref
    xb = x_ref[...].astype(BF16)
    h = jnp.dot(xb, w_in_ref[...], preferred_element_type=F32)
    o_kv = A_Q_LORA
    o_kr = A_Q_LORA + A_KV_LORA
    o_g = o_kr + LANES
    cq = _rms_scale(h[:, :o_kv], gq_ref[...]).astype(BF16)
    ckv = _rms_scale(h[:, o_kv:o_kr], gkv_ref[...]).astype(BF16)

    scale = A_QK ** -0.5 * math.log2(math.e)
    qT = lax.dot_general(w_uqT_ref[...], cq, _NT, preferred_element_type=F32) * scale

    krs = h[:, o_kr:o_g]
    kr = (krs * cosk_ref[...]
          + pltpu.roll(krs, LANES - ROPE_HALF, axis=1) * sinka_ref[...]
          + pltpu.roll(krs, ROPE_HALF, axis=1) * sinkb_ref[...])
    kr_ref[...] = (kr + ktail_ref[...]).astype(BF16)
    kn = jnp.dot(ckv, w_kn_ref[...], preferred_element_type=F32)

    kr2 = jnp.sum(kr * kr, axis=1, keepdims=True)
    lane = lax.broadcasted_iota(jnp.int32, kmax_ref.shape, 1)
    kmax2 = kmax_ref[...]
    for hh in range(A_HEADS):
        knh = kn[:, hh * A_NOPE:(hh + 1) * A_NOPE]
        k2 = jnp.max(jnp.sum(knh * knh, axis=1, keepdims=True) + kr2, axis=0, keepdims=True)
        kmax2 = jnp.where(lane == hh, jnp.maximum(kmax2, k2), kmax2)
    kmax_ref[...] = kmax2

    cos = cosT_ref[...]
    sin = sinT_ref[...]
    n_r1 = A_NOPE
    n_r2 = n_r1 + ROPE_HALF
    n_r3 = n_r2 + ROPE_HALF
    tail_row = lax.broadcasted_iota(jnp.int32, (A_QPAD - A_QK, qT.shape[1]), 0)
    bound_row = vT_ref.shape[-1] // CHUNK
    for hh in range(A_HEADS):
        r0 = hh * A_QK
        t1 = qT[r0 + n_r1:r0 + n_r2]
        t2 = qT[r0 + n_r2:r0 + n_r3]
        qT_ref[hh, :n_r1, :] = qT[r0:r0 + n_r1].astype(BF16)
        qT_ref[hh, n_r1:n_r2, :] = (t1 * cos - t2 * sin).astype(BF16)
        qT_ref[hh, n_r2:n_r3, :] = (t2 * cos + t1 * sin).astype(BF16)
        qh = qT[r0:r0 + A_QK]
        q2 = jnp.sum(qh * qh, axis=0, keepdims=True)
        bound = jnp.sqrt(q2 * kmax2[:, hh:hh + 1]) * BOUND_MARGIN
        qT_ref[hh, n_r3:, :] = jnp.where(tail_row == bound_row, -bound, 0.0).astype(BF16)

    kn = kn.astype(BF16)
    vT = lax.dot_general(w_vT_ref[...], ckv, _NT, preferred_element_type=F32).astype(BF16)
    tk = vT_ref.shape[-1]
    for hh in range(A_HEADS):
        kn_ref[hh] = kn[:, hh * A_NOPE:(hh + 1) * A_NOPE]
        for n in range(vT_ref.shape[1]):
            vT_ref[hh, n] = vT[hh * A_VDIM:(hh + 1) * A_VDIM, n * tk:(n + 1) * tk]

    gate = h[:, o_g:]
    sg_ref[...] = (gate * _sigmoid(gate)).astype(BF16)


def _layer_block(stacked, layer):
    zeros = (0,) * (stacked.ndim - 1)
    return pl.BlockSpec((None,) + stacked.shape[1:], lambda *_: (layer,) + zeros,
                        pipeline_mode=pl.Buffered(1))


def _mla_proj(x, layer, params, rope, tm, tq, tk, prev=None):
    B, S, _ = x.shape
    nt = S // tm
    cosT, sinT, cosk, sinka, sinkb = rope
    weights = [params[k] for k in ("w_in", "gq", "gkv", "w_uqT", "w_kn", "w_vT")]
    key = np.arange(S)[:, None]
    lane = np.arange(LANES)[None, :] - A_ROPE
    ktail = jnp.asarray((lane == (key % tk) // CHUNK) | (lane == tk // CHUNK), F32)
    tok_lanes = pl.BlockSpec((tm, LANES), lambda b, i: (i, 0))
    row = pl.BlockSpec((None, tm, D_MODEL), lambda b, i: (b, i, 0))
    in_specs = [_layer_block(w, layer) for w in weights] + [
        pl.BlockSpec((ROPE_HALF, tm), lambda b, i: (0, i)),
        pl.BlockSpec((ROPE_HALF, tm), lambda b, i: (0, i)),
        tok_lanes, tok_lanes, tok_lanes, tok_lanes,
    ]
    operands = (*weights, cosT, sinT, cosk, sinka, sinkb, ktail)
    out_specs = [
        pl.BlockSpec((None, A_HEADS, None, A_QPAD, tm),
                     lambda b, i: (b, 0, i // (tq // tm), 0, i % (tq // tm))),
        pl.BlockSpec((None, A_HEADS, tm, A_NOPE), lambda b, i: (b, 0, i, 0)),
        pl.BlockSpec((None, tm, LANES), lambda b, i: (b, i, 0)),
        pl.BlockSpec((None, A_HEADS, tm // tk, A_VDIM, tk), lambda b, i: (b, 0, i, 0, 0)),
        pl.BlockSpec((None, tm, A_WIDTH), lambda b, i: (b, i, 0)),
    ]
    out_shape = [
        jax.ShapeDtypeStruct((B, A_HEADS, S // tq, A_QPAD, tq), BF16),
        jax.ShapeDtypeStruct((B, A_HEADS, S, A_NOPE), BF16),
        jax.ShapeDtypeStruct((B, S, LANES), BF16),
        jax.ShapeDtypeStruct((B, A_HEADS, S // tk, A_VDIM, tk), BF16),
        jax.ShapeDtypeStruct((B, S, A_WIDTH), BF16),
    ]
    if prev is None:
        in_specs = [row] + in_specs
        operands = (x,) + operands
    else:
        o, sg = prev
        tail = [params[k] for k in ("w_o", "ln_g", "ln_b")]
        heads = pl.BlockSpec((None, A_HEADS, None, A_VDIM, tm), lambda b_, i: (b_, 0, i, 0, 0))
        in_specs = [heads, row, row] + [_layer_block(w, layer - 1) for w in tail] + in_specs
        operands = (o, sg, x, *tail) + operands
        out_specs = [row] + out_specs
        out_shape = [jax.ShapeDtypeStruct(x.shape, F32)] + out_shape
    return pl.pallas_call(
        functools.partial(_mla_proj_kernel, after_layer=prev is not None),
        grid=(B, nt),
        in_specs=in_specs,
        out_specs=out_specs,
        out_shape=out_shape,
        scratch_shapes=[pltpu.VMEM((1, LANES), F32)],
        compiler_params=_cparams(2),
        name="mla_proj" if prev is None else "out_ln_mla_proj",
    )(*operands)


def _mla_attn_kernel(qT_ref, kn_ref, kr_ref, vT_ref, mrow_ref, o_ref, *scratch, nq, tq):
    *p_refs, l_ref, acc_ref = scratch
    n_buf = len(p_refs)

    def shifted_scores(qi, t):
        k0 = pl.multiple_of(t * tq, tq)
        kcat = jnp.concatenate([kn_ref[pl.ds(k0, tq), :], kr_ref[pl.ds(k0, tq), :]], axis=1)
        tail = qT_ref[qi, A_QK:, :] + mrow_ref[jnp.where(t == qi, 1, 0)]
        q_full = jnp.concatenate([qT_ref[qi, :A_QK, :], tail], axis=0)
        return jnp.dot(kcat, q_full, preferred_element_type=F32)

    def weighted_values(qi, t, buf):
        acc_ref[qi] += jnp.dot(vT_ref[t], p_refs[buf][...], preferred_element_type=F32)

    acc_ref[...] = jnp.zeros(acc_ref.shape, F32)
    l_ref[...] = jnp.zeros(l_ref.shape, F32)
    for p_ref in p_refs[-2:]:
        p_ref[...] = jnp.zeros(p_ref.shape, BF16)

    def step(carry, buf):
        qi, t, qi_1, t_1, qi_2, t_2 = carry
        weighted_values(qi_2, t_2, (buf - 2) % n_buf)
        p = jnp.exp2(shifted_scores(qi, t))
        l_ref[qi] += jnp.sum(p, axis=0, keepdims=True)
        p_refs[buf][...] = p.astype(BF16)
        last = t == qi
        return jnp.where(last, qi + 1, qi), jnp.where(last, 0, t + 1), qi, t, qi_1, t_1

    carry = (jnp.int32(0),) * 6
    n_steps = nq * (nq + 1) // 2
    n_lead = n_steps % UNROLL
    for u in range(n_lead):
        carry = step(carry, u % n_buf)

    def steps(it, carry):
        for u in range(UNROLL):
            carry = step(carry, (n_lead + u) % n_buf)
        return carry

    _, _, qi_1, t_1, qi_2, t_2 = lax.fori_loop(0, n_steps // UNROLL, steps, carry)
    weighted_values(qi_2, t_2, (n_steps - 2) % n_buf)
    weighted_values(qi_1, t_1, (n_steps - 1) % n_buf)

    def write_out(qi, acc, l):
        o_ref[qi] = (acc / l).astype(o_ref.dtype)

    def finish(qi, denom_min):
        l = l_ref[qi]
        write_out(qi, acc_ref[qi], l)
        return jnp.minimum(denom_min, l)

    denom_min = lax.fori_loop(0, nq, finish, jnp.full((1, tq), jnp.inf, F32), unroll=4)

    @pl.when(jnp.logical_not(jnp.min(denom_min) >= DENOM_FLOOR))
    def _redo_with_running_max():
        def query_tile(qi, carry):
            def key_tile(t, mla):
                m, l, acc = mla
                s = shifted_scores(qi, t)
                m_new = jnp.maximum(m, jnp.max(s, axis=0, keepdims=True))
                alpha = jnp.exp2(m - m_new)
                p = jnp.exp2(s - m_new)
                l = l * alpha + jnp.sum(p, axis=0, keepdims=True)
                acc = acc * alpha + jnp.dot(vT_ref[t], p.astype(BF16), preferred_element_type=F32)
                return m_new, l, acc

            init = (jnp.full((1, tq), MASK_VALUE, F32), jnp.zeros((1, tq), F32),
                    jnp.zeros((A_VDIM, tq), F32))
            _, l, acc = lax.fori_loop(0, qi + 1, key_tile, init)
            write_out(qi, acc, l)
            return carry

        lax.fori_loop(0, nq, query_tile, 0)


def _mla_attn(qT, kn, kr, vT):
    B, H, nq, _, tq = qT.shape
    S = kn.shape[2]
    assert vT.shape[2:] == (nq, A_VDIM, tq) and tq // CHUNK < A_QPAD - A_QK
    assert UNROLL % WEIGHT_BUFFERS == 0 and WEIGHT_BUFFERS >= 3
    a = np.arange(A_QPAD - A_QK)[:, None]
    qchunk = (np.arange(tq) // CHUNK)[None, :]
    diagonal = np.where((qchunk < a) & (a < tq // CHUNK), MASK_VALUE, 0.0).astype(np.float32)
    mrow = jnp.asarray(np.stack([np.zeros_like(diagonal), diagonal]), BF16)
    return pl.pallas_call(
        functools.partial(_mla_attn_kernel, nq=nq, tq=tq),
        grid=(B, H),
        in_specs=[
            pl.BlockSpec((None, None, nq, A_QPAD, tq), lambda b, h: (b, h, 0, 0, 0)),
            pl.BlockSpec((None, None, S, A_NOPE), lambda b, h: (b, h, 0, 0)),
            pl.BlockSpec((None, S, LANES), lambda b, h: (b, 0, 0)),
            pl.BlockSpec((None, None, nq, A_VDIM, tq), lambda b, h: (b, h, 0, 0, 0)),
            pl.BlockSpec(mrow.shape, lambda b, h: (0, 0, 0)),
        ],
        out_specs=pl.BlockSpec((None, None, nq, A_VDIM, tq), lambda b, h: (b, h, 0, 0, 0)),
        out_shape=jax.ShapeDtypeStruct((B, H, nq, A_VDIM, tq), BF16),
        scratch_shapes=[
            *([pltpu.VMEM((tq, tq), BF16)] * WEIGHT_BUFFERS),
            pltpu.VMEM((nq, 1, tq), F32),
            pltpu.VMEM((nq, A_VDIM, tq), F32),
        ],
        compiler_params=_cparams(2),
        name="mla_attn",
    )(qT, kn, kr, vT, mrow)


def _gated_out_ln(o_ref, sg_ref, x_ref, wo_ref, g_ref, b_ref, xo_ref):
    half = xo_ref.shape[0] // 2
    for rows in (slice(0, half), slice(half, 2 * half)):
        o = jnp.concatenate(
            [o_ref[hh, :, rows].astype(F32).T for hh in range(o_ref.shape[0])], axis=1)
        y = jnp.dot((o * sg_ref[rows, :]).astype(BF16), wo_ref[...], preferred_element_type=F32)
        xo_ref[rows, :] = _deepnorm_ln(x_ref[rows, :], y, g_ref[...], b_ref[...])


def _out_ln_kv_kernel(o_ref, sg_ref, x_ref, wo_ref, g_ref, b_ref, wk_ref, wvT_ref,
                      xo_ref, ks_ref, vsT_ref):
    _gated_out_ln(o_ref, sg_ref, x_ref, wo_ref, g_ref, b_ref, xo_ref)
    xb = xo_ref[...].astype(BF16)
    ks_ref[...] = jnp.dot(xb, wk_ref[...], preferred_element_type=F32).astype(BF16)
    vT = lax.dot_general(wvT_ref[...], xb, _NT, preferred_element_type=F32).astype(BF16)
    ones = (lax.broadcasted_iota(jnp.int32, (B_VROWS - B_HEAD_DIM, vT.shape[1]), 0) == 0).astype(BF16)
    vT = jnp.concatenate(
        [part for g in range(B_KV_HEADS) for part in (vT[g * B_HEAD_DIM:(g + 1) * B_HEAD_DIM], ones)],
        axis=0)
    for n in range(vsT_ref.shape[0]):
        vsT_ref[n] = vT[:, n * Q_BLOCK:(n + 1) * Q_BLOCK]


def _out_ln_kv(o, sg, x, layer, params, wk, wvT, tm):
    B, S, _ = x.shape
    row = pl.BlockSpec((None, tm, D_MODEL), lambda b_, i: (b_, i, 0))
    const = lambda *shape: pl.BlockSpec(shape, lambda b_, i: (0,) * len(shape))
    heads = pl.BlockSpec((None, A_HEADS, None, A_VDIM, tm), lambda b_, i: (b_, 0, i, 0, 0))
    tail = [params[k] for k in ("w_o", "ln_g", "ln_b")]
    nb = tm // Q_BLOCK
    return pl.pallas_call(
        _out_ln_kv_kernel, grid=(B, S // tm),
        in_specs=[heads, row, row] + [_layer_block(w, layer) for w in tail]
                 + [const(*wk.shape), const(*wvT.shape)],
        out_specs=[
            row,
            pl.BlockSpec((None, tm, B_KV_WIDTH), lambda b_, i: (b_, i, 0)),
            pl.BlockSpec((None, nb, B_KV_HEADS * B_VROWS, Q_BLOCK), lambda b_, i: (b_, i, 0, 0)),
        ],
        out_shape=[
            jax.ShapeDtypeStruct(x.shape, F32),
            jax.ShapeDtypeStruct((B, S, B_KV_WIDTH), BF16),
            jax.ShapeDtypeStruct((B, S // Q_BLOCK, B_KV_HEADS * B_VROWS, Q_BLOCK), BF16),
        ],
        compiler_params=_cparams(2), name="out_ln_kv",
    )(o, sg, x, *tail, wk, wvT)


def _band_bias_kernel(tab_ref, bkt_ref, ok_ref, out_ref):
    bkt = bkt_ref[...]
    ok = ok_ref[...] != 0
    has_prev = lax.broadcasted_iota(jnp.int32, bkt.shape, 0) >= Q_BLOCK
    for h in range(B_Q_HEADS):
        acc = jnp.zeros(bkt.shape, F32)
        for bucket in range(NUM_BUCKETS):
            acc = jnp.where(bkt == bucket, tab_ref[bucket, h], acc)
        generic = jnp.where(ok, acc * LOG2_E, MASK_VALUE)
        g, hi = divmod(h, B_GROUP)
        out_ref[0, g, :, hi * Q_BLOCK:(hi + 1) * Q_BLOCK] = jnp.where(has_prev, generic, MASK_VALUE)
        out_ref[1, g, :, hi * Q_BLOCK:(hi + 1) * Q_BLOCK] = generic


def _t5_bucket_map(rel):
    half = NUM_BUCKETS // 2
    ret = jnp.where(rel > 0, half, 0)
    n = jnp.abs(rel)
    max_exact = half // 2
    large = max_exact + (jnp.log(jnp.maximum(n, 1).astype(F32) / max_exact)
                         / math.log(MAX_DISTANCE / max_exact) * (half - max_exact)).astype(jnp.int32)
    large = jnp.minimum(large, half - 1)
    return ret + jnp.where(n < max_exact, n, large)


def _band_bias(rel_bias_table):
    j = jnp.arange(2 * Q_BLOCK)[:, None]
    r = jnp.arange(Q_BLOCK)[None, :]
    rel = j - Q_BLOCK - r
    d = (r // CHUNK) - ((j - Q_BLOCK) // CHUNK)
    ok = ((d >= 0) & (d <= WIN_CHUNKS)).astype(jnp.int32)
    bkt = _t5_bucket_map(rel).astype(jnp.int32)
    return pl.pallas_call(
        _band_bias_kernel,
        in_specs=[
            pl.BlockSpec(memory_space=pltpu.SMEM),
            pl.BlockSpec(memory_space=pltpu.VMEM),
            pl.BlockSpec(memory_space=pltpu.VMEM),
        ],
        out_specs=pl.BlockSpec(memory_space=pltpu.VMEM),
        out_shape=jax.ShapeDtypeStruct((2, B_KV_HEADS, 2 * Q_BLOCK, B_GROUP * Q_BLOCK), F32),
        name="band_bias",
    )(rel_bias_table, bkt, ok)


def _swa_layer_kernel(xn_ref, x_ref, wqT_ref, wg_ref, wo_ref, ks_ref, vsT_ref, bias_ref, sink_ref,
                      g_ref, b_ref, xo_ref, qT0_ref, sg0_ref, qT1_ref, sg1_ref, og_ref, s_ref,
                      *, nblk, nt):
    step = pl.program_id(0)

    @pl.when(step == 0)
    def _first_step():
        qT0_ref[...] = jnp.zeros(qT0_ref.shape, qT0_ref.dtype)
        sg0_ref[...] = jnp.zeros(sg0_ref.shape, sg0_ref.dtype)

    def body(qT_ref, sg_ref, qT_next_ref, sg_next_ref):
        tile = jnp.maximum(step - 1, 0) % nt
        xb_next = xn_ref[...].astype(BF16)
        zeros = jnp.zeros((B_HEAD_DIM, B_GROUP * Q_BLOCK), BF16)
        chunk = B_WIDTH // nblk
        half = max(nblk // 2, 1)

        for n in range(nblk):
            blk = tile * nblk + n
            cur = pl.multiple_of(blk * Q_BLOCK, Q_BLOCK)
            prev_blk = jnp.maximum(blk - 1, 0)
            prev = pl.multiple_of(prev_blk * Q_BLOCK, Q_BLOCK)
            kband = jnp.concatenate(
                [ks_ref[pl.ds(prev, Q_BLOCK), :], ks_ref[pl.ds(cur, Q_BLOCK), :]], axis=0)
            vband = jnp.concatenate([vsT_ref[prev_blk], vsT_ref[blk]], axis=1)
            table = jnp.minimum(blk, 1)
            lanes = slice(n * Q_BLOCK, (n + 1) * Q_BLOCK)
            score_max = []
            for g in range(B_KV_HEADS):
                qg = jnp.concatenate(
                    [qT_ref[(g * B_GROUP + hi) * B_HEAD_DIM:(g * B_GROUP + hi + 1) * B_HEAD_DIM, lanes]
                     for hi in range(B_GROUP)], axis=1)
                rhs = jnp.concatenate([qg, zeros] if g == 0 else [zeros, qg], axis=0)
                sT = jnp.dot(kband, rhs, preferred_element_type=F32) + bias_ref[table, g]
                s_ref[g] = sT
                score_max.append(jnp.max(sT, axis=0, keepdims=True))

            part = slice(n * chunk, (n + 1) * chunk)
            qT_next_ref[part, :] = (
                lax.dot_general(wqT_ref[part, :], xb_next, _NT, preferred_element_type=F32)
                * (B_HEAD_DIM ** -0.5 * LOG2_E)).astype(BF16)
            gate = jnp.dot(xb_next, wg_ref[:, part], preferred_element_type=F32)
            sg_next_ref[:, part] = gate * _sigmoid(gate)

            o_parts = []
            for g in range(B_KV_HEADS):
                sink = sink_ref[g]
                m = jnp.maximum(score_max[g], sink)
                e = jnp.exp2(s_ref[g] - m).astype(BF16)
                v_rows = slice(g * B_VROWS, (g + 1) * B_VROWS)
                oT = jnp.dot(vband[v_rows, :], e, preferred_element_type=F32)
                den = oT[B_HEAD_DIM:B_HEAD_DIM + 1] + jnp.exp2(sink - m)
                o_parts.append(oT[:B_HEAD_DIM] / den)
            for hi in range(B_GROUP):
                cols = slice(hi * Q_BLOCK, (hi + 1) * Q_BLOCK)
                o_hi = jnp.concatenate([o_parts[0][:, cols], o_parts[1][:, cols]], axis=0).T
                og_ref[lanes, cols] = (o_hi * sg_ref[lanes, cols]).astype(BF16)

            if (n + 1) % half == 0:
                rows = slice((n + 1 - half) * Q_BLOCK, (n + 1) * Q_BLOCK)
                y = jnp.dot(og_ref[rows, :], wo_ref[...], preferred_element_type=F32)
                xo_ref[rows, :] = _deepnorm_ln(x_ref[rows, :], y, g_ref[...], b_ref[...])

    @pl.when(step % 2 == 0)
    def _even():
        body(qT0_ref, sg0_ref, qT1_ref, sg1_ref)

    @pl.when(step % 2 == 1)
    def _odd():
        body(qT1_ref, sg1_ref, qT0_ref, sg0_ref)


def _swa_layer(x, j, params, ks, vsT, bias, tm):
    B, S, _ = x.shape
    nt = S // tm
    nblk = tm // Q_BLOCK
    once = pl.Buffered(1)
    last = B * nt - 1
    attended = lambda i: jnp.maximum(i - 1, 0)
    done = pl.BlockSpec((None, tm, D_MODEL), lambda i: (attended(i) // nt, attended(i) % nt, 0))
    weights = [params[k] for k in ("wqT", "wg", "w_o")]
    small = [params[k] for k in ("sink", "ln_g", "ln_b")]
    return pl.pallas_call(
        functools.partial(_swa_layer_kernel, nblk=nblk, nt=nt),
        grid=(B * nt + 1,),
        in_specs=[
            pl.BlockSpec((None, tm, D_MODEL),
                         lambda i: (jnp.minimum(i, last) // nt, jnp.minimum(i, last) % nt, 0)),
            done, *[_layer_block(w, j) for w in weights],
            pl.BlockSpec((None, S, B_KV_WIDTH), lambda i: (attended(i) // nt, 0, 0), pipeline_mode=once),
            pl.BlockSpec((None, S // Q_BLOCK, B_KV_HEADS * B_VROWS, Q_BLOCK),
                         lambda i: (attended(i) // nt, 0, 0, 0), pipeline_mode=once),
            pl.BlockSpec(bias.shape, lambda i: (0,) * bias.ndim, pipeline_mode=once),
            *[_layer_block(w, j) for w in small],
        ],
        out_specs=done,
        out_shape=jax.ShapeDtypeStruct(x.shape, F32),
        scratch_shapes=[
            pltpu.VMEM((B_WIDTH, tm), BF16), pltpu.VMEM((tm, B_WIDTH), F32),
            pltpu.VMEM((B_WIDTH, tm), BF16), pltpu.VMEM((tm, B_WIDTH), F32),
            pltpu.VMEM((tm, B_WIDTH), BF16),
            pltpu.VMEM((B_KV_HEADS, 2 * Q_BLOCK, B_GROUP * Q_BLOCK), F32),
        ],
        compiler_params=_cparams(1),
        name="swa_layer",
    )(x, x, *weights, ks, vsT, bias, *small)


def _rope_tables(S):
    inv = ROPE_THETA ** (-jnp.arange(ROPE_HALF, dtype=F32) / ROPE_HALF)
    ang = jnp.arange(S, dtype=F32)[:, None] * inv[None, :]
    cos, sin = jnp.cos(ang), jnp.sin(ang)
    zeros = jnp.zeros((S, LANES - A_ROPE), F32)
    z32 = jnp.zeros((S, ROPE_HALF), F32)
    cosk = jnp.concatenate([cos, cos, zeros], axis=1)
    sinka = jnp.concatenate([-sin, z32, zeros], axis=1)
    sinkb = jnp.concatenate([z32, sin, zeros], axis=1)
    return cos.T, sin.T, cosk, sinka, sinkb


def _mla_params(w_in, q_norm, kv_norm, w_uq, w_ukv, w_o, ln_gain, ln_bias):
    n = w_in.shape[0]
    o_kr = A_Q_LORA + A_KV_LORA
    w_in_pad = jnp.concatenate(
        [w_in[..., :o_kr + A_ROPE], jnp.zeros((n, D_MODEL, LANES - A_ROPE), w_in.dtype),
         w_in[..., o_kr + A_ROPE:]], axis=-1)
    w_ukv_h = w_ukv.reshape(n, A_KV_LORA, A_HEADS, A_NOPE + A_VDIM)
    w_kn = w_ukv_h[..., :A_NOPE].reshape(n, A_KV_LORA, A_HEADS * A_NOPE)
    w_v = w_ukv_h[..., A_NOPE:].reshape(n, A_KV_LORA, A_WIDTH)
    return {
        "w_in": w_in_pad.astype(BF16), "gq": q_norm[:, None, :], "gkv": kv_norm[:, None, :],
        "w_uqT": jnp.swapaxes(w_uq, 1, 2).astype(BF16), "w_kn": w_kn.astype(BF16),
        "w_vT": jnp.swapaxes(w_v, 1, 2).astype(BF16), "w_o": w_o.astype(BF16),
        "ln_g": ln_gain[:n, None, :], "ln_b": ln_bias[:n, None, :],
    }


def _swa_feature_order(w, axis):
    shape = w.shape
    split = shape[:axis] + (B_KV_HEADS, B_GROUP, B_HEAD_DIM) + shape[axis + 1:]
    return jnp.swapaxes(w.reshape(split), axis, axis + 1).reshape(shape)


def _swa_params(w_in, sinks, w_o, ln_gain, ln_bias):
    n = w_in.shape[0]
    sink = sinks.astype(F32).reshape(n, B_KV_HEADS, 1, B_GROUP) * LOG2_E
    return {
        "wqT": jnp.swapaxes(w_in[..., :B_WIDTH], 1, 2).astype(BF16),
        "wg": _swa_feature_order(w_in[..., B_WIDTH:], 2).astype(BF16),
        "w_o": _swa_feature_order(w_o, 1).astype(BF16),
        "sink": jnp.repeat(sink, Q_BLOCK, axis=3),
        "ln_g": ln_gain[-n:, None, :], "ln_b": ln_bias[-n:, None, :],
    }


def kernel(x, w_in_a, q_norm_a, kv_norm_a, w_uq_a, w_ukv_a, w_o_a, w_kv_shared, w_in_b, sinks_b,
           w_o_b, rel_bias_table, ln_gain, ln_bias):
    B, S, _ = x.shape
    tm = min(512, S)
    tq = min(512, S)
    rope = _rope_tables(S)

    mla = _mla_params(w_in_a, q_norm_a, kv_norm_a, w_uq_a, w_ukv_a, w_o_a, ln_gain, ln_bias)
    prev = None
    for i in range(N_A_LAYERS):
        outs = _mla_proj(x, i, mla, rope, tm, tq, tq, prev)
        if prev is not None:
            x, *outs = outs
        qT, kn, kr, vT, sg = outs
        prev = (_mla_attn(qT, kn, kr, vT), sg)
    x, ks, vsT = _out_ln_kv(*prev, x, N_A_LAYERS - 1, mla, w_kv_shared[:, :B_KV_WIDTH].astype(BF16),
                            w_kv_shared[:, B_KV_WIDTH:].T.astype(BF16), tm)

    bias = _band_bias(rel_bias_table)
    swa = _swa_params(w_in_b, sinks_b, w_o_b, ln_gain, ln_bias)
    for j in range(DEPTH - N_A_LAYERS):
        x = _swa_layer(x, j, swa, ks, vsT, bias, min(512, S))
    return x
```
